```python
import math
import jax, jax.numpy as jnp
from jax import lax
import numpy as np

D_MODEL = 2048
BATCH = 4
SEQ = 2048
DEPTH = 1

CTX_LEN = 256
GRID_W = 64

HEAD_DIM = 128
DA_HEADS = 8
DA_HALF = HEAD_DIM // 2
MLA_HEADS = 8
MLA_NOPE = 128
MLA_ROPE = 64
MLA_V = 128
Q_RANK = 384
KV_RANK = 256
ROPE_DIM = 64
ROPE_BASE = 10000.0
D_FF = 5632
CONV_W = 3
N_MOD = 6
EPS = 1e-6
Q_BLOCK = 128

DA_WIDTH = DA_HEADS * HEAD_DIM
MLA_WIDTH = MLA_HEADS * MLA_V
MIX_WIDTH = DA_WIDTH + MLA_WIDTH
MLA_QK = MLA_NOPE + MLA_ROPE
IN_WIDTH = 3 * DA_WIDTH + Q_RANK + KV_RANK + MLA_ROPE
DA_SCALE = 1.0 / math.sqrt(DA_HALF)
MLA_SCALE = 1.0 / math.sqrt(MLA_QK)

kernel_name = "hybrid_diffattn_mla_convffn_dit_layer"


def _rmsnorm(x, w):
    xf = x.astype(jnp.float32)
    y = xf * lax.rsqrt(jnp.mean(xf * xf, axis=-1, keepdims=True) + EPS)
    return y.astype(x.dtype) * w


def _modulate(h, shift, scale):
    return h * (1.0 + scale) + shift


def _axial_rope_tables(n_tokens):
    rows = n_tokens // GRID_W
    row = jnp.broadcast_to(jnp.arange(rows)[:, None], (rows, GRID_W)).reshape(-1).astype(jnp.float32)
    col = jnp.broadcast_to(jnp.arange(GRID_W)[None, :], (rows, GRID_W)).reshape(-1).astype(jnp.float32)
    nf = ROPE_DIM // 4
    inv = ROPE_BASE ** (-jnp.arange(nf, dtype=jnp.float32) / nf)
    ang_r = row[:, None] * inv
    ang_c = col[:, None] * inv
    return (jnp.cos(ang_r), jnp.sin(ang_r), jnp.cos(ang_c), jnp.sin(ang_c))


def _rot(x, cos, sin):
    cos = cos.astype(x.dtype)
    sin = sin.astype(x.dtype)
    x1, x2 = jnp.split(x, 2, axis=-1)
    return jnp.concatenate([x1 * cos - x2 * sin, x2 * cos + x1 * sin], axis=-1)


def _axial_rope(x, tabs):
    cos_r, sin_r, cos_c, sin_c = tabs
    xr, xc = jnp.split(x, 2, axis=-1)
    return jnp.concatenate([_rot(xr, cos_r, sin_r), _rot(xc, cos_c, sin_c)], axis=-1)


def _mixer_inputs(h, tabs, w_in, q_norm_w, kv_norm_w, w_uq, w_ukv):
    B, T, _ = h.shape
    p = h @ w_in
    o1, o2, o3 = DA_WIDTH, 2 * DA_WIDTH, 3 * DA_WIDTH
    o4 = o3 + Q_RANK
    o5 = o4 + KV_RANK

    def heads(a, n):
        return a.reshape(B, T, n, -1).transpose(0, 2, 1, 3)

    q_da = heads(p[..., :o1], DA_HEADS)
    k_da = heads(p[..., o1:o2], DA_HEADS)
    v_da = heads(p[..., o2:o3], DA_HEADS)
    c_q = _rmsnorm(p[..., o3:o4], q_norm_w)
    c_kv = _rmsnorm(p[..., o4:o5], kv_norm_w)
    k_rope = p[..., o5:][:, None]
    q_mla = heads(c_q @ w_uq, MLA_HEADS)
    kv = heads(c_kv @ w_ukv, MLA_HEADS)
    q_nope, q_rope = q_mla[..., :MLA_NOPE], q_mla[..., MLA_NOPE:]
    k_nope, v_mla = kv[..., :MLA_NOPE], kv[..., MLA_NOPE:]
    if tabs is not None:
        q_da = jnp.concatenate([_axial_rope(q_da[..., :DA_HALF], tabs), _axial_rope(q_da[..., DA_HALF:], tabs)], -1)
        k_da = jnp.concatenate([_axial_rope(k_da[..., :DA_HALF], tabs), _axial_rope(k_da[..., DA_HALF:], tabs)], -1)
        q_rope = _axial_rope(q_rope, tabs)
        k_rope = _axial_rope(k_rope, tabs)
    q_mla = jnp.concatenate([q_nope, q_rope], axis=-1)
    k_mla = jnp.concatenate([k_nope, jnp.broadcast_to(k_rope, (B, MLA_HEADS, T, MLA_ROPE))], axis=-1)
    return q_da, k_da, v_da, q_mla, k_mla, v_mla


def _diff_attend(q, k, v, lam):
    q1, q2 = jnp.split(q, 2, axis=-1)
    k1, k2 = jnp.split(k, 2, axis=-1)
    s1 = jnp.einsum('bhqd,bhkd->bhqk', q1, k1).astype(jnp.float32) * DA_SCALE
    s2 = jnp.einsum('bhqd,bhkd->bhqk', q2, k2).astype(jnp.float32) * DA_SCALE
    w = jax.nn.softmax(s1, axis=-1) - lam * jax.nn.softmax(s2, axis=-1)
    return jnp.einsum('bhqk,bhkd->bhqd', w.astype(v.dtype), v)


def _softmax_attend(q, k, v, scale):
    s = jnp.einsum('bhqd,bhkd->bhqk', q, k).astype(jnp.float32) * scale
    p = jax.nn.softmax(s, axis=-1)
    return jnp.einsum('bhqk,bhkd->bhqd', p.astype(v.dtype), v)


def _sweep(block_fn, q):
    B, H, S, d = q.shape
    nb = S // Q_BLOCK
    qb = q.reshape(B, H, nb, Q_BLOCK, d).transpose(2, 0, 1, 3, 4)
    out = lax.map(block_fn, qb)
    return out.transpose(1, 2, 0, 3, 4).reshape(B, H, S, out.shape[-1])


def _merge_heads(o_da, o_mla, subln_w, lambda_init):
    o_da = _rmsnorm(o_da, subln_w) * (1.0 - lambda_init)
    B, _, T, _ = o_da.shape
    o_da = o_da.transpose(0, 2, 1, 3).reshape(B, T, DA_WIDTH)
    o_mla = o_mla.transpose(0, 2, 1, 3).reshape(B, T, MLA_WIDTH)
    return jnp.concatenate([o_da, o_mla], axis=-1)


def _conv_ffn(h, w_up, conv_w, conv_b, w_down):
    g, u = jnp.split(h @ w_up, 2, axis=-1)
    T = g.shape[1]
    pad = CONV_W // 2
    gp = jnp.pad(g, ((0, 0), (pad, pad), (0, 0)))
    g = sum(gp[:, j:j + T] * conv_w[j] for j in range(CONV_W)) + conv_b
    return (jax.nn.silu(g) * u) @ w_down


def setup_inputs(seed: int = 0) -> dict:
    key = jax.random.key(seed)
    ks = jax.random.split(key, 24)
    f32 = jnp.float32

    def nrm(k, shape, fan_in, gain=1.0):
        return (gain * fan_in ** -0.5) * jax.random.normal(k, shape, f32)

    def gain(k, shape):
        return 1.0 + 0.05 * jax.random.normal(k, shape, f32)

    L = DEPTH
    return {
        "x": jax.random.normal(ks[0], (BATCH, SEQ, D_MODEL), f32),
        "c": jax.random.normal(ks[1], (BATCH, D_MODEL), f32),
        "ctx": jax.random.normal(ks[2], (BATCH, CTX_LEN, D_MODEL), f32),
        "c_ctx": jax.random.normal(ks[3], (D_MODEL,), f32),
        "w_ada": nrm(ks[4], (L, D_MODEL, N_MOD * D_MODEL), D_MODEL, 0.5),
        "b_ada": 0.02 * jax.random.normal(ks[5], (L, N_MOD * D_MODEL), f32),
        "norm1_w": gain(ks[6], (L, D_MODEL)),
        "w_in": nrm(ks[7], (L, D_MODEL, IN_WIDTH), D_MODEL),
        "q_norm_w": gain(ks[8], (L, Q_RANK)),
        "kv_norm_w": gain(ks[9], (L, KV_RANK)),
        "w_uq": nrm(ks[10], (L, Q_RANK, MLA_HEADS * MLA_QK), Q_RANK),
        "w_ukv": nrm(ks[11], (L, KV_RANK, MLA_HEADS * (MLA_NOPE + MLA_V)), KV_RANK),
        "lambda_q1": 0.1 * jax.random.normal(ks[12], (L, DA_HALF), f32),
        "lambda_k1": 0.1 * jax.random.normal(ks[13], (L, DA_HALF), f32),
        "lambda_q2": 0.1 * jax.random.normal(ks[14], (L, DA_HALF), f32),
        "lambda_k2": 0.1 * jax.random.normal(ks[15], (L, DA_HALF), f32),
        "subln_w": gain(ks[16], (L, HEAD_DIM)),
        "w_o": nrm(ks[17], (L, MIX_WIDTH, D_MODEL), MIX_WIDTH),
        "norm2_w": gain(ks[18], (L, D_MODEL)),
        "w_up": nrm(ks[19], (L, D_MODEL, 2 * D_FF), D_MODEL),
        "conv_w": nrm(ks[20], (L, CONV_W, D_FF), CONV_W),
        "conv_b": 0.02 * jax.random.normal(ks[21], (L, D_FF), f32),
        "w_down": nrm(ks[22], (L, D_FF, D_MODEL), D_FF),
        "final_w": gain(ks[23], (D_MODEL,)),
    }


def reference(x, c, ctx, c_ctx, w_ada, b_ada, norm1_w, w_in, q_norm_w, kv_norm_w, w_uq, w_ukv,
              lambda_q1, lambda_k1, lambda_q2, lambda_k2, subln_w, w_o, norm2_w, w_up,
              conv_w, conv_b, w_down, final_w):
    B, S, D = x.shape
    tabs = _axial_rope_tables(S)
    xc = jnp.broadcast_to(ctx, ctx.shape)
    sc_ = jax.nn.silu(c)
    scc = jax.nn.silu(c_ctx)
    for l in range(DEPTH):
        mod_x = (sc_ @ w_ada[l] + b_ada[l])[:, None, :]
        mod_c = (scc @ w_ada[l] + b_ada[l])[None, None, :]
        sh1, s1, g1, sh2, s2, g2 = jnp.split(mod_x, N_MOD, axis=-1)
        sh1c, s1c, g1c, sh2c, s2c, g2c = jnp.split(mod_c, N_MOD, axis=-1)
        lambda_init = 0.8 - 0.6 * math.exp(-0.3 * l)
        lam = (jnp.exp(jnp.sum(lambda_q1[l].astype(jnp.float32) * lambda_k1[l].astype(jnp.float32)))
               - jnp.exp(jnp.sum(lambda_q2[l].astype(jnp.float32) * lambda_k2[l].astype(jnp.float32)))
               + lambda_init)
        proj = (w_in[l], q_norm_w[l], kv_norm_w[l], w_uq[l], w_ukv[l])

        h = _modulate(_rmsnorm(x, norm1_w[l]), sh1, s1)
        hc = _modulate(_rmsnorm(xc, norm1_w[l]), sh1c, s1c)
        q_da, k_da, v_da, q_mla, k_mla, v_mla = _mixer_inputs(h, tabs, *proj)
        qc_da, kc_da, vc_da, qc_mla, kc_mla, vc_mla = _mixer_inputs(hc, None, *proj)
        k_da_all = jnp.concatenate([kc_da, k_da], axis=2)
        v_da_all = jnp.concatenate([vc_da, v_da], axis=2)
        k_mla_all = jnp.concatenate([kc_mla, k_mla], axis=2)
        v_mla_all = jnp.concatenate([vc_mla, v_mla], axis=2)
        o_da = _sweep(lambda qb: _diff_attend(qb, k_da_all, v_da_all, lam), q_da)
        o_mla = _sweep(lambda qb: _softmax_attend(qb, k_mla_all, v_mla_all, MLA_SCALE), q_mla)
        x = x + g1 * (_merge_heads(o_da, o_mla, subln_w[l], lambda_init) @ w_o[l])

        h2 = _modulate(_rmsnorm(x, norm2_w[l]), sh2, s2)
        x = x + g2 * _conv_ffn(h2, w_up[l], conv_w[l], conv_b[l], w_down[l])

        if l < DEPTH - 1:
            oc_da = _diff_attend(qc_da, kc_da, vc_da, lam)
            oc_mla = _softmax_attend(qc_mla, kc_mla, vc_mla, MLA_SCALE)
            xc = xc + g1c * (_merge_heads(oc_da, oc_mla, subln_w[l], lambda_init) @ w_o[l])
            hc2 = _modulate(_rmsnorm(xc, norm2_w[l]), sh2c, s2c)
            xc = xc + g2c * _conv_ffn(hc2, w_up[l], conv_w[l], conv_b[l], w_down[l])
    return _rmsnorm(x, final_w)
```

```python
import functools
import math

import jax
import jax.numpy as jnp
from jax import lax
from jax.experimental import pallas as pl
from jax.experimental.pallas import tpu as pltpu

D_MODEL = 2048
SEQ = 2048
CTX_LEN = 256
GRID_W = 64
HEAD_DIM = 128
DA_HEADS = 8
DA_HALF = 64
MLA_HEADS = 8
MLA_NOPE = 128
MLA_ROPE = 64
MLA_V = 128
Q_RANK = 384
KV_RANK = 256
ROPE_DIM = 64
ROPE_BASE = 10000.0
D_FF = 5632
CONV_W = 3
N_MOD = 6
EPS = 1e-6
DA_WIDTH = DA_HEADS * HEAD_DIM
MLA_WIDTH = MLA_HEADS * MLA_V
MLA_QK = MLA_NOPE + MLA_ROPE
MLA_QK_PAD = 256
DA_SCALE = 1.0 / math.sqrt(DA_HALF)
MLA_SCALE = 1.0 / math.sqrt(MLA_QK)
LAMBDA_INIT = 0.8 - 0.6 * math.exp(-0.3 * 0)
T_ALL = CTX_LEN + SEQ

LANES = 128
MOD_ROWS = 8
CTX_ROW = 4
VMEM_LIMIT = 56 * 1024 * 1024

TOK_TILE = 256
ADA_TN = 1024
ATT_TQ = 256
OUT_TM = 512
FFN_TM = 1024
FFN_TF = 256
FFN_HALO = 16
FIN_TM = 512

F32 = jnp.float32
BF16 = jnp.bfloat16


def _dot(a, b):
    return jnp.dot(a, b, preferred_element_type=F32)


def _dot_nt(a, b):
    return lax.dot_general(a, b, (((1,), (1,)), ((), ())), preferred_element_type=F32)


def _rms(x):
    return x * lax.rsqrt(jnp.mean(x * x, axis=-1, keepdims=True) + EPS)


def _cparams(n_grid):
    return pltpu.CompilerParams(dimension_semantics=("arbitrary",) * n_grid,
                                vmem_limit_bytes=VMEM_LIMIT)


def _ada_kernel(c_ref, w_ref, b_ref, o_ref):
    c = c_ref[...]
    sc = c * (1.0 / (1.0 + jnp.exp(-c)))
    o_ref[...] = _dot(sc.astype(BF16), w_ref[...].astype(BF16)) + b_ref[...]


def _ada(c8, w_ada, b_ada):
    n = w_ada.shape[1]
    return pl.pallas_call(
        _ada_kernel,
        grid=(n // ADA_TN,),
        in_specs=[pl.BlockSpec((MOD_ROWS, D_MODEL), lambda j: (0, 0)),
                  pl.BlockSpec((D_MODEL, ADA_TN), lambda j: (0, j)),
                  pl.BlockSpec((1, ADA_TN), lambda j: (0, j))],
        out_specs=pl.BlockSpec((MOD_ROWS, ADA_TN), lambda j: (0, j)),
        out_shape=jax.ShapeDtypeStruct((MOD_ROWS, n), F32),
        compiler_params=_cparams(1),
        name="ada",
    )(c8, w_ada, b_ada)


def _rope_chunk(xc, cos, sin, lo_mask):
    up = pltpu.roll(xc, LANES - 16, 1)
    dn = pltpu.roll(xc, 16, 1)
    return xc * cos + jnp.where(lo_mask, up, dn) * sin


def _mixer_kernel(ctx_ref, x_ref, sh_ref, sc_ref, n1_ref, cos_ref, sin_ref, win_ref,
                  qn_ref, kvn_ref, wuq_ref, wukv_ref,
                  qda_ref, kda_ref, vda_ref, qm_ref, km_ref, vm_ref):
    t = pl.program_id(1)
    is_ctx = t == 0
    xt = jnp.where(is_ctx, ctx_ref[0], x_ref[0])
    h = _rms(xt) * n1_ref[...]
    h = h * (1.0 + sc_ref[0]) + sh_ref[0]
    hb = h.astype(BF16)
    cos = jnp.where(is_ctx, 1.0, cos_ref[...])
    sin = jnp.where(is_ctx, 0.0, sin_ref[...])
    lane = lax.broadcasted_iota(jnp.int32, (TOK_TILE, LANES), 1)
    lo_mask = (lane % 32) < 16
    o1, o2, o3 = DA_WIDTH, 2 * DA_WIDTH, 3 * DA_WIDTH
    o4 = o3 + Q_RANK
    o5 = o4 + KV_RANK

    k = _dot(hb, win_ref[:, o1:o2])
    for hh in range(DA_HEADS):
        sl = slice(hh * LANES, (hh + 1) * LANES)
        kda_ref[0, :, sl] = _rope_chunk(k[:, sl], cos, sin, lo_mask).astype(BF16)
    vda_ref[0] = _dot(hb, win_ref[:, o2:o3]).astype(BF16)

    ckv = _rms(_dot(hb, win_ref[:, o4:o5])) * kvn_ref[...]
    kv = _dot(ckv.astype(BF16), wukv_ref[...])
    kr = _rope_chunk(_dot(hb, win_ref[:, o5:o5 + LANES]), cos, sin, lo_mask).astype(BF16)
    for hh in range(MLA_HEADS):
        base = hh * MLA_QK_PAD
        km_ref[0, :, base:base + LANES] = kv[:, hh * LANES:(hh + 1) * LANES].astype(BF16)
        km_ref[0, :, base + LANES:base + 2 * LANES] = kr
    vm_ref[0] = kv[:, MLA_WIDTH:].astype(BF16)

    @pl.when(t > 0)
    def _():
        q = _dot(hb, win_ref[:, 0:o1])
        for hh in range(DA_HEADS):
            sl = slice(hh * LANES, (hh + 1) * LANES)
            qda_ref[0, :, sl] = (_rope_chunk(q[:, sl], cos, sin, lo_mask) * DA_SCALE).astype(BF16)
        cq = _rms(_dot(hb, win_ref[:, o3:o4])) * qn_ref[...]
        qm = _dot(cq.astype(BF16), wuq_ref[...])
        for hh in range(MLA_HEADS):
            base = hh * MLA_QK_PAD
            qm_ref[0, :, base:base + LANES] = (qm[:, base:base + LANES] * MLA_SCALE).astype(BF16)
            qr = _rope_chunk(qm[:, base + LANES:base + 2 * LANES], cos, sin, lo_mask)
            qm_ref[0, :, base + LANES:base + 2 * LANES] = (qr * MLA_SCALE).astype(BF16)


def _mixer_inputs(ctx, x, mod3, n1, cos_t, sin_t, win_b, qn, kvn, wuq_b, wukv_b):
    B = x.shape[0]
    nt = T_ALL // TOK_TILE

    def lat(t):
        return jnp.maximum(t - 1, 0)

    def const(shape):
        return pl.BlockSpec(shape, lambda b, t: (0,) * len(shape), pipeline_mode=pl.Buffered(1))

    in_specs = [
        pl.BlockSpec((1, CTX_LEN, D_MODEL), lambda b, t: (b, 0, 0)),
        pl.BlockSpec((1, TOK_TILE, D_MODEL), lambda b, t: (b, lat(t), 0)),
        pl.BlockSpec((1, 1, D_MODEL), lambda b, t: (jnp.where(t == 0, CTX_ROW, b), 0, 0)),
        pl.BlockSpec((1, 1, D_MODEL), lambda b, t: (jnp.where(t == 0, CTX_ROW, b), 0, 1)),
        const((1, D_MODEL)),
        pl.BlockSpec((TOK_TILE, LANES), lambda b, t: (lat(t), 0)),
        pl.BlockSpec((TOK_TILE, LANES), lambda b, t: (lat(t), 0)),
        const(win_b.shape),
        const((1, Q_RANK)),
        const((1, KV_RANK)),
        const(wuq_b.shape),
        const(wukv_b.shape),
    ]
    out_specs = [
        pl.BlockSpec((1, TOK_TILE, DA_WIDTH), lambda b, t: (b, lat(t), 0)),
        pl.BlockSpec((1, TOK_TILE, DA_WIDTH), lambda b, t: (b, t, 0)),
        pl.BlockSpec((1, TOK_TILE, DA_WIDTH), lambda b, t: (b, t, 0)),
        pl.BlockSpec((1, TOK_TILE, MLA_HEADS * MLA_QK_PAD), lambda b, t: (b, lat(t), 0)),
        pl.BlockSpec((1, TOK_TILE, MLA_HEADS * MLA_QK_PAD), lambda b, t: (b, t, 0)),
        pl.BlockSpec((1, TOK_TILE, MLA_WIDTH), lambda b, t: (b, t, 0)),
    ]
    out_shape = [
        jax.ShapeDtypeStruct((B, SEQ, DA_WIDTH), BF16),
        jax.ShapeDtypeStruct((B, T_ALL, DA_WIDTH), BF16),
        jax.ShapeDtypeStruct((B, T_ALL, DA_WIDTH), BF16),
        jax.ShapeDtypeStruct((B, SEQ, MLA_HEADS * MLA_QK_PAD), BF16),
        jax.ShapeDtypeStruct((B, T_ALL, MLA_HEADS * MLA_QK_PAD), BF16),
        jax.ShapeDtypeStruct((B, T_ALL, MLA_WIDTH), BF16),
    ]
    return pl.pallas_call(
        _mixer_kernel,
        grid=(B, nt),
        in_specs=in_specs,
        out_specs=out_specs,
        out_shape=out_shape,
        compiler_params=_cparams(2),
        name="mixer_in",
    )(ctx, x, mod3, mod3, n1, cos_t, sin_t, win_b, qn, kvn, wuq_b, wukv_b)


def _softmax_parts(s):
    m = jnp.max(s, axis=-1, keepdims=True)
    p = jnp.exp(s - m)
    return p, jnp.sum(p, axis=-1, keepdims=True)


def _diff_attn_kernel(q_ref, k_ref, v_ref, lq1_ref, lk1_ref, lq2_ref, lk2_ref, sub_ref, o_ref):
    q = q_ref[0]
    k = k_ref[0]
    lane = lax.broadcasted_iota(jnp.int32, q.shape, 1)
    zero = jnp.zeros_like(q)
    q1 = jnp.where(lane < DA_HALF, q, zero)
    q2 = jnp.where(lane >= DA_HALF, q, zero)
    p1, l1 = _softmax_parts(_dot_nt(q1, k))
    p2, l2 = _softmax_parts(_dot_nt(q2, k))
    lam = (jnp.exp(jnp.sum(lq1_ref[...] * lk1_ref[...], axis=-1, keepdims=True))
           - jnp.exp(jnp.sum(lq2_ref[...] * lk2_ref[...], axis=-1, keepdims=True))
           + LAMBDA_INIT)
    w = p1 * (1.0 / l1) - p2 * (lam / l2)
    o = _dot(w.astype(BF16), v_ref[0])
    o = _rms(o) * sub_ref[...] * (1.0 - LAMBDA_INIT)
    o_ref[0] = o.astype(BF16)


def _diff_attn(q, k, v, lq1, lk1, lq2, lk2, subln):
    B = q.shape[0]
    small = lambda n: pl.BlockSpec((1, n), lambda b, h, i: (0, 0))
    return pl.pallas_call(
        _diff_attn_kernel,
        grid=(B, DA_HEADS, SEQ // ATT_TQ),
        in_specs=[pl.BlockSpec((1, ATT_TQ, HEAD_DIM), lambda b, h, i: (b, i, h)),
                  pl.BlockSpec((1, T_ALL, HEAD_DIM), lambda b, h, i: (b, 0, h)),
                  pl.BlockSpec((1, T_ALL, HEAD_DIM), lambda b, h, i: (b, 0, h)),
                  small(DA_HALF), small(DA_HALF), small(DA_HALF), small(DA_HALF),
                  small(HEAD_DIM)],
        out_specs=pl.BlockSpec((1, ATT_TQ, HEAD_DIM), lambda b, h, i: (b, i, h)),
        out_shape=jax.ShapeDtypeStruct((B, SEQ, DA_WIDTH), BF16),
        compiler_params=_cparams(3),
        name="diff_attn",
    )(q, k, v, lq1, lk1, lq2, lk2, subln)


def _mla_attn_kernel(q_ref, k_ref, v_ref, o_ref):
    p, l = _softmax_parts(_dot_nt(q_ref[0], k_ref[0]))
    o = _dot(p.astype(BF16), v_ref[0]) * (1.0 / l)
    o_ref[0] = o.astype(BF16)


def _mla_attn(q, k, v):
    B = q.shape[0]
    return pl.pallas_call(
        _mla_attn_kernel,
        grid=(B, MLA_HEADS, SEQ // ATT_TQ),
        in_specs=[pl.BlockSpec((1, ATT_TQ, MLA_QK_PAD), lambda b, h, i: (b, i, h)),
                  pl.BlockSpec((1, T_ALL, MLA_QK_PAD), lambda b, h, i: (b, 0, h)),
                  pl.BlockSpec((1, T_ALL, MLA_V), lambda b, h, i: (b, 0, h))],
        out_specs=pl.BlockSpec((1, ATT_TQ, MLA_V), lambda b, h, i: (b, i, h)),
        out_shape=jax.ShapeDtypeStruct((B, SEQ, MLA_WIDTH), BF16),
        compiler_params=_cparams(3),
        name="mla_attn",
    )(q, k, v)


def _out_proj_kernel(oda_ref, omla_ref, wo_ref, x_ref, g1_ref, sh2_ref, sc2_ref, n2_ref,
                     x1_ref, h2_ref):
    y = _dot(oda_ref[0], wo_ref[0:DA_WIDTH, :]) + _dot(omla_ref[0], wo_ref[DA_WIDTH:, :])
    x1 = x_ref[0] + g1_ref[0] * y
    x1_ref[0] = x1
    h2 = _rms(x1) * n2_ref[...]
    h2_ref[0] = (h2 * (1.0 + sc2_ref[0]) + sh2_ref[0]).astype(BF16)


def _out_proj(o_da, o_mla, wo_b, x, mod3, n2):
    B = x.shape[0]
    tile = lambda w: pl.BlockSpec((1, OUT_TM, w), lambda b, i: (b, i, 0))
    modrow = lambda col: pl.BlockSpec((1, 1, D_MODEL), lambda b, i: (b, 0, col))
    return pl.pallas_call(
        _out_proj_kernel,
        grid=(B, SEQ // OUT_TM),
        in_specs=[tile(DA_WIDTH), tile(MLA_WIDTH),
                  pl.BlockSpec(wo_b.shape, lambda b, i: (0, 0), pipeline_mode=pl.Buffered(1)),
                  tile(D_MODEL), modrow(2), modrow(3), modrow(4),
                  pl.BlockSpec((1, D_MODEL), lambda b, i: (0, 0))],
        out_specs=[tile(D_MODEL), tile(D_MODEL)],
        out_shape=[jax.ShapeDtypeStruct((B, SEQ, D_MODEL), F32),
                   jax.ShapeDtypeStruct((B, SEQ, D_MODEL), BF16)],
        compiler_params=_cparams(2),
        name="out_proj",
    )(o_da, o_mla, wo_b, x, mod3, mod3, mod3, n2)


def _ffn_kernel(h_ref, top_ref, bot_ref, wg_ref, wu_ref, cw_ref, cb_ref, wd_ref, o_ref,
                hs_ref, g_ref):
    i = pl.program_id(1)
    j = pl.program_id(2)
    last_i = pl.num_programs(1) - 1

    @pl.when(j == 0)
    def _():
        top = top_ref[0]
        bot = bot_ref[0]
        hs_ref[0:FFN_HALO, :] = jnp.where(i == 0, jnp.zeros_like(top), top)
        hs_ref[FFN_HALO:FFN_HALO + FFN_TM, :] = h_ref[0]
        hs_ref[FFN_HALO + FFN_TM:, :] = jnp.where(i == last_i, jnp.zeros_like(bot), bot)

    g_ref[...] = _dot(hs_ref[...], wg_ref[...].astype(BF16))
    u = _dot(hs_ref[FFN_HALO:FFN_HALO + FFN_TM, :], wu_ref[...].astype(BF16))
    lo = FFN_HALO
    gc = (g_ref[lo - 1:lo - 1 + FFN_TM, :] * cw_ref[0:1, :]
          + g_ref[lo:lo + FFN_TM, :] * cw_ref[1:2, :]
          + g_ref[lo + 1:lo + 1 + FFN_TM, :] * cw_ref[2:3, :]
          + cb_ref[...])
    act = gc * (1.0 / (1.0 + jnp.exp(-gc))) * u
    part = _dot(act.astype(BF16), wd_ref[...].astype(BF16))

    @pl.when(j == 0)
    def _():
        o_ref[0] = part

    @pl.when(j > 0)
    def _():
        o_ref[0] += part


def _conv_ffn(h2, w_up, conv_w, conv_b, w_down):
    B = h2.shape[0]
    nf = D_FF // FFN_TF
    ni = SEQ // FFN_TM
    halo_per_tile = FFN_TM // FFN_HALO
    n_halo = SEQ // FFN_HALO
    return pl.pallas_call(
        _ffn_kernel,
        grid=(B, ni, nf),
        in_specs=[
            pl.BlockSpec((1, FFN_TM, D_MODEL), lambda b, i, j: (b, i, 0)),
            pl.BlockSpec((1, FFN_HALO, D_MODEL),
                         lambda b, i, j: (b, jnp.maximum(i * halo_per_tile - 1, 0), 0)),
            pl.BlockSpec((1, FFN_HALO, D_MODEL),
                         lambda b, i, j: (b, jnp.minimum((i + 1) * halo_per_tile, n_halo - 1), 0)),
            pl.BlockSpec((D_MODEL, FFN_TF), lambda b, i, j: (0, j)),
            pl.BlockSpec((D_MODEL, FFN_TF), lambda b, i, j: (0, nf + j)),
            pl.BlockSpec((CONV_W, FFN_TF), lambda b, i, j: (0, j)),
            pl.BlockSpec((1, FFN_TF), lambda b, i, j: (0, j)),
            pl.BlockSpec((FFN_TF, D_MODEL), lambda b, i, j: (j, 0)),
        ],
        out_specs=pl.BlockSpec((1, FFN_TM, D_MODEL), lambda b, i, j: (b, i, 0)),
        out_shape=jax.ShapeDtypeStruct((B, SEQ, D_MODEL), F32),
        scratch_shapes=[pltpu.VMEM((FFN_TM + 2 * FFN_HALO, D_MODEL), BF16),
                        pltpu.VMEM((FFN_TM + 2 * FFN_HALO, FFN_TF), F32)],
        compiler_params=_cparams(3),
        name="conv_ffn",
    )(h2, h2, h2, w_up, w_up, conv_w, conv_b, w_down)


def _final_kernel(x1_ref, y_ref, g2_ref, fw_ref, o_ref):
    x2 = x1_ref[0] + g2_ref[0] * y_ref[0]
    o_ref[0] = _rms(x2) * fw_ref[...]


def _final(x1, y, mod3, final_w):
    B = x1.shape[0]
    tile = pl.BlockSpec((1, FIN_TM, D_MODEL), lambda b, i: (b, i, 0))
    return pl.pallas_call(
        _final_kernel,
        grid=(B, SEQ // FIN_TM),
        in_specs=[tile, tile,
                  pl.BlockSpec((1, 1, D_MODEL), lambda b, i: (b, 0, 5)),
                  pl.BlockSpec((1, D_MODEL), lambda b, i: (0, 0))],
        out_specs=tile,
        out_shape=jax.ShapeDtypeStruct((B, SEQ, D_MODEL), F32),
        compiler_params=_cparams(2),
        name="final",
    )(x1, y, mod3, final_w)


def _rope_tables():
    pos = jnp.arange(SEQ)
    row = (pos // GRID_W).astype(F32)
    col = (pos % GRID_W).astype(F32)
    nf = ROPE_DIM // 4
    inv = ROPE_BASE ** (-jnp.arange(nf, dtype=F32) / nf)
    ar = row[:, None] * inv
    ac = col[:, None] * inv
    cos = jnp.concatenate([jnp.cos(ar), jnp.cos(ar), jnp.cos(ac), jnp.cos(ac)], axis=-1)
    sin = jnp.concatenate([-jnp.sin(ar), jnp.sin(ar), -jnp.sin(ac), jnp.sin(ac)], axis=-1)
    return jnp.tile(cos, (1, LANES // ROPE_DIM)), jnp.tile(sin, (1, LANES // ROPE_DIM))


def kernel(x, c, ctx, c_ctx, w_ada, b_ada, norm1_w, w_in, q_norm_w, kv_norm_w, w_uq, w_ukv,
           lambda_q1, lambda_k1, lambda_q2, lambda_k2, subln_w, w_o, norm2_w, w_up,
           conv_w, conv_b, w_down, final_w):
    B = x.shape[0]
    assert B <= CTX_ROW and x.shape == (B, SEQ, D_MODEL) and ctx.shape == (B, CTX_LEN, D_MODEL)
    l = 0
    c8 = jnp.concatenate([c, jnp.zeros((CTX_ROW - B, D_MODEL), F32), c_ctx[None, :],
                          jnp.zeros((MOD_ROWS - CTX_ROW - 1, D_MODEL), F32)], axis=0)
    mod = _ada(c8, w_ada[l], b_ada[l][None, :])
    mod3 = mod.reshape(MOD_ROWS, 1, N_MOD * D_MODEL)

    win_b = jnp.pad(w_in[l], ((0, 0), (0, LANES - MLA_ROPE))).astype(BF16)
    wuq_b = jnp.pad(w_uq[l].reshape(Q_RANK, MLA_HEADS, MLA_QK),
                    ((0, 0), (0, 0), (0, MLA_QK_PAD - MLA_QK))
                    ).reshape(Q_RANK, MLA_HEADS * MLA_QK_PAD).astype(BF16)
    wukv3 = w_ukv[l].reshape(KV_RANK, MLA_HEADS, MLA_NOPE + MLA_V)
    wukv_b = jnp.concatenate([wukv3[:, :, :MLA_NOPE].reshape(KV_RANK, MLA_HEADS * MLA_NOPE),
                              wukv3[:, :, MLA_NOPE:].reshape(KV_RANK, MLA_WIDTH)],
                             axis=1).astype(BF16)
    wo_b = w_o[l].astype(BF16)
    cos_t, sin_t = _rope_tables()

    q_da, k_da, v_da, q_mla, k_mla, v_mla = _mixer_inputs(
        ctx, x, mod3, norm1_w[l][None, :], cos_t, sin_t, win_b,
        q_norm_w[l][None, :], kv_norm_w[l][None, :], wuq_b, wukv_b)
    o_da = _diff_attn(q_da, k_da, v_da, lambda_q1[l][None, :], lambda_k1[l][None, :],
                      lambda_q2[l][None, :], lambda_k2[l][None, :], subln_w[l][None, :])
    o_mla = _mla_attn(q_mla, k_mla, v_mla)
    x1, h2 = _out_proj(o_da, o_mla, wo_b, x, mod3, norm2_w[l][None, :])
    y = _conv_ffn(h2, w_up[l], conv_w[l], conv_b[l][None, :], w_down[l])
    return _final(x1, y, mod3, final_w[None, :])
```

```python
import functools
import math

import jax
import jax.numpy as jnp
from jax import lax
from jax.experimental import pallas as pl
from jax.experimental.pallas import tpu as pltpu

D_MODEL = 2048
SEQ = 2048
CTX_LEN = 256
GRID_W = 64
HEAD_DIM = 128
DA_HEADS = 8
DA_HALF = 64
MLA_HEADS = 8
MLA_NOPE = 128
MLA_ROPE = 64
MLA_V = 128
Q_RANK = 384
KV_RANK = 256
ROPE_DIM = 64
ROPE_BASE = 10000.0
D_FF = 5632
CONV_W = 3
N_MOD = 6
EPS = 1e-6
DA_WIDTH = DA_HEADS * HEAD_DIM
MLA_WIDTH = MLA_HEADS * MLA_V
MLA_QK = MLA_NOPE + MLA_ROPE
MLA_QK_PAD = 256
LOG2E = math.log2(math.e)
DA_SCALE = LOG2E / math.sqrt(DA_HALF)
MLA_SCALE = LOG2E / math.sqrt(MLA_QK)
LAMBDA_INIT = 0.8 - 0.6 * math.exp(-0.3 * 0)
T_ALL = CTX_LEN + SEQ

LANES = 128
MOD_ROWS = 8
CTX_ROW = 4
VMEM_LIMIT = 56 * 1024 * 1024

TOK_TILE = 256
ADA_TN = 1024
ATT_TQ = 256
OUT_TM = 512
FFN_TM = 1024
FFN_TF = 512
FFN_SUB = 256
FFN_NOUT = 512
FFN_HALO = 16
FIN_TM = 512

F32 = jnp.float32
BF16 = jnp.bfloat16


def _dot(a, b):
    return jnp.dot(a, b, preferred_element_type=F32)


def _dot_nt(a, b):
    return lax.dot_general(a, b, (((1,), (1,)), ((), ())), preferred_element_type=F32)


def _rms(x):
    return x * lax.rsqrt(jnp.mean(x * x, axis=-1, keepdims=True) + EPS)


def _cparams(n_grid):
    return pltpu.CompilerParams(dimension_semantics=("arbitrary",) * n_grid,
                                vmem_limit_bytes=VMEM_LIMIT)


def _ada_kernel(c_ref, w_ref, b_ref, o_ref):
    c = c_ref[...]
    sc = c * (1.0 / (1.0 + jnp.exp(-c)))
    o_ref[...] = _dot(sc.astype(BF16), w_ref[...].astype(BF16)) + b_ref[...]


def _ada(c8, w_ada, b_ada):
    n = w_ada.shape[1]
    return pl.pallas_call(
        _ada_kernel,
        grid=(n // ADA_TN,),
        in_specs=[pl.BlockSpec((MOD_ROWS, D_MODEL), lambda j: (0, 0)),
                  pl.BlockSpec((D_MODEL, ADA_TN), lambda j: (0, j)),
                  pl.BlockSpec((1, ADA_TN), lambda j: (0, j))],
        out_specs=pl.BlockSpec((MOD_ROWS, ADA_TN), lambda j: (0, j)),
        out_shape=jax.ShapeDtypeStruct((MOD_ROWS, n), F32),
        compiler_params=_cparams(1),
        name="ada",
    )(c8, w_ada, b_ada)


def _rope_chunk(xc, cos, sin, lo_mask):
    up = pltpu.roll(xc, LANES - 16, 1)
    dn = pltpu.roll(xc, 16, 1)
    return xc * cos + jnp.where(lo_mask, up, dn) * sin


def _mixer_kernel(ctx_ref, x_ref, sh_ref, sc_ref, n1_ref, cos_ref, sin_ref, win_ref,
                  qn_ref, kvn_ref, wuq_ref, wukv_ref,
                  qda_ref, kda_ref, vda_ref, qm_ref, km_ref, vm_ref):
    t = pl.program_id(1)
    is_ctx = t == 0
    xt = jnp.where(is_ctx, ctx_ref[0], x_ref[0])
    h = _rms(xt) * n1_ref[...]
    h = h * (1.0 + sc_ref[0]) + sh_ref[0]
    hb = h.astype(BF16)
    cos = jnp.where(is_ctx, 1.0, cos_ref[...])
    sin = jnp.where(is_ctx, 0.0, sin_ref[...])
    lane = lax.broadcasted_iota(jnp.int32, (TOK_TILE, LANES), 1)
    lo_mask = (lane % 32) < 16
    o1, o2, o3 = DA_WIDTH, 2 * DA_WIDTH, 3 * DA_WIDTH
    o4 = o3 + Q_RANK
    o5 = o4 + KV_RANK

    k = _dot(hb, win_ref[:, o1:o2])
    for hh in range(DA_HEADS):
        sl = slice(hh * LANES, (hh + 1) * LANES)
        kda_ref[0, :, sl] = _rope_chunk(k[:, sl], cos, sin, lo_mask).astype(BF16)
    vda_ref[0] = _dot(hb, win_ref[:, o2:o3]).astype(BF16)

    ckv = _rms(_dot(hb, win_ref[:, o4:o5])) * kvn_ref[...]
    kv = _dot(ckv.astype(BF16), wukv_ref[...])
    kr = _rope_chunk(_dot(hb, win_ref[:, o5:o5 + LANES]), cos, sin, lo_mask).astype(BF16)
    for hh in range(MLA_HEADS):
        base = hh * MLA_QK_PAD
        km_ref[0, :, base:base + LANES] = kv[:, hh * LANES:(hh + 1) * LANES].astype(BF16)
        km_ref[0, :, base + LANES:base + 2 * LANES] = kr
    vm_ref[0] = kv[:, MLA_WIDTH:].astype(BF16)

    @pl.when(t > 0)
    def _():
        q = _dot(hb, win_ref[:, 0:o1])
        for hh in range(DA_HEADS):
            sl = slice(hh * LANES, (hh + 1) * LANES)
            qda_ref[0, :, sl] = (_rope_chunk(q[:, sl], cos, sin, lo_mask) * DA_SCALE).astype(BF16)
        cq = _rms(_dot(hb, win_ref[:, o3:o4])) * qn_ref[...]
        qm = _dot(cq.astype(BF16), wuq_ref[...])
        for hh in range(MLA_HEADS):
            base = hh * MLA_QK_PAD
            qm_ref[0, :, base:base + LANES] = (qm[:, base:base + LANES] * MLA_SCALE).astype(BF16)
            qr = _rope_chunk(qm[:, base + LANES:base + 2 * LANES], cos, sin, lo_mask)
            qm_ref[0, :, base + LANES:base + 2 * LANES] = (qr * MLA_SCALE).astype(BF16)


def _mixer_inputs(ctx, x, mod3, n1, cos_t, sin_t, win_b, qn, kvn, wuq_b, wukv_b):
    B = x.shape[0]
    nt = T_ALL // TOK_TILE

    def lat(t):
        return jnp.maximum(t - 1, 0)

    def const(shape):
        return pl.BlockSpec(shape, lambda b, t: (0,) * len(shape), pipeline_mode=pl.Buffered(1))

    in_specs = [
        pl.BlockSpec((1, CTX_LEN, D_MODEL), lambda b, t: (b, 0, 0)),
        pl.BlockSpec((1, TOK_TILE, D_MODEL), lambda b, t: (b, lat(t), 0)),
        pl.BlockSpec((1, 1, D_MODEL), lambda b, t: (jnp.where(t == 0, CTX_ROW, b), 0, 0)),
        pl.BlockSpec((1, 1, D_MODEL), lambda b, t: (jnp.where(t == 0, CTX_ROW, b), 0, 1)),
        const((1, D_MODEL)),
        pl.BlockSpec((TOK_TILE, LANES), lambda b, t: (lat(t), 0)),
        pl.BlockSpec((TOK_TILE, LANES), lambda b, t: (lat(t), 0)),
        const(win_b.shape),
        const((1, Q_RANK)),
        const((1, KV_RANK)),
        const(wuq_b.shape),
        const(wukv_b.shape),
    ]
    out_specs = [
        pl.BlockSpec((1, TOK_TILE, DA_WIDTH), lambda b, t: (b, lat(t), 0)),
        pl.BlockSpec((1, TOK_TILE, DA_WIDTH), lambda b, t: (b, t, 0)),
        pl.BlockSpec((1, TOK_TILE, DA_WIDTH), lambda b, t: (b, t, 0)),
        pl.BlockSpec((1, TOK_TILE, MLA_HEADS * MLA_QK_PAD), lambda b, t: (b, lat(t), 0)),
        pl.BlockSpec((1, TOK_TILE, MLA_HEADS * MLA_QK_PAD), lambda b, t: (b, t, 0)),
        pl.BlockSpec((1, TOK_TILE, MLA_WIDTH), lambda b, t: (b, t, 0)),
    ]
    out_shape = [
        jax.ShapeDtypeStruct((B, SEQ, DA_WIDTH), BF16),
        jax.ShapeDtypeStruct((B, T_ALL, DA_WIDTH), BF16),
        jax.ShapeDtypeStruct((B, T_ALL, DA_WIDTH), BF16),
        jax.ShapeDtypeStruct((B, SEQ, MLA_HEADS * MLA_QK_PAD), BF16),
        jax.ShapeDtypeStruct((B, T_ALL, MLA_HEADS * MLA_QK_PAD), BF16),
        jax.ShapeDtypeStruct((B, T_ALL, MLA_WIDTH), BF16),
    ]
    return pl.pallas_call(
        _mixer_kernel,
        grid=(B, nt),
        in_specs=in_specs,
        out_specs=out_specs,
        out_shape=out_shape,
        compiler_params=_cparams(2),
        name="mixer_in",
    )(ctx, x, mod3, mod3, n1, cos_t, sin_t, win_b, qn, kvn, wuq_b, wukv_b)


def _softmax_parts(s):
    m = jnp.max(s, axis=-1, keepdims=True)
    p = jnp.exp2(s - m)
    return p, jnp.sum(p, axis=-1, keepdims=True)


def _pipelined_tiles(scores, finish):
    n_tiles = SEQ // ATT_TQ
    s = scores(0)
    for n in range(n_tiles):
        s_next = scores(n + 1) if n + 1 < n_tiles else None
        finish(n, s)
        s = s_next


def _diff_attn_kernel(q_ref, k_ref, v_ref, lq1_ref, lk1_ref, lq2_ref, lk2_ref, sub_ref, o_ref):
    lam = (jnp.exp(jnp.sum(lq1_ref[...] * lk1_ref[...], axis=-1, keepdims=True))
           - jnp.exp(jnp.sum(lq2_ref[...] * lk2_ref[...], axis=-1, keepdims=True))
           + LAMBDA_INIT)
    lane = lax.broadcasted_iota(jnp.int32, (ATT_TQ, HEAD_DIM), 1)

    def scores(n):
        q = q_ref[0, n * ATT_TQ:(n + 1) * ATT_TQ, :]
        zero = jnp.zeros_like(q)
        q1 = jnp.where(lane < DA_HALF, q, zero)
        q2 = jnp.where(lane >= DA_HALF, q, zero)
        return _dot_nt(q1, k_ref[0]), _dot_nt(q2, k_ref[0])

    def finish(n, s):
        p1, l1 = _softmax_parts(s[0])
        p2, l2 = _softmax_parts(s[1])
        w = p1 - p2 * (lam * l1 / l2)
        o = _dot(w.astype(BF16), v_ref[0]) * (1.0 / l1)
        o = _rms(o) * sub_ref[...] * (1.0 - LAMBDA_INIT)
        o_ref[0, n * ATT_TQ:(n + 1) * ATT_TQ, :] = o.astype(BF16)

    _pipelined_tiles(scores, finish)


def _diff_attn(q, k, v, lq1, lk1, lq2, lk2, subln):
    B = q.shape[0]
    small = lambda n: pl.BlockSpec((1, n), lambda b, h: (0, 0))
    return pl.pallas_call(
        _diff_attn_kernel,
        grid=(B, DA_HEADS),
        in_specs=[pl.BlockSpec((1, SEQ, HEAD_DIM), lambda b, h: (b, 0, h)),
                  pl.BlockSpec((1, T_ALL, HEAD_DIM), lambda b, h: (b, 0, h)),
                  pl.BlockSpec((1, T_ALL, HEAD_DIM), lambda b, h: (b, 0, h)),
                  small(DA_HALF), small(DA_HALF), small(DA_HALF), small(DA_HALF),
                  small(HEAD_DIM)],
        out_specs=pl.BlockSpec((1, SEQ, HEAD_DIM), lambda b, h: (b, 0, h)),
        out_shape=jax.ShapeDtypeStruct((B, SEQ, DA_WIDTH), BF16),
        compiler_params=_cparams(2),
        name="diff_attn",
    )(q, k, v, lq1, lk1, lq2, lk2, subln)


def _mla_attn_kernel(q_ref, k_ref, v_ref, o_ref):
    def scores(n):
        return _dot_nt(q_ref[0, n * ATT_TQ:(n + 1) * ATT_TQ, :], k_ref[0])

    def finish(n, s):
        p, l = _softmax_parts(s)
        o = _dot(p.astype(BF16), v_ref[0]) * (1.0 / l)
        o_ref[0, n * ATT_TQ:(n + 1) * ATT_TQ, :] = o.astype(BF16)

    _pipelined_tiles(scores, finish)


def _mla_attn(q, k, v):
    B = q.shape[0]
    return pl.pallas_call(
        _mla_attn_kernel,
        grid=(B, MLA_HEADS),
        in_specs=[pl.BlockSpec((1, SEQ, MLA_QK_PAD), lambda b, h: (b, 0, h)),
                  pl.BlockSpec((1, T_ALL, MLA_QK_PAD), lambda b, h: (b, 0, h)),
                  pl.BlockSpec((1, T_ALL, MLA_V), lambda b, h: (b, 0, h))],
        out_specs=pl.BlockSpec((1, SEQ, MLA_V), lambda b, h: (b, 0, h)),
        out_shape=jax.ShapeDtypeStruct((B, SEQ, MLA_WIDTH), BF16),
        compiler_params=_cparams(2),
        name="mla_attn",
    )(q, k, v)


def _out_proj_kernel(oda_ref, omla_ref, wo_ref, x_ref, g1_ref, sh2_ref, sc2_ref, n2_ref,
                     x1_ref, h2_ref):
    y = _dot(oda_ref[0], wo_ref[0:DA_WIDTH, :]) + _dot(omla_ref[0], wo_ref[DA_WIDTH:, :])
    x1 = x_ref[0] + g1_ref[0] * y
    x1_ref[0] = x1
    h2 = _rms(x1) * n2_ref[...]
    h2_ref[0] = (h2 * (1.0 + sc2_ref[0]) + sh2_ref[0]).astype(BF16)


def _out_proj(o_da, o_mla, wo_b, x, mod3, n2):
    B = x.shape[0]
    tile = lambda w: pl.BlockSpec((1, OUT_TM, w), lambda b, i: (b, i, 0))
    modrow = lambda col: pl.BlockSpec((1, 1, D_MODEL), lambda b, i: (b, 0, col))
    return pl.pallas_call(
        _out_proj_kernel,
        grid=(B, SEQ // OUT_TM),
        in_specs=[tile(DA_WIDTH), tile(MLA_WIDTH),
                  pl.BlockSpec(wo_b.shape, lambda b, i: (0, 0), pipeline_mode=pl.Buffered(1)),
                  tile(D_MODEL), modrow(2), modrow(3), modrow(4),
                  pl.BlockSpec((1, D_MODEL), lambda b, i: (0, 0))],
        out_specs=[tile(D_MODEL), tile(D_MODEL)],
        out_shape=[jax.ShapeDtypeStruct((B, SEQ, D_MODEL), F32),
                   jax.ShapeDtypeStruct((B, SEQ, D_MODEL), BF16)],
        compiler_params=_cparams(2),
        name="out_proj",
    )(o_da, o_mla, wo_b, x, mod3, mod3, mod3, n2)


def _ffn_kernel(h_ref, top_ref, bot_ref, wg_ref, wu_ref, cw_ref, cb_ref, wd_ref, o_ref,
                hs_ref, g_ref):
    i = pl.program_id(1)
    j = pl.program_id(2)
    last_i = pl.num_programs(1) - 1

    @pl.when(j == 0)
    def _():
        top = top_ref[0]
        bot = bot_ref[0]
        hs_ref[0:FFN_HALO, :] = jnp.where(i == 0, jnp.zeros_like(top), top)
        hs_ref[FFN_HALO:FFN_HALO + FFN_TM, :] = h_ref[0]
        hs_ref[FFN_HALO + FFN_TM:, :] = jnp.where(i == last_i, jnp.zeros_like(bot), bot)
        o_ref[0] = jnp.zeros((FFN_TM, D_MODEL), F32)

    lo = FFN_HALO
    for c in range(FFN_TF // FFN_SUB):
        cs = slice(c * FFN_SUB, (c + 1) * FFN_SUB)
        g_ref[:, cs] = _dot(hs_ref[...], wg_ref[:, cs])
        u = _dot(hs_ref[lo:lo + FFN_TM, :], wu_ref[:, cs])
        gc = (g_ref[lo - 1:lo - 1 + FFN_TM, cs] * cw_ref[0:1, cs]
              + g_ref[lo:lo + FFN_TM, cs] * cw_ref[1:2, cs]
              + g_ref[lo + 1:lo + 1 + FFN_TM, cs] * cw_ref[2:3, cs]
              + cb_ref[:, cs])
        act = (gc * (1.0 / (1.0 + jnp.exp(-gc))) * u).astype(BF16)
        for n in range(D_MODEL // FFN_NOUT):
            ns = slice(n * FFN_NOUT, (n + 1) * FFN_NOUT)
            o_ref[0, :, ns] += _dot(act, wd_ref[cs, ns])


def _conv_ffn(h2, w_up, conv_w, conv_b, w_down):
    B = h2.shape[0]
    nf = D_FF // FFN_TF
    ni = SEQ // FFN_TM
    halo_per_tile = FFN_TM // FFN_HALO
    n_halo = SEQ // FFN_HALO
    return pl.pallas_call(
        _ffn_kernel,
        grid=(B, ni, nf),
        in_specs=[
            pl.BlockSpec((1, FFN_TM, D_MODEL), lambda b, i, j: (b, i, 0),
                         pipeline_mode=pl.Buffered(1)),
            pl.BlockSpec((1, FFN_HALO, D_MODEL),
                         lambda b, i, j: (b, jnp.maximum(i * halo_per_tile - 1, 0), 0)),
            pl.BlockSpec((1, FFN_HALO, D_MODEL),
                         lambda b, i, j: (b, jnp.minimum((i + 1) * halo_per_tile, n_halo - 1), 0)),
            pl.BlockSpec((D_MODEL, FFN_TF), lambda b, i, j: (0, j)),
            pl.BlockSpec((D_MODEL, FFN_TF), lambda b, i, j: (0, nf + j)),
            pl.BlockSpec((CONV_W, FFN_TF), lambda b, i, j: (0, j)),
            pl.BlockSpec((1, FFN_TF), lambda b, i, j: (0, j)),
            pl.BlockSpec((FFN_TF, D_MODEL), lambda b, i, j: (j, 0)),
        ],
        out_specs=pl.BlockSpec((1, FFN_TM, D_MODEL), lambda b, i, j: (b, i, 0)),
        out_shape=jax.ShapeDtypeStruct((B, SEQ, D_MODEL), F32),
        scratch_shapes=[pltpu.VMEM((FFN_TM + 2 * FFN_HALO, D_MODEL), BF16),
                        pltpu.VMEM((FFN_TM + 2 * FFN_HALO, FFN_TF), F32)],
        compiler_params=_cparams(3),
        name="conv_ffn",
    )(h2, h2, h2, w_up, w_up, conv_w, conv_b, w_down)


def _final_kernel(x1_ref, y_ref, g2_ref, fw_ref, o_ref):
    x2 = x1_ref[0] + g2_ref[0] * y_ref[0]
    o_ref[0] = _rms(x2) * fw_ref[...]


def _final(x1, y, mod3, final_w):
    B = x1.shape[0]
    tile = pl.BlockSpec((1, FIN_TM, D_MODEL), lambda b, i: (b, i, 0))
    return pl.pallas_call(
        _final_kernel,
        grid=(B, SEQ // FIN_TM),
        in_specs=[tile, tile,
                  pl.BlockSpec((1, 1, D_MODEL), lambda b, i: (b, 0, 5)),
                  pl.BlockSpec((1, D_MODEL), lambda b, i: (0, 0))],
        out_specs=tile,
        out_shape=jax.ShapeDtypeStruct((B, SEQ, D_MODEL), F32),
        compiler_params=_cparams(2),
        name="final",
    )(x1, y, mod3, final_w)


def _rope_tables():
    pos = jnp.arange(SEQ)
    row = (pos // GRID_W).astype(F32)
    col = (pos % GRID_W).astype(F32)
    nf = ROPE_DIM // 4
    inv = ROPE_BASE ** (-jnp.arange(nf, dtype=F32) / nf)
    ar = row[:, None] * inv
    ac = col[:, None] * inv
    cos = jnp.concatenate([jnp.cos(ar), jnp.cos(ar), jnp.cos(ac), jnp.cos(ac)], axis=-1)
    sin = jnp.concatenate([-jnp.sin(ar), jnp.sin(ar), -jnp.sin(ac), jnp.sin(ac)], axis=-1)
    return jnp.tile(cos, (1, LANES // ROPE_DIM)), jnp.tile(sin, (1, LANES // ROPE_DIM))


def kernel(x, c, ctx, c_ctx, w_ada, b_ada, norm1_w, w_in, q_norm_w, kv_norm_w, w_uq, w_ukv,
           lambda_q1, lambda_k1, lambda_q2, lambda_k2, subln_w, w_o, norm2_w, w_up,
           conv_w, conv_b, w_down, final_w):
    B = x.shape[0]
    assert B <= CTX_ROW and x.shape == (B, SEQ, D_MODEL) and ctx.shape == (B, CTX_LEN, D_MODEL)
    l = 0
    c8 = jnp.concatenate([c, jnp.zeros((CTX_ROW - B, D_MODEL), F32), c_ctx[None, :],
                          jnp.zeros((MOD_ROWS - CTX_ROW - 1, D_MODEL), F32)], axis=0)
    mod = _ada(c8, w_ada[l], b_ada[l][None, :])
    mod3 = mod.reshape(MOD_ROWS, 1, N_MOD * D_MODEL)

    win_b = jnp.pad(w_in[l], ((0, 0), (0, LANES - MLA_ROPE))).astype(BF16)
    wuq_b = jnp.pad(w_uq[l].reshape(Q_RANK, MLA_HEADS, MLA_QK),
                    ((0, 0), (0, 0), (0, MLA_QK_PAD - MLA_QK))
                    ).reshape(Q_RANK, MLA_HEADS * MLA_QK_PAD).astype(BF16)
    wukv3 = w_ukv[l].reshape(KV_RANK, MLA_HEADS, MLA_NOPE + MLA_V)
    wukv_b = jnp.concatenate([wukv3[:, :, :MLA_NOPE].reshape(KV_RANK, MLA_HEADS * MLA_NOPE),
                              wukv3[:, :, MLA_NOPE:].reshape(KV_RANK, MLA_WIDTH)],
                             axis=1).astype(BF16)
    wo_b = w_o[l].astype(BF16)
    cos_t, sin_t = _rope_tables()

    q_da, k_da, v_da, q_mla, k_mla, v_mla = _mixer_inputs(
        ctx, x, mod3, norm1_w[l][None, :], cos_t, sin_t, win_b,
        q_norm_w[l][None, :], kv_norm_w[l][None, :], wuq_b, wukv_b)
    o_da = _diff_attn(q_da, k_da, v_da, lambda_q1[l][None, :], lambda_k1[l][None, :],
                      lambda_q2[l][None, :], lambda_k2[l][None, :], subln_w[l][None, :])
    o_mla = _mla_attn(q_mla, k_mla, v_mla)
    x1, h2 = _out_proj(o_da, o_mla, wo_b, x, mod3, norm2_w[l][None, :])
    y = _conv_ffn(h2, w_up[l].astype(BF16), conv_w[l], conv_b[l][None, :], w_down[l].astype(BF16))
    return _final(x1, y, mod3, final_w[None, :])
```

```python
import functools
import math

import jax
import jax.numpy as jnp
from jax import lax
from jax.experimental import pallas as pl
from jax.experimental.pallas import tpu as pltpu

D_MODEL = 2048
SEQ = 2048
CTX_LEN = 256
GRID_W = 64
HEAD_DIM = 128
DA_HEADS = 8
DA_HALF = 64
MLA_HEADS = 8
MLA_NOPE = 128
MLA_ROPE = 64
MLA_V = 128
Q_RANK = 384
KV_RANK = 256
ROPE_DIM = 64
ROPE_BASE = 10000.0
D_FF = 5632
CONV_W = 3
N_MOD = 6
EPS = 1e-6
DA_WIDTH = DA_HEADS * HEAD_DIM
MLA_WIDTH = MLA_HEADS * MLA_V
MLA_QK = MLA_NOPE + MLA_ROPE
MLA_QK_PAD = 256
LOG2E = math.log2(math.e)
DA_SCALE = LOG2E / math.sqrt(DA_HALF)
MLA_SCALE = LOG2E / math.sqrt(MLA_QK)
LAMBDA_INIT = 0.8 - 0.6 * math.exp(-0.3 * 0)
T_ALL = CTX_LEN + SEQ

LANES = 128
MOD_ROWS = 8
CTX_ROW = 4
VMEM_LIMIT = 56 * 1024 * 1024

TOK_TILE = 256
ADA_TN = 1024
ATT_TQ = 256
OUT_TM = 512
FFN_TM = 1024
FFN_TF = 512
FFN_SUB = 256
FFN_NOUT = 512
FFN_HALO = 16
FFN_X1_STEPS = 8
FFN_X1_ROWS = FFN_TM // FFN_X1_STEPS

F32 = jnp.float32
BF16 = jnp.bfloat16


def _dot(a, b):
    return jnp.dot(a, b, preferred_element_type=F32)


def _dot_nt(a, b):
    return lax.dot_general(a, b, (((1,), (1,)), ((), ())), preferred_element_type=F32)


def _rms(x):
    return x * lax.rsqrt(jnp.mean(x * x, axis=-1, keepdims=True) + EPS)


def _cparams(n_grid):
    return pltpu.CompilerParams(dimension_semantics=("arbitrary",) * n_grid,
                                vmem_limit_bytes=VMEM_LIMIT)


def _ada_kernel(c_ref, w_ref, b_ref, o_ref):
    c = c_ref[...]
    sc = c * (1.0 / (1.0 + jnp.exp(-c)))
    o_ref[...] = _dot(sc.astype(BF16), w_ref[...].astype(BF16)) + b_ref[...]


def _ada(c8, w_ada, b_ada):
    n = w_ada.shape[1]
    return pl.pallas_call(
        _ada_kernel,
        grid=(n // ADA_TN,),
        in_specs=[pl.BlockSpec((MOD_ROWS, D_MODEL), lambda j: (0, 0)),
                  pl.BlockSpec((D_MODEL, ADA_TN), lambda j: (0, j)),
                  pl.BlockSpec((1, ADA_TN), lambda j: (0, j))],
        out_specs=pl.BlockSpec((MOD_ROWS, ADA_TN), lambda j: (0, j)),
        out_shape=jax.ShapeDtypeStruct((MOD_ROWS, n), F32),
        compiler_params=_cparams(1),
        name="ada",
    )(c8, w_ada, b_ada)


def _rope_chunk(xc, cos, sin, lo_mask):
    up = pltpu.roll(xc, LANES - 16, 1)
    dn = pltpu.roll(xc, 16, 1)
    return xc * cos + jnp.where(lo_mask, up, dn) * sin


def _mixer_kernel(ctx_ref, x_ref, sh_ref, sc_ref, n1_ref, cos_ref, sin_ref, win_ref,
                  qn_ref, kvn_ref, wuq_ref, wukv_ref,
                  qda_ref, kda_ref, vda_ref, qm_ref, km_ref, vm_ref):
    t = pl.program_id(1)
    is_ctx = t == 0
    xt = jnp.where(is_ctx, ctx_ref[0], x_ref[0])
    h = _rms(xt) * n1_ref[...]
    h = h * (1.0 + sc_ref[0]) + sh_ref[0]
    hb = h.astype(BF16)
    cos = jnp.where(is_ctx, 1.0, cos_ref[...])
    sin = jnp.where(is_ctx, 0.0, sin_ref[...])
    lane = lax.broadcasted_iota(jnp.int32, (TOK_TILE, LANES), 1)
    lo_mask = (lane % 32) < 16
    o1, o2, o3 = DA_WIDTH, 2 * DA_WIDTH, 3 * DA_WIDTH
    o4 = o3 + Q_RANK
    o5 = o4 + KV_RANK

    k = _dot(hb, win_ref[:, o1:o2])
    for hh in range(DA_HEADS):
        sl = slice(hh * LANES, (hh + 1) * LANES)
        kda_ref[0, :, sl] = _rope_chunk(k[:, sl], cos, sin, lo_mask).astype(BF16)
    vda_ref[0] = _dot(hb, win_ref[:, o2:o3]).astype(BF16)

    low = _dot(hb, win_ref[:, o3:o5 + LANES])
    ckv = _rms(low[:, Q_RANK:Q_RANK + KV_RANK]) * kvn_ref[...]
    kv = _dot(ckv.astype(BF16), wukv_ref[...])
    kr = _rope_chunk(low[:, Q_RANK + KV_RANK:], cos, sin, lo_mask).astype(BF16)
    for hh in range(MLA_HEADS):
        base = hh * MLA_QK_PAD
        km_ref[0, :, base:base + LANES] = kv[:, hh * LANES:(hh + 1) * LANES].astype(BF16)
        km_ref[0, :, base + LANES:base + 2 * LANES] = kr
    vm_ref[0] = kv[:, MLA_WIDTH:].astype(BF16)

    q = _dot(hb, win_ref[:, 0:o1])
    for hh in range(DA_HEADS):
        sl = slice(hh * LANES, (hh + 1) * LANES)
        qda_ref[0, :, sl] = (_rope_chunk(q[:, sl], cos, sin, lo_mask) * DA_SCALE).astype(BF16)
    cq = _rms(low[:, :Q_RANK]) * qn_ref[...]
    qm = _dot(cq.astype(BF16), wuq_ref[...])
    for hh in range(MLA_HEADS):
        base = hh * MLA_QK_PAD
        qm_ref[0, :, base:base + LANES] = (qm[:, base:base + LANES] * MLA_SCALE).astype(BF16)
        qr = _rope_chunk(qm[:, base + LANES:base + 2 * LANES], cos, sin, lo_mask)
        qm_ref[0, :, base + LANES:base + 2 * LANES] = (qr * MLA_SCALE).astype(BF16)


def _mixer_inputs(ctx, x, mod3, n1, cos_t, sin_t, win_b, qn, kvn, wuq_b, wukv_b):
    B = x.shape[0]
    nt = T_ALL // TOK_TILE

    def lat(t):
        return jnp.maximum(t - 1, 0)

    def const(shape):
        return pl.BlockSpec(shape, lambda b, t: (0,) * len(shape), pipeline_mode=pl.Buffered(1))

    in_specs = [
        pl.BlockSpec((1, CTX_LEN, D_MODEL), lambda b, t: (b, 0, 0)),
        pl.BlockSpec((1, TOK_TILE, D_MODEL), lambda b, t: (b, lat(t), 0)),
        pl.BlockSpec((1, 1, D_MODEL), lambda b, t: (jnp.where(t == 0, CTX_ROW, b), 0, 0)),
        pl.BlockSpec((1, 1, D_MODEL), lambda b, t: (jnp.where(t == 0, CTX_ROW, b), 0, 1)),
        const((1, D_MODEL)),
        pl.BlockSpec((TOK_TILE, LANES), lambda b, t: (lat(t), 0)),
        pl.BlockSpec((TOK_TILE, LANES), lambda b, t: (lat(t), 0)),
        const(win_b.shape),
        const((1, Q_RANK)),
        const((1, KV_RANK)),
        const(wuq_b.shape),
        const(wukv_b.shape),
    ]
    out_specs = [
        pl.BlockSpec((1, TOK_TILE, DA_WIDTH), lambda b, t: (b, lat(t), 0)),
        pl.BlockSpec((1, TOK_TILE, DA_WIDTH), lambda b, t: (b, t, 0)),
        pl.BlockSpec((1, TOK_TILE, DA_WIDTH), lambda b, t: (b, t, 0)),
        pl.BlockSpec((1, TOK_TILE, MLA_HEADS * MLA_QK_PAD), lambda b, t: (b, lat(t), 0)),
        pl.BlockSpec((1, TOK_TILE, MLA_HEADS * MLA_QK_PAD), lambda b, t: (b, t, 0)),
        pl.BlockSpec((1, TOK_TILE, MLA_WIDTH), lambda b, t: (b, t, 0)),
    ]
    out_shape = [
        jax.ShapeDtypeStruct((B, SEQ, DA_WIDTH), BF16),
        jax.ShapeDtypeStruct((B, T_ALL, DA_WIDTH), BF16),
        jax.ShapeDtypeStruct((B, T_ALL, DA_WIDTH), BF16),
        jax.ShapeDtypeStruct((B, SEQ, MLA_HEADS * MLA_QK_PAD), BF16),
        jax.ShapeDtypeStruct((B, T_ALL, MLA_HEADS * MLA_QK_PAD), BF16),
        jax.ShapeDtypeStruct((B, T_ALL, MLA_WIDTH), BF16),
    ]
    return pl.pallas_call(
        _mixer_kernel,
        grid=(B, nt),
        in_specs=in_specs,
        out_specs=out_specs,
        out_shape=out_shape,
        compiler_params=_cparams(2),
        name="mixer_in",
    )(ctx, x, mod3, mod3, n1, cos_t, sin_t, win_b, qn, kvn, wuq_b, wukv_b)


def _softmax_parts(s):
    m = jnp.max(s, axis=-1, keepdims=True)
    p = jnp.exp2(s - m)
    return p, jnp.sum(p, axis=-1, keepdims=True)


def _pipelined_tiles(scores, finish):
    n_tiles = SEQ // ATT_TQ
    s = scores(0)
    for n in range(n_tiles):
        s_next = scores(n + 1) if n + 1 < n_tiles else None
        finish(n, s)
        s = s_next


def _diff_attn_kernel(q_ref, k_ref, v_ref, lq1_ref, lk1_ref, lq2_ref, lk2_ref, sub_ref, o_ref,
                      vx_ref):
    lam = (jnp.exp(jnp.sum(lq1_ref[...] * lk1_ref[...], axis=-1, keepdims=True))
           - jnp.exp(jnp.sum(lq2_ref[...] * lk2_ref[...], axis=-1, keepdims=True))
           + LAMBDA_INIT)
    lane = lax.broadcasted_iota(jnp.int32, (ATT_TQ, HEAD_DIM), 1)
    vx_ref[:, 0:HEAD_DIM] = v_ref[0]
    ones_lane = lax.broadcasted_iota(jnp.int32, (T_ALL, HEAD_DIM), 1) == 0
    vx_ref[:, HEAD_DIM:] = jnp.where(ones_lane, 1.0, 0.0).astype(BF16)

    def weighted(s):
        p = jnp.exp2(s - jnp.max(s, axis=-1, keepdims=True))
        ox = _dot(p.astype(BF16), vx_ref[...])
        return ox[:, 0:HEAD_DIM] * (1.0 / ox[:, HEAD_DIM:HEAD_DIM + 1])

    def scores(n):
        q = q_ref[0, n * ATT_TQ:(n + 1) * ATT_TQ, :]
        zero = jnp.zeros_like(q)
        q1 = jnp.where(lane < DA_HALF, q, zero)
        q2 = jnp.where(lane >= DA_HALF, q, zero)
        return _dot_nt(q1, k_ref[0]), _dot_nt(q2, k_ref[0])

    def finish(n, s):
        o = weighted(s[0]) - lam * weighted(s[1])
        o = _rms(o) * sub_ref[...] * (1.0 - LAMBDA_INIT)
        o_ref[0, n * ATT_TQ:(n + 1) * ATT_TQ, :] = o.astype(BF16)

    _pipelined_tiles(scores, finish)


def _diff_attn(q, k, v, lq1, lk1, lq2, lk2, subln):
    B = q.shape[0]
    small = lambda n: pl.BlockSpec((1, n), lambda b, h: (0, 0))
    return pl.pallas_call(
        _diff_attn_kernel,
        grid=(B, DA_HEADS),
        in_specs=[pl.BlockSpec((1, SEQ, HEAD_DIM), lambda b, h: (b, 0, h)),
                  pl.BlockSpec((1, T_ALL, HEAD_DIM), lambda b, h: (b, 0, h)),
                  pl.BlockSpec((1, T_ALL, HEAD_DIM), lambda b, h: (b, 0, h)),
                  small(DA_HALF), small(DA_HALF), small(DA_HALF), small(DA_HALF),
                  small(HEAD_DIM)],
        out_specs=pl.BlockSpec((1, SEQ, HEAD_DIM), lambda b, h: (b, 0, h)),
        out_shape=jax.ShapeDtypeStruct((B, SEQ, DA_WIDTH), BF16),
        scratch_shapes=[pltpu.VMEM((T_ALL, 2 * HEAD_DIM), BF16)],
        compiler_params=_cparams(2),
        name="diff_attn",
    )(q, k, v, lq1, lk1, lq2, lk2, subln)


def _mla_attn_kernel(q_ref, k_ref, v_ref, o_ref):
    def scores(n):
        return _dot_nt(q_ref[0, n * ATT_TQ:(n + 1) * ATT_TQ, :], k_ref[0])

    def finish(n, s):
        p, l = _softmax_parts(s)
        o = _dot(p.astype(BF16), v_ref[0]) * (1.0 / l)
        o_ref[0, n * ATT_TQ:(n + 1) * ATT_TQ, :] = o.astype(BF16)

    _pipelined_tiles(scores, finish)


def _mla_attn(q, k, v):
    B = q.shape[0]
    return pl.pallas_call(
        _mla_attn_kernel,
        grid=(B, MLA_HEADS),
        in_specs=[pl.BlockSpec((1, SEQ, MLA_QK_PAD), lambda b, h: (b, 0, h)),
                  pl.BlockSpec((1, T_ALL, MLA_QK_PAD), lambda b, h: (b, 0, h)),
                  pl.BlockSpec((1, T_ALL, MLA_V), lambda b, h: (b, 0, h))],
        out_specs=pl.BlockSpec((1, SEQ, MLA_V), lambda b, h: (b, 0, h)),
        out_shape=jax.ShapeDtypeStruct((B, SEQ, MLA_WIDTH), BF16),
        compiler_params=_cparams(2),
        name="mla_attn",
    )(q, k, v)


def _out_proj_kernel(oda_ref, omla_ref, wo_ref, x_ref, g1_ref, sh2_ref, sc2_ref, n2_ref,
                     x1_ref, h2_ref):
    y = _dot(oda_ref[0], wo_ref[0:DA_WIDTH, :]) + _dot(omla_ref[0], wo_ref[DA_WIDTH:, :])
    x1 = x_ref[0] + g1_ref[0] * y
    x1_ref[0] = x1
    h2 = _rms(x1) * n2_ref[...]
    h2_ref[0] = (h2 * (1.0 + sc2_ref[0]) + sh2_ref[0]).astype(BF16)


def _out_proj(o_da, o_mla, wo_b, x, mod3, n2):
    B = x.shape[0]
    tile = lambda w: pl.BlockSpec((1, OUT_TM, w), lambda b, i: (b, i, 0))
    modrow = lambda col: pl.BlockSpec((1, 1, D_MODEL), lambda b, i: (b, 0, col))
    return pl.pallas_call(
        _out_proj_kernel,
        grid=(B, SEQ // OUT_TM),
        in_specs=[tile(DA_WIDTH), tile(MLA_WIDTH),
                  pl.BlockSpec(wo_b.shape, lambda b, i: (0, 0), pipeline_mode=pl.Buffered(1)),
                  tile(D_MODEL), modrow(2), modrow(3), modrow(4),
                  pl.BlockSpec((1, D_MODEL), lambda b, i: (0, 0))],
        out_specs=[tile(D_MODEL), tile(D_MODEL)],
        out_shape=[jax.ShapeDtypeStruct((B, SEQ, D_MODEL), F32),
                   jax.ShapeDtypeStruct((B, SEQ, D_MODEL), BF16)],
        compiler_params=_cparams(2),
        name="out_proj",
    )(o_da, o_mla, wo_b, x, mod3, mod3, mod3, n2)


def _ffn_kernel(h_ref, top_ref, bot_ref, wg_ref, wu_ref, cw_ref, cb_ref, wd_ref, x1_ref, g2_ref,
                fw_ref, o_ref, hs_ref, g_ref):
    i = pl.program_id(1)
    j = pl.program_id(2)
    last_i = pl.num_programs(1) - 1
    last_j = pl.num_programs(2) - 1

    @pl.when(j == 0)
    def _():
        top = top_ref[0]
        bot = bot_ref[0]
        hs_ref[0:FFN_HALO, :] = jnp.where(i == 0, jnp.zeros_like(top), top)
        hs_ref[FFN_HALO:FFN_HALO + FFN_TM, :] = h_ref[0]
        hs_ref[FFN_HALO + FFN_TM:, :] = jnp.where(i == last_i, jnp.zeros_like(bot), bot)
        o_ref[0] = jnp.zeros((FFN_TM, D_MODEL), F32)

    @pl.when(j < FFN_X1_STEPS)
    def _():
        rows = pl.ds(pl.multiple_of(j * FFN_X1_ROWS, FFN_X1_ROWS), FFN_X1_ROWS)
        o_ref[0, rows, :] += x1_ref[0]

    lo = FFN_HALO
    g2 = g2_ref[0]
    for c in range(FFN_TF // FFN_SUB):
        cs = slice(c * FFN_SUB, (c + 1) * FFN_SUB)
        g_ref[:, cs] = _dot(hs_ref[...], wg_ref[:, cs])
        u = _dot(hs_ref[lo:lo + FFN_TM, :], wu_ref[:, cs])
        gc = (g_ref[lo - 1:lo - 1 + FFN_TM, cs] * cw_ref[0:1, cs]
              + g_ref[lo:lo + FFN_TM, cs] * cw_ref[1:2, cs]
              + g_ref[lo + 1:lo + 1 + FFN_TM, cs] * cw_ref[2:3, cs]
              + cb_ref[:, cs])
        act = (gc * (1.0 / (1.0 + jnp.exp(-gc))) * u).astype(BF16)
        for n in range(D_MODEL // FFN_NOUT):
            ns = slice(n * FFN_NOUT, (n + 1) * FFN_NOUT)
            o_ref[0, :, ns] += g2[:, ns] * _dot(act, wd_ref[cs, ns])

    @pl.when(j == last_j)
    def _():
        o_ref[0] = _rms(o_ref[0]) * fw_ref[...]


def _conv_ffn_final(h2, w_up, conv_w, conv_b, w_down, x1, mod3, final_w):
    B = h2.shape[0]
    nf = D_FF // FFN_TF
    ni = SEQ // FFN_TM
    assert nf >= FFN_X1_STEPS
    halo_per_tile = FFN_TM // FFN_HALO
    n_halo = SEQ // FFN_HALO
    return pl.pallas_call(
        _ffn_kernel,
        grid=(B, ni, nf),
        in_specs=[
            pl.BlockSpec((1, FFN_TM, D_MODEL), lambda b, i, j: (b, i, 0),
                         pipeline_mode=pl.Buffered(1)),
            pl.BlockSpec((1, FFN_HALO, D_MODEL),
                         lambda b, i, j: (b, jnp.maximum(i * halo_per_tile - 1, 0), 0)),
            pl.BlockSpec((1, FFN_HALO, D_MODEL),
                         lambda b, i, j: (b, jnp.minimum((i + 1) * halo_per_tile, n_halo - 1), 0)),
            pl.BlockSpec((D_MODEL, FFN_TF), lambda b, i, j: (0, j)),
            pl.BlockSpec((D_MODEL, FFN_TF), lambda b, i, j: (0, nf + j)),
            pl.BlockSpec((CONV_W, FFN_TF), lambda b, i, j: (0, j)),
            pl.BlockSpec((1, FFN_TF), lambda b, i, j: (0, j)),
            pl.BlockSpec((FFN_TF, D_MODEL), lambda b, i, j: (j, 0)),
            pl.BlockSpec((1, FFN_X1_ROWS, D_MODEL),
                         lambda b, i, j: (b, i * FFN_X1_STEPS + jnp.minimum(j, FFN_X1_STEPS - 1), 0)),
            pl.BlockSpec((1, 1, D_MODEL), lambda b, i, j: (b, 0, 5)),
            pl.BlockSpec((1, D_MODEL), lambda b, i, j: (0, 0)),
        ],
        out_specs=pl.BlockSpec((1, FFN_TM, D_MODEL), lambda b, i, j: (b, i, 0)),
        out_shape=jax.ShapeDtypeStruct((B, SEQ, D_MODEL), F32),
        scratch_shapes=[pltpu.VMEM((FFN_TM + 2 * FFN_HALO, D_MODEL), BF16),
                        pltpu.VMEM((FFN_TM + 2 * FFN_HALO, FFN_TF), F32)],
        compiler_params=_cparams(3),
        name="conv_ffn",
    )(h2, h2, h2, w_up, w_up, conv_w, conv_b, w_down, x1, mod3, final_w)


def _rope_tables():
    pos = jnp.arange(SEQ)
    row = (pos // GRID_W).astype(F32)
    col = (pos % GRID_W).astype(F32)
    nf = ROPE_DIM // 4
    inv = ROPE_BASE ** (-jnp.arange(nf, dtype=F32) / nf)
    ar = row[:, None] * inv
    ac = col[:, None] * inv
    cos = jnp.concatenate([jnp.cos(ar), jnp.cos(ar), jnp.cos(ac), jnp.cos(ac)], axis=-1)
    sin = jnp.concatenate([-jnp.sin(ar), jnp.sin(ar), -jnp.sin(ac), jnp.sin(ac)], axis=-1)
    return jnp.tile(cos, (1, LANES // ROPE_DIM)), jnp.tile(sin, (1, LANES // ROPE_DIM))


def kernel(x, c, ctx, c_ctx, w_ada, b_ada, norm1_w, w_in, q_norm_w, kv_norm_w, w_uq, w_ukv,
           lambda_q1, lambda_k1, lambda_q2, lambda_k2, subln_w, w_o, norm2_w, w_up,
           conv_w, conv_b, w_down, final_w):
    B = x.shape[0]
    assert B <= CTX_ROW and x.shape == (B, SEQ, D_MODEL) and ctx.shape == (B, CTX_LEN, D_MODEL)
    l = 0
    c8 = jnp.concatenate([c, jnp.zeros((CTX_ROW - B, D_MODEL), F32), c_ctx[None, :],
                          jnp.zeros((MOD_ROWS - CTX_ROW - 1, D_MODEL), F32)], axis=0)
    mod = _ada(c8, w_ada[l], b_ada[l][None, :])
    mod3 = mod.reshape(MOD_ROWS, 1, N_MOD * D_MODEL)

    win_b = jnp.pad(w_in[l], ((0, 0), (0, LANES - MLA_ROPE))).astype(BF16)
    wuq_b = jnp.pad(w_uq[l].reshape(Q_RANK, MLA_HEADS, MLA_QK),
                    ((0, 0), (0, 0), (0, MLA_QK_PAD - MLA_QK))
                    ).reshape(Q_RANK, MLA_HEADS * MLA_QK_PAD).astype(BF16)
    wukv3 = w_ukv[l].reshape(KV_RANK, MLA_HEADS, MLA_NOPE + MLA_V)
    wukv_b = jnp.concatenate([wukv3[:, :, :MLA_NOPE].reshape(KV_RANK, MLA_HEADS * MLA_NOPE),
                              wukv3[:, :, MLA_NOPE:].reshape(KV_RANK, MLA_WIDTH)],
                             axis=1).astype(BF16)
    wo_b = w_o[l].astype(BF16)
    cos_t, sin_t = _rope_tables()

    q_da, k_da, v_da, q_mla, k_mla, v_mla = _mixer_inputs(
        ctx, x, mod3, norm1_w[l][None, :], cos_t, sin_t, win_b,
        q_norm_w[l][None, :], kv_norm_w[l][None, :], wuq_b, wukv_b)
    o_da = _diff_attn(q_da, k_da, v_da, lambda_q1[l][None, :], lambda_k1[l][None, :],
                      lambda_q2[l][None, :], lambda_k2[l][None, :], subln_w[l][None, :])
    o_mla = _mla_attn(q_mla, k_mla, v_mla)
    x1, h2 = _out_proj(o_da, o_mla, wo_b, x, mod3, norm2_w[l][None, :])
    return _conv_ffn_final(h2, w_up[l].astype(BF16), conv_w[l], conv_b[l][None, :],
                           w_down[l].astype(BF16), x1, mod3, final_w[None, :])
```

```python
import functools
import math

import jax
import jax.numpy as jnp
from jax import lax
from jax.experimental import pallas as pl
from jax.experimental.pallas import tpu as pltpu

D_MODEL = 2048
SEQ = 2048
CTX_LEN = 256
GRID_W = 64
HEAD_DIM = 128
DA_HEADS = 8
DA_HALF = 64
MLA_HEADS = 8
MLA_NOPE = 128
MLA_ROPE = 64
MLA_V = 128
Q_RANK = 384
KV_RANK = 256
ROPE_DIM = 64
ROPE_BASE = 10000.0
D_FF = 5632
CONV_W = 3
N_MOD = 6
EPS = 1e-6
DA_WIDTH = DA_HEADS * HEAD_DIM
MLA_WIDTH = MLA_HEADS * MLA_V
MLA_QK = MLA_NOPE + MLA_ROPE
MLA_QK_PAD = 256
LOG2E = math.log2(math.e)
DA_SCALE = LOG2E / math.sqrt(DA_HALF)
MLA_SCALE = LOG2E / math.sqrt(MLA_QK)
LAMBDA_INIT = 0.8 - 0.6 * math.exp(-0.3 * 0)
T_ALL = CTX_LEN + SEQ

LANES = 128
MOD_ROWS = 8
CTX_ROW = 4
VMEM_LIMIT = 56 * 1024 * 1024

TOK_TILE = 256
ADA_TN = 1024
ATT_TQ = 256
OUT_TM = 512
OUT_SUB = 256
FFN_TM = 1024
FFN_TF = 512
FFN_SUB = 256
FFN_NOUT = 512
FFN_HALO = 16
FFN_X1_STEPS = 8
FFN_X1_ROWS = FFN_TM // FFN_X1_STEPS

F32 = jnp.float32
BF16 = jnp.bfloat16


def _dot(a, b):
    return jnp.dot(a, b, preferred_element_type=F32)


def _dot_nt(a, b):
    return lax.dot_general(a, b, (((1,), (1,)), ((), ())), preferred_element_type=F32)


def _rms(x):
    return x * lax.rsqrt(jnp.mean(x * x, axis=-1, keepdims=True) + EPS)


def _cparams(n_grid):
    return pltpu.CompilerParams(dimension_semantics=("arbitrary",) * n_grid,
                                vmem_limit_bytes=VMEM_LIMIT)


def _ada_kernel(c_ref, w_ref, b_ref, o_ref):
    c = c_ref[...]
    sc = c * (1.0 / (1.0 + jnp.exp(-c)))
    o_ref[...] = _dot(sc.astype(BF16), w_ref[...].astype(BF16)) + b_ref[...]


def _ada(c8, w_ada, b_ada):
    n = w_ada.shape[1]
    return pl.pallas_call(
        _ada_kernel,
        grid=(n // ADA_TN,),
        in_specs=[pl.BlockSpec((MOD_ROWS, D_MODEL), lambda j: (0, 0)),
                  pl.BlockSpec((D_MODEL, ADA_TN), lambda j: (0, j)),
                  pl.BlockSpec((1, ADA_TN), lambda j: (0, j))],
        out_specs=pl.BlockSpec((MOD_ROWS, ADA_TN), lambda j: (0, j)),
        out_shape=jax.ShapeDtypeStruct((MOD_ROWS, n), F32),
        compiler_params=_cparams(1),
        name="ada",
    )(c8, w_ada, b_ada)


def _rope_chunk(xc, cos, sin, lo_mask):
    up = pltpu.roll(xc, LANES - 16, 1)
    dn = pltpu.roll(xc, 16, 1)
    return xc * cos + jnp.where(lo_mask, up, dn) * sin


def _mixer_kernel(ctx_ref, x_ref, sh_ref, sc_ref, n1_ref, cos_ref, sin_ref, win_ref,
                  qn_ref, kvn_ref, wuq_ref, wukv_ref,
                  qda_ref, kda_ref, vda_ref, qm_ref, km_ref, vm_ref):
    t = pl.program_id(1)
    is_ctx = t == 0
    xt = jnp.where(is_ctx, ctx_ref[0], x_ref[0])
    h = _rms(xt) * n1_ref[...]
    h = h * (1.0 + sc_ref[0]) + sh_ref[0]
    hb = h.astype(BF16)
    cos = jnp.where(is_ctx, 1.0, cos_ref[...])
    sin = jnp.where(is_ctx, 0.0, sin_ref[...])
    lane = lax.broadcasted_iota(jnp.int32, (TOK_TILE, LANES), 1)
    lo_mask = (lane % 32) < 16
    o1, o2, o3 = DA_WIDTH, 2 * DA_WIDTH, 3 * DA_WIDTH
    o4 = o3 + Q_RANK
    o5 = o4 + KV_RANK

    k = _dot(hb, win_ref[:, o1:o2])
    for hh in range(DA_HEADS):
        sl = slice(hh * LANES, (hh + 1) * LANES)
        kda_ref[0, :, sl] = _rope_chunk(k[:, sl], cos, sin, lo_mask).astype(BF16)
    vda_ref[0] = _dot(hb, win_ref[:, o2:o3]).astype(BF16)

    low = _dot(hb, win_ref[:, o3:o5 + LANES])
    ckv = _rms(low[:, Q_RANK:Q_RANK + KV_RANK]) * kvn_ref[...]
    kv = _dot(ckv.astype(BF16), wukv_ref[...])
    kr = _rope_chunk(low[:, Q_RANK + KV_RANK:], cos, sin, lo_mask).astype(BF16)
    for hh in range(MLA_HEADS):
        base = hh * MLA_QK_PAD
        km_ref[0, :, base:base + LANES] = kv[:, hh * LANES:(hh + 1) * LANES].astype(BF16)
        km_ref[0, :, base + LANES:base + 2 * LANES] = kr
    vm_ref[0] = kv[:, MLA_WIDTH:].astype(BF16)

    q = _dot(hb, win_ref[:, 0:o1])
    for hh in range(DA_HEADS):
        sl = slice(hh * LANES, (hh + 1) * LANES)
        qda_ref[0, :, sl] = (_rope_chunk(q[:, sl], cos, sin, lo_mask) * DA_SCALE).astype(BF16)
    cq = _rms(low[:, :Q_RANK]) * qn_ref[...]
    qm = _dot(cq.astype(BF16), wuq_ref[...])
    for hh in range(MLA_HEADS):
        base = hh * MLA_QK_PAD
        qm_ref[0, :, base:base + LANES] = (qm[:, base:base + LANES] * MLA_SCALE).astype(BF16)
        qr = _rope_chunk(qm[:, base + LANES:base + 2 * LANES], cos, sin, lo_mask)
        qm_ref[0, :, base + LANES:base + 2 * LANES] = (qr * MLA_SCALE).astype(BF16)


def _mixer_inputs(ctx, x, mod3, n1, cos_t, sin_t, win_b, qn, kvn, wuq_b, wukv_b):
    B = x.shape[0]
    nt = T_ALL // TOK_TILE

    def lat(t):
        return jnp.maximum(t - 1, 0)

    def const(shape):
        return pl.BlockSpec(shape, lambda b, t: (0,) * len(shape), pipeline_mode=pl.Buffered(1))

    in_specs = [
        pl.BlockSpec((1, CTX_LEN, D_MODEL), lambda b, t: (b, 0, 0)),
        pl.BlockSpec((1, TOK_TILE, D_MODEL), lambda b, t: (b, lat(t), 0)),
        pl.BlockSpec((1, 1, D_MODEL), lambda b, t: (jnp.where(t == 0, CTX_ROW, b), 0, 0)),
        pl.BlockSpec((1, 1, D_MODEL), lambda b, t: (jnp.where(t == 0, CTX_ROW, b), 0, 1)),
        const((1, D_MODEL)),
        pl.BlockSpec((TOK_TILE, LANES), lambda b, t: (lat(t), 0)),
        pl.BlockSpec((TOK_TILE, LANES), lambda b, t: (lat(t), 0)),
        const(win_b.shape),
        const((1, Q_RANK)),
        const((1, KV_RANK)),
        const(wuq_b.shape),
        const(wukv_b.shape),
    ]
    out_specs = [
        pl.BlockSpec((1, TOK_TILE, DA_WIDTH), lambda b, t: (b, lat(t), 0)),
        pl.BlockSpec((1, TOK_TILE, DA_WIDTH), lambda b, t: (b, t, 0)),
        pl.BlockSpec((1, TOK_TILE, DA_WIDTH), lambda b, t: (b, t, 0)),
        pl.BlockSpec((1, TOK_TILE, MLA_HEADS * MLA_QK_PAD), lambda b, t: (b, lat(t), 0)),
        pl.BlockSpec((1, TOK_TILE, MLA_HEADS * MLA_QK_PAD), lambda b, t: (b, t, 0)),
        pl.BlockSpec((1, TOK_TILE, MLA_WIDTH), lambda b, t: (b, t, 0)),
    ]
    out_shape = [
        jax.ShapeDtypeStruct((B, SEQ, DA_WIDTH), BF16),
        jax.ShapeDtypeStruct((B, T_ALL, DA_WIDTH), BF16),
        jax.ShapeDtypeStruct((B, T_ALL, DA_WIDTH), BF16),
        jax.ShapeDtypeStruct((B, SEQ, MLA_HEADS * MLA_QK_PAD), BF16),
        jax.ShapeDtypeStruct((B, T_ALL, MLA_HEADS * MLA_QK_PAD), BF16),
        jax.ShapeDtypeStruct((B, T_ALL, MLA_WIDTH), BF16),
    ]
    return pl.pallas_call(
        _mixer_kernel,
        grid=(B, nt),
        in_specs=in_specs,
        out_specs=out_specs,
        out_shape=out_shape,
        compiler_params=_cparams(2),
        name="mixer_in",
    )(ctx, x, mod3, mod3, n1, cos_t, sin_t, win_b, qn, kvn, wuq_b, wukv_b)


def _fill_values_with_ones(v_ref, vx_ref):
    width = v_ref.shape[-1]
    vx_ref[:, 0:width] = v_ref[0]
    ones_lane = lax.broadcasted_iota(jnp.int32, (T_ALL, width), 1) == 0
    vx_ref[:, width:] = jnp.where(ones_lane, 1.0, 0.0).astype(BF16)


def _softmax_weighted(s, vx_ref):
    width = vx_ref.shape[-1] // 2
    p = jnp.exp2(s - jnp.max(s, axis=-1, keepdims=True))
    ox = _dot(p.astype(BF16), vx_ref[...])
    return ox[:, 0:width] * (1.0 / ox[:, width:width + 1])


def _pipelined_tiles(scores, finish):
    n_tiles = SEQ // ATT_TQ
    s = scores(0)
    for n in range(n_tiles):
        s_next = scores(n + 1) if n + 1 < n_tiles else None
        finish(n, s)
        s = s_next


def _diff_attn_kernel(q_ref, k_ref, v_ref, lq1_ref, lk1_ref, lq2_ref, lk2_ref, sub_ref, o_ref,
                      vx_ref):
    lam = (jnp.exp(jnp.sum(lq1_ref[...] * lk1_ref[...], axis=-1, keepdims=True))
           - jnp.exp(jnp.sum(lq2_ref[...] * lk2_ref[...], axis=-1, keepdims=True))
           + LAMBDA_INIT)
    lane = lax.broadcasted_iota(jnp.int32, (ATT_TQ, HEAD_DIM), 1)
    _fill_values_with_ones(v_ref, vx_ref)

    def scores(n):
        q = q_ref[0, n * ATT_TQ:(n + 1) * ATT_TQ, :]
        zero = jnp.zeros_like(q)
        q1 = jnp.where(lane < DA_HALF, q, zero)
        q2 = jnp.where(lane >= DA_HALF, q, zero)
        return _dot_nt(q1, k_ref[0]), _dot_nt(q2, k_ref[0])

    def finish(n, s):
        o = _softmax_weighted(s[0], vx_ref) - lam * _softmax_weighted(s[1], vx_ref)
        o = _rms(o) * sub_ref[...] * (1.0 - LAMBDA_INIT)
        o_ref[0, n * ATT_TQ:(n + 1) * ATT_TQ, :] = o.astype(BF16)

    _pipelined_tiles(scores, finish)


def _diff_attn(q, k, v, lq1, lk1, lq2, lk2, subln):
    B = q.shape[0]
    small = lambda n: pl.BlockSpec((1, n), lambda b, h: (0, 0))
    return pl.pallas_call(
        _diff_attn_kernel,
        grid=(B, DA_HEADS),
        in_specs=[pl.BlockSpec((1, SEQ, HEAD_DIM), lambda b, h: (b, 0, h)),
                  pl.BlockSpec((1, T_ALL, HEAD_DIM), lambda b, h: (b, 0, h)),
                  pl.BlockSpec((1, T_ALL, HEAD_DIM), lambda b, h: (b, 0, h)),
                  small(DA_HALF), small(DA_HALF), small(DA_HALF), small(DA_HALF),
                  small(HEAD_DIM)],
        out_specs=pl.BlockSpec((1, SEQ, HEAD_DIM), lambda b, h: (b, 0, h)),
        out_shape=jax.ShapeDtypeStruct((B, SEQ, DA_WIDTH), BF16),
        scratch_shapes=[pltpu.VMEM((T_ALL, 2 * HEAD_DIM), BF16)],
        compiler_params=_cparams(2),
        name="diff_attn",
    )(q, k, v, lq1, lk1, lq2, lk2, subln)


def _mla_attn_kernel(q_ref, k_ref, v_ref, o_ref, vx_ref):
    _fill_values_with_ones(v_ref, vx_ref)

    def scores(n):
        return _dot_nt(q_ref[0, n * ATT_TQ:(n + 1) * ATT_TQ, :], k_ref[0])

    def finish(n, s):
        o_ref[0, n * ATT_TQ:(n + 1) * ATT_TQ, :] = _softmax_weighted(s, vx_ref).astype(BF16)

    _pipelined_tiles(scores, finish)


def _mla_attn(q, k, v):
    B = q.shape[0]
    return pl.pallas_call(
        _mla_attn_kernel,
        grid=(B, MLA_HEADS),
        in_specs=[pl.BlockSpec((1, SEQ, MLA_QK_PAD), lambda b, h: (b, 0, h)),
                  pl.BlockSpec((1, T_ALL, MLA_QK_PAD), lambda b, h: (b, 0, h)),
                  pl.BlockSpec((1, T_ALL, MLA_V), lambda b, h: (b, 0, h))],
        out_specs=pl.BlockSpec((1, SEQ, MLA_V), lambda b, h: (b, 0, h)),
        out_shape=jax.ShapeDtypeStruct((B, SEQ, MLA_WIDTH), BF16),
        scratch_shapes=[pltpu.VMEM((T_ALL, 2 * MLA_V), BF16)],
        compiler_params=_cparams(2),
        name="mla_attn",
    )(q, k, v)


def _out_proj_kernel(oda_ref, omla_ref, wo_ref, x_ref, g1_ref, sh2_ref, sc2_ref, n2_ref,
                     x1_ref, h2_ref):
    for r in range(OUT_TM // OUT_SUB):
        rs = slice(r * OUT_SUB, (r + 1) * OUT_SUB)
        y = (_dot(oda_ref[0, rs, :], wo_ref[0:DA_WIDTH, :])
             + _dot(omla_ref[0, rs, :], wo_ref[DA_WIDTH:, :]))
        x1 = x_ref[0, rs, :] + g1_ref[0] * y
        x1_ref[0, rs, :] = x1
        h2 = _rms(x1) * n2_ref[...]
        h2_ref[0, rs, :] = (h2 * (1.0 + sc2_ref[0]) + sh2_ref[0]).astype(BF16)


def _out_proj(o_da, o_mla, wo_b, x, mod3, n2):
    B = x.shape[0]
    tile = lambda w: pl.BlockSpec((1, OUT_TM, w), lambda b, i: (b, i, 0))
    modrow = lambda col: pl.BlockSpec((1, 1, D_MODEL), lambda b, i: (b, 0, col))
    return pl.pallas_call(
        _out_proj_kernel,
        grid=(B, SEQ // OUT_TM),
        in_specs=[tile(DA_WIDTH), tile(MLA_WIDTH),
                  pl.BlockSpec(wo_b.shape, lambda b, i: (0, 0), pipeline_mode=pl.Buffered(1)),
                  tile(D_MODEL), modrow(2), modrow(3), modrow(4),
                  pl.BlockSpec((1, D_MODEL), lambda b, i: (0, 0))],
        out_specs=[tile(D_MODEL), tile(D_MODEL)],
        out_shape=[jax.ShapeDtypeStruct((B, SEQ, D_MODEL), F32),
                   jax.ShapeDtypeStruct((B, SEQ, D_MODEL), BF16)],
        compiler_params=_cparams(2),
        name="out_proj",
    )(o_da, o_mla, wo_b, x, mod3, mod3, mod3, n2)


def _ffn_kernel(h_ref, top_ref, bot_ref, wg_ref, wu_ref, cw_ref, cb_ref, wd_ref, x1_ref, g2_ref,
                fw_ref, o_ref, hs_ref):
    i = pl.program_id(1)
    j = pl.program_id(2)
    last_i = pl.num_programs(1) - 1
    last_j = pl.num_programs(2) - 1

    @pl.when(j == 0)
    def _():
        top = top_ref[0]
        bot = bot_ref[0]
        hs_ref[0:FFN_HALO, :] = jnp.where(i == 0, jnp.zeros_like(top), top)
        hs_ref[FFN_HALO:FFN_HALO + FFN_TM, :] = h_ref[0]
        hs_ref[FFN_HALO + FFN_TM:, :] = jnp.where(i == last_i, jnp.zeros_like(bot), bot)
        o_ref[0] = jnp.zeros((FFN_TM, D_MODEL), F32)

    @pl.when(j < FFN_X1_STEPS)
    def _():
        rows = pl.ds(pl.multiple_of(j * FFN_X1_ROWS, FFN_X1_ROWS), FFN_X1_ROWS)
        o_ref[0, rows, :] += x1_ref[0]

    lo = FFN_HALO
    g2 = g2_ref[0]
    for c in range(FFN_TF // FFN_SUB):
        cs = slice(c * FFN_SUB, (c + 1) * FFN_SUB)
        g = _dot(hs_ref[...], wg_ref[:, cs])
        u = _dot(hs_ref[lo:lo + FFN_TM, :], wu_ref[:, cs])
        g_prev = pltpu.roll(g, 1, 0)
        g_next = pltpu.roll(g, FFN_TM + 2 * FFN_HALO - 1, 0)
        gc = (g_prev[lo:lo + FFN_TM] * cw_ref[0:1, cs]
              + g[lo:lo + FFN_TM] * cw_ref[1:2, cs]
              + g_next[lo:lo + FFN_TM] * cw_ref[2:3, cs]
              + cb_ref[:, cs])
        act = (gc * (1.0 / (1.0 + jnp.exp(-gc))) * u).astype(BF16)
        for n in range(D_MODEL // FFN_NOUT):
            ns = slice(n * FFN_NOUT, (n + 1) * FFN_NOUT)
            o_ref[0, :, ns] += g2[:, ns] * _dot(act, wd_ref[cs, ns])

    @pl.when(j == last_j)
    def _():
        o_ref[0] = _rms(o_ref[0]) * fw_ref[...]


def _conv_ffn_final(h2, w_up, conv_w, conv_b, w_down, x1, mod3, final_w):
    B = h2.shape[0]
    nf = D_FF // FFN_TF
    ni = SEQ // FFN_TM
    assert nf >= FFN_X1_STEPS
    halo_per_tile = FFN_TM // FFN_HALO
    n_halo = SEQ // FFN_HALO
    return pl.pallas_call(
        _ffn_kernel,
        grid=(B, ni, nf),
        in_specs=[
            pl.BlockSpec((1, FFN_TM, D_MODEL), lambda b, i, j: (b, i, 0),
                         pipeline_mode=pl.Buffered(1)),
            pl.BlockSpec((1, FFN_HALO, D_MODEL),
                         lambda b, i, j: (b, jnp.maximum(i * halo_per_tile - 1, 0), 0)),
            pl.BlockSpec((1, FFN_HALO, D_MODEL),
                         lambda b, i, j: (b, jnp.minimum((i + 1) * halo_per_tile, n_halo - 1), 0)),
            pl.BlockSpec((D_MODEL, FFN_TF), lambda b, i, j: (0, j)),
            pl.BlockSpec((D_MODEL, FFN_TF), lambda b, i, j: (0, nf + j)),
            pl.BlockSpec((CONV_W, FFN_TF), lambda b, i, j: (0, j)),
            pl.BlockSpec((1, FFN_TF), lambda b, i, j: (0, j)),
            pl.BlockSpec((FFN_TF, D_MODEL), lambda b, i, j: (j, 0)),
            pl.BlockSpec((1, FFN_X1_ROWS, D_MODEL),
                         lambda b, i, j: (b, i * FFN_X1_STEPS + jnp.minimum(j, FFN_X1_STEPS - 1), 0)),
            pl.BlockSpec((1, 1, D_MODEL), lambda b, i, j: (b, 0, 5)),
            pl.BlockSpec((1, D_MODEL), lambda b, i, j: (0, 0)),
        ],
        out_specs=pl.BlockSpec((1, FFN_TM, D_MODEL), lambda b, i, j: (b, i, 0)),
        out_shape=jax.ShapeDtypeStruct((B, SEQ, D_MODEL), F32),
        scratch_shapes=[pltpu.VMEM((FFN_TM + 2 * FFN_HALO, D_MODEL), BF16)],
        compiler_params=_cparams(3),
        name="conv_ffn",
    )(h2, h2, h2, w_up, w_up, conv_w, conv_b, w_down, x1, mod3, final_w)


def _rope_tables():
    pos = jnp.arange(SEQ)
    row = (pos // GRID_W).astype(F32)
    col = (pos % GRID_W).astype(F32)
    nf = ROPE_DIM // 4
    inv = ROPE_BASE ** (-jnp.arange(nf, dtype=F32) / nf)
    ar = row[:, None] * inv
    ac = col[:, None] * inv
    cos = jnp.concatenate([jnp.cos(ar), jnp.cos(ar), jnp.cos(ac), jnp.cos(ac)], axis=-1)
    sin = jnp.concatenate([-jnp.sin(ar), jnp.sin(ar), -jnp.sin(ac), jnp.sin(ac)], axis=-1)
    return jnp.tile(cos, (1, LANES // ROPE_DIM)), jnp.tile(sin, (1, LANES // ROPE_DIM))


def kernel(x, c, ctx, c_ctx, w_ada, b_ada, norm1_w, w_in, q_norm_w, kv_norm_w, w_uq, w_ukv,
           lambda_q1, lambda_k1, lambda_q2, lambda_k2, subln_w, w_o, norm2_w, w_up,
           conv_w, conv_b, w_down, final_w):
    B = x.shape[0]
    assert B <= CTX_ROW and x.shape == (B, SEQ, D_MODEL) and ctx.shape == (B, CTX_LEN, D_MODEL)
    l = 0
    c8 = jnp.concatenate([c, jnp.zeros((CTX_ROW - B, D_MODEL), F32), c_ctx[None, :],
                          jnp.zeros((MOD_ROWS - CTX_ROW - 1, D_MODEL), F32)], axis=0)
    mod = _ada(c8, w_ada[l], b_ada[l][None, :])
    mod3 = mod.reshape(MOD_ROWS, 1, N_MOD * D_MODEL)

    win_b = jnp.pad(w_in[l], ((0, 0), (0, LANES - MLA_ROPE))).astype(BF16)
    wuq_b = jnp.pad(w_uq[l].reshape(Q_RANK, MLA_HEADS, MLA_QK),
                    ((0, 0), (0, 0), (0, MLA_QK_PAD - MLA_QK))
                    ).reshape(Q_RANK, MLA_HEADS * MLA_QK_PAD).astype(BF16)
    wukv3 = w_ukv[l].reshape(KV_RANK, MLA_HEADS, MLA_NOPE + MLA_V)
    wukv_b = jnp.concatenate([wukv3[:, :, :MLA_NOPE].reshape(KV_RANK, MLA_HEADS * MLA_NOPE),
                              wukv3[:, :, MLA_NOPE:].reshape(KV_RANK, MLA_WIDTH)],
                             axis=1).astype(BF16)
    wo_b = w_o[l].astype(BF16)
    cos_t, sin_t = _rope_tables()

    q_da, k_da, v_da, q_mla, k_mla, v_mla = _mixer_inputs(
        ctx, x, mod3, norm1_w[l][None, :], cos_t, sin_t, win_b,
        q_norm_w[l][None, :], kv_norm_w[l][None, :], wuq_b, wukv_b)
    o_da = _diff_attn(q_da, k_da, v_da, lambda_q1[l][None, :], lambda_k1[l][None, :],
                      lambda_q2[l][None, :], lambda_k2[l][None, :], subln_w[l][None, :])
    o_mla = _mla_attn(q_mla, k_mla, v_mla)
    x1, h2 = _out_proj(o_da, o_mla, wo_b, x, mod3, norm2_w[l][None, :])
    return _conv_ffn_final(h2, w_up[l].astype(BF16), conv_w[l], conv_b[l][None, :],
                           w_down[l].astype(BF16), x1, mod3, final_w[None, :])
```

```python
import functools
import math

import jax
import jax.numpy as jnp
from jax import lax
from jax.experimental import pallas as pl
from jax.experimental.pallas import tpu as pltpu

D_MODEL = 2048
SEQ = 2048
CTX_LEN = 256
GRID_W = 64
HEAD_DIM = 128
DA_HEADS = 8
DA_HALF = 64
MLA_HEADS = 8
MLA_NOPE = 128
MLA_ROPE = 64
MLA_V = 128
Q_RANK = 384
KV_RANK = 256
ROPE_DIM = 64
ROPE_BASE = 10000.0
D_FF = 5632
CONV_W = 3
N_MOD = 6
EPS = 1e-6
DA_WIDTH = DA_HEADS * HEAD_DIM
MLA_WIDTH = MLA_HEADS * MLA_V
MLA_QK = MLA_NOPE + MLA_ROPE
MLA_QK_PAD = 256
LOG2E = math.log2(math.e)
DA_SCALE = LOG2E / math.sqrt(DA_HALF)
MLA_SCALE = LOG2E / math.sqrt(MLA_QK)
LAMBDA_INIT = 0.8 - 0.6 * math.exp(-0.3 * 0)
T_ALL = CTX_LEN + SEQ

LANES = 128
MOD_ROWS = 8
CTX_ROW = 4
VMEM_LIMIT = 60 * 1024 * 1024

TOK_TILE = 256
ADA_TN = 1024
ATT_TQ = 256
OUT_TM = 512
FFN_TM = 1024
FFN_TF = 512
FFN_SUB = 256
FFN_NOUT = 512
FFN_HALO = 16
FFN_X1_STEPS = 8
FFN_X1_ROWS = FFN_TM // FFN_X1_STEPS

F32 = jnp.float32
BF16 = jnp.bfloat16


def _dot(a, b):
    return jnp.dot(a, b, preferred_element_type=F32)


def _dot_nt(a, b):
    return lax.dot_general(a, b, (((1,), (1,)), ((), ())), preferred_element_type=F32)


def _rms(x):
    return x * lax.rsqrt(jnp.mean(x * x, axis=-1, keepdims=True) + EPS)


def _cparams(n_grid):
    return pltpu.CompilerParams(dimension_semantics=("arbitrary",) * n_grid,
                                vmem_limit_bytes=VMEM_LIMIT)


def _ada_kernel(c_ref, w_ref, b_ref, o_ref):
    c = c_ref[...]
    sc = c * (1.0 / (1.0 + jnp.exp(-c)))
    o_ref[...] = _dot(sc.astype(BF16), w_ref[...].astype(BF16)) + b_ref[...]


def _ada(c8, w_ada, b_ada):
    n = w_ada.shape[1]
    return pl.pallas_call(
        _ada_kernel,
        grid=(n // ADA_TN,),
        in_specs=[pl.BlockSpec((MOD_ROWS, D_MODEL), lambda j: (0, 0)),
                  pl.BlockSpec((D_MODEL, ADA_TN), lambda j: (0, j)),
                  pl.BlockSpec((1, ADA_TN), lambda j: (0, j))],
        out_specs=pl.BlockSpec((MOD_ROWS, ADA_TN), lambda j: (0, j)),
        out_shape=jax.ShapeDtypeStruct((MOD_ROWS, n), F32),
        compiler_params=_cparams(1),
        name="ada",
    )(c8, w_ada, b_ada)


def _rope_chunk(xc, cos, sin, lo_mask):
    up = pltpu.roll(xc, LANES - 16, 1)
    dn = pltpu.roll(xc, 16, 1)
    return xc * cos + jnp.where(lo_mask, up, dn) * sin


def _mixer_kernel(ctx_ref, x_ref, sh_ref, sc_ref, n1_ref, cos_ref, sin_ref, win_ref, wtail_ref,
                  qn_ref, kvn_ref, wuq_ref, wukv_ref,
                  qda_ref, kda_ref, vda_ref, qm_ref, km_ref, vm_ref):
    t = pl.program_id(1)
    is_ctx = t == 0
    xt = jnp.where(is_ctx, ctx_ref[0], x_ref[0])
    h = _rms(xt) * n1_ref[...]
    h = h * (1.0 + sc_ref[0]) + sh_ref[0]
    hb = h.astype(BF16)
    cos = jnp.where(is_ctx, 1.0, cos_ref[...])
    sin = jnp.where(is_ctx, 0.0, sin_ref[...])
    lane = lax.broadcasted_iota(jnp.int32, (TOK_TILE, LANES), 1)
    lo_mask = (lane % 32) < 16
    o1, o2, o3 = DA_WIDTH, 2 * DA_WIDTH, 3 * DA_WIDTH
    o4 = o3 + Q_RANK
    o5 = o4 + KV_RANK

    k = _dot(hb, win_ref[:, o1:o2].astype(BF16))
    for hh in range(DA_HEADS):
        sl = slice(hh * LANES, (hh + 1) * LANES)
        kda_ref[0, :, sl] = _rope_chunk(k[:, sl], cos, sin, lo_mask).astype(BF16)
    vda_ref[0] = _dot(hb, win_ref[:, o2:o3].astype(BF16)).astype(BF16)

    w_low = jnp.concatenate([win_ref[:, o3:o5].astype(BF16), wtail_ref[...].astype(BF16)], axis=1)
    low = _dot(hb, w_low)
    ckv = _rms(low[:, Q_RANK:Q_RANK + KV_RANK]) * kvn_ref[...]
    kv = _dot(ckv.astype(BF16), wukv_ref[...])
    kr = _rope_chunk(low[:, Q_RANK + KV_RANK:], cos, sin, lo_mask).astype(BF16)
    for hh in range(MLA_HEADS):
        base = hh * MLA_QK_PAD
        km_ref[0, :, base:base + LANES] = kv[:, hh * LANES:(hh + 1) * LANES].astype(BF16)
        km_ref[0, :, base + LANES:base + 2 * LANES] = kr
    vm_ref[0] = kv[:, MLA_WIDTH:].astype(BF16)

    q = _dot(hb, win_ref[:, 0:o1].astype(BF16))
    for hh in range(DA_HEADS):
        sl = slice(hh * LANES, (hh + 1) * LANES)
        qda_ref[0, :, sl] = (_rope_chunk(q[:, sl], cos, sin, lo_mask) * DA_SCALE).astype(BF16)
    cq = _rms(low[:, :Q_RANK]) * qn_ref[...]
    qm = _dot(cq.astype(BF16), wuq_ref[...])
    for hh in range(MLA_HEADS):
        base = hh * MLA_QK_PAD
        qm_ref[0, :, base:base + LANES] = (qm[:, base:base + LANES] * MLA_SCALE).astype(BF16)
        qr = _rope_chunk(qm[:, base + LANES:base + 2 * LANES], cos, sin, lo_mask)
        qm_ref[0, :, base + LANES:base + 2 * LANES] = (qr * MLA_SCALE).astype(BF16)


def _mixer_inputs(ctx, x, mod3, n1, cos_t, sin_t, w_in, w_tail, qn, kvn, wuq_b, wukv_b):
    B = x.shape[0]
    nt = T_ALL // TOK_TILE

    def lat(t):
        return jnp.maximum(t - 1, 0)

    def const(shape):
        return pl.BlockSpec(shape, lambda b, t: (0,) * len(shape), pipeline_mode=pl.Buffered(1))

    in_specs = [
        pl.BlockSpec((1, CTX_LEN, D_MODEL), lambda b, t: (b, 0, 0)),
        pl.BlockSpec((1, TOK_TILE, D_MODEL), lambda b, t: (b, lat(t), 0)),
        pl.BlockSpec((1, 1, D_MODEL), lambda b, t: (jnp.where(t == 0, CTX_ROW, b), 0, 0)),
        pl.BlockSpec((1, 1, D_MODEL), lambda b, t: (jnp.where(t == 0, CTX_ROW, b), 0, 1)),
        const((1, D_MODEL)),
        pl.BlockSpec((TOK_TILE, LANES), lambda b, t: (lat(t), 0)),
        pl.BlockSpec((TOK_TILE, LANES), lambda b, t: (lat(t), 0)),
        const(w_in.shape),
        const(w_tail.shape),
        const((1, Q_RANK)),
        const((1, KV_RANK)),
        const(wuq_b.shape),
        const(wukv_b.shape),
    ]
    out_specs = [
        pl.BlockSpec((1, TOK_TILE, DA_WIDTH), lambda b, t: (b, lat(t), 0)),
        pl.BlockSpec((1, TOK_TILE, DA_WIDTH), lambda b, t: (b, t, 0)),
        pl.BlockSpec((1, TOK_TILE, DA_WIDTH), lambda b, t: (b, t, 0)),
        pl.BlockSpec((1, TOK_TILE, MLA_HEADS * MLA_QK_PAD), lambda b, t: (b, lat(t), 0)),
        pl.BlockSpec((1, TOK_TILE, MLA_HEADS * MLA_QK_PAD), lambda b, t: (b, t, 0)),
        pl.BlockSpec((1, TOK_TILE, MLA_WIDTH), lambda b, t: (b, t, 0)),
    ]
    out_shape = [
        jax.ShapeDtypeStruct((B, SEQ, DA_WIDTH), BF16),
        jax.ShapeDtypeStruct((B, T_ALL, DA_WIDTH), BF16),
        jax.ShapeDtypeStruct((B, T_ALL, DA_WIDTH), BF16),
        jax.ShapeDtypeStruct((B, SEQ, MLA_HEADS * MLA_QK_PAD), BF16),
        jax.ShapeDtypeStruct((B, T_ALL, MLA_HEADS * MLA_QK_PAD), BF16),
        jax.ShapeDtypeStruct((B, T_ALL, MLA_WIDTH), BF16),
    ]
    return pl.pallas_call(
        _mixer_kernel,
        grid=(B, nt),
        in_specs=in_specs,
        out_specs=out_specs,
        out_shape=out_shape,
        compiler_params=_cparams(2),
        name="mixer_in",
    )(ctx, x, mod3, mod3, n1, cos_t, sin_t, w_in, w_tail, qn, kvn, wuq_b, wukv_b)


def _fill_values_with_ones(v_ref, vx_ref):
    width = v_ref.shape[-1]
    vx_ref[:, 0:width] = v_ref[0]
    ones_lane = lax.broadcasted_iota(jnp.int32, (T_ALL, width), 1) == 0
    vx_ref[:, width:] = jnp.where(ones_lane, 1.0, 0.0).astype(BF16)


def _softmax_weighted(s, vx_ref):
    width = vx_ref.shape[-1] // 2
    p = jnp.exp2(s - jnp.max(s, axis=-1, keepdims=True))
    ox = _dot(p.astype(BF16), vx_ref[...])
    return ox[:, 0:width] * (1.0 / ox[:, width:width + 1])


def _pipelined_tiles(scores, finish):
    n_tiles = SEQ // ATT_TQ
    s = scores(0)
    for n in range(n_tiles):
        s_next = scores(n + 1) if n + 1 < n_tiles else None
        finish(n, s)
        s = s_next


def _diff_attn_kernel(q_ref, k_ref, v_ref, lq1_ref, lk1_ref, lq2_ref, lk2_ref, sub_ref, o_ref,
                      vx_ref):
    lam = (jnp.exp(jnp.sum(lq1_ref[...] * lk1_ref[...], axis=-1, keepdims=True))
           - jnp.exp(jnp.sum(lq2_ref[...] * lk2_ref[...], axis=-1, keepdims=True))
           + LAMBDA_INIT)
    lane = lax.broadcasted_iota(jnp.int32, (ATT_TQ, HEAD_DIM), 1)
    _fill_values_with_ones(v_ref, vx_ref)

    def scores(n):
        q = q_ref[0, n * ATT_TQ:(n + 1) * ATT_TQ, :]
        zero = jnp.zeros_like(q)
        q1 = jnp.where(lane < DA_HALF, q, zero)
        q2 = jnp.where(lane >= DA_HALF, q, zero)
        return _dot_nt(q1, k_ref[0]), _dot_nt(q2, k_ref[0])

    def finish(n, s):
        o = _softmax_weighted(s[0], vx_ref) - lam * _softmax_weighted(s[1], vx_ref)
        o = _rms(o) * sub_ref[...] * (1.0 - LAMBDA_INIT)
        o_ref[0, n * ATT_TQ:(n + 1) * ATT_TQ, :] = o.astype(BF16)

    _pipelined_tiles(scores, finish)


def _diff_attn(q, k, v, lq1, lk1, lq2, lk2, subln):
    B = q.shape[0]
    small = lambda n: pl.BlockSpec((1, n), lambda b, h: (0, 0))
    return pl.pallas_call(
        _diff_attn_kernel,
        grid=(B, DA_HEADS),
        in_specs=[pl.BlockSpec((1, SEQ, HEAD_DIM), lambda b, h: (b, 0, h)),
                  pl.BlockSpec((1, T_ALL, HEAD_DIM), lambda b, h: (b, 0, h)),
                  pl.BlockSpec((1, T_ALL, HEAD_DIM), lambda b, h: (b, 0, h)),
                  small(DA_HALF), small(DA_HALF), small(DA_HALF), small(DA_HALF),
                  small(HEAD_DIM)],
        out_specs=pl.BlockSpec((1, SEQ, HEAD_DIM), lambda b, h: (b, 0, h)),
        out_shape=jax.ShapeDtypeStruct((B, SEQ, DA_WIDTH), BF16),
        scratch_shapes=[pltpu.VMEM((T_ALL, 2 * HEAD_DIM), BF16)],
        compiler_params=_cparams(2),
        name="diff_attn",
    )(q, k, v, lq1, lk1, lq2, lk2, subln)


def _mla_attn_kernel(q_ref, k_ref, v_ref, o_ref, vx_ref):
    _fill_values_with_ones(v_ref, vx_ref)

    def scores(n):
        return _dot_nt(q_ref[0, n * ATT_TQ:(n + 1) * ATT_TQ, :], k_ref[0])

    def finish(n, s):
        o_ref[0, n * ATT_TQ:(n + 1) * ATT_TQ, :] = _softmax_weighted(s, vx_ref).astype(BF16)

    _pipelined_tiles(scores, finish)


def _mla_attn(q, k, v):
    B = q.shape[0]
    return pl.pallas_call(
        _mla_attn_kernel,
        grid=(B, MLA_HEADS),
        in_specs=[pl.BlockSpec((1, SEQ, MLA_QK_PAD), lambda b, h: (b, 0, h)),
                  pl.BlockSpec((1, T_ALL, MLA_QK_PAD), lambda b, h: (b, 0, h)),
                  pl.BlockSpec((1, T_ALL, MLA_V), lambda b, h: (b, 0, h))],
        out_specs=pl.BlockSpec((1, SEQ, MLA_V), lambda b, h: (b, 0, h)),
        out_shape=jax.ShapeDtypeStruct((B, SEQ, MLA_WIDTH), BF16),
        scratch_shapes=[pltpu.VMEM((T_ALL, 2 * MLA_V), BF16)],
        compiler_params=_cparams(2),
        name="mla_attn",
    )(q, k, v)


def _out_proj_kernel(oda_ref, omla_ref, wo_ref, x_ref, g1_ref, sh2_ref, sc2_ref, n2_ref,
                     x1_ref, h2_ref):
    y = (_dot(oda_ref[0], wo_ref[0:DA_WIDTH, :].astype(BF16))
         + _dot(omla_ref[0], wo_ref[DA_WIDTH:, :].astype(BF16)))
    x1 = x_ref[0] + g1_ref[0] * y
    x1_ref[0] = x1
    h2 = _rms(x1) * n2_ref[...]
    h2_ref[0] = (h2 * (1.0 + sc2_ref[0]) + sh2_ref[0]).astype(BF16)


def _out_proj(o_da, o_mla, w_o, x, mod3, n2):
    B = x.shape[0]
    tile = lambda w: pl.BlockSpec((1, OUT_TM, w), lambda b, i: (b, i, 0))
    modrow = lambda col: pl.BlockSpec((1, 1, D_MODEL), lambda b, i: (b, 0, col))
    return pl.pallas_call(
        _out_proj_kernel,
        grid=(B, SEQ // OUT_TM),
        in_specs=[tile(DA_WIDTH), tile(MLA_WIDTH),
                  pl.BlockSpec(w_o.shape, lambda b, i: (0, 0), pipeline_mode=pl.Buffered(1)),
                  tile(D_MODEL), modrow(2), modrow(3), modrow(4),
                  pl.BlockSpec((1, D_MODEL), lambda b, i: (0, 0))],
        out_specs=[tile(D_MODEL), tile(D_MODEL)],
        out_shape=[jax.ShapeDtypeStruct((B, SEQ, D_MODEL), F32),
                   jax.ShapeDtypeStruct((B, SEQ, D_MODEL), BF16)],
        compiler_params=_cparams(2),
        name="out_proj",
    )(o_da, o_mla, w_o, x, mod3, mod3, mod3, n2)


def _ffn_kernel(h_ref, top_ref, bot_ref, wg_ref, wu_ref, cw_ref, cb_ref, wd_ref, x1_ref, g2_ref,
                fw_ref, o_ref, hs_ref):
    i = pl.program_id(1)
    j = pl.program_id(2)
    last_i = pl.num_programs(1) - 1
    last_j = pl.num_programs(2) - 1

    @pl.when(j == 0)
    def _():
        top = top_ref[0]
        bot = bot_ref[0]
        hs_ref[0:FFN_HALO, :] = jnp.where(i == 0, jnp.zeros_like(top), top)
        hs_ref[FFN_HALO:FFN_HALO + FFN_TM, :] = h_ref[0]
        hs_ref[FFN_HALO + FFN_TM:, :] = jnp.where(i == last_i, jnp.zeros_like(bot), bot)
        o_ref[0] = jnp.zeros((FFN_TM, D_MODEL), F32)

    @pl.when(j < FFN_X1_STEPS)
    def _():
        rows = pl.ds(pl.multiple_of(j * FFN_X1_ROWS, FFN_X1_ROWS), FFN_X1_ROWS)
        o_ref[0, rows, :] += x1_ref[0]

    lo = FFN_HALO
    g2 = g2_ref[0]
    for c in range(FFN_TF // FFN_SUB):
        cs = slice(c * FFN_SUB, (c + 1) * FFN_SUB)
        g = _dot(hs_ref[...], wg_ref[:, cs].astype(BF16))
        u = _dot(hs_ref[lo:lo + FFN_TM, :], wu_ref[:, cs].astype(BF16))
        g_prev = pltpu.roll(g, 1, 0)
        g_next = pltpu.roll(g, FFN_TM + 2 * FFN_HALO - 1, 0)
        gc = (g_prev[lo:lo + FFN_TM] * cw_ref[0:1, cs]
              + g[lo:lo + FFN_TM] * cw_ref[1:2, cs]
              + g_next[lo:lo + FFN_TM] * cw_ref[2:3, cs]
              + cb_ref[:, cs])
        act = (gc * (1.0 / (1.0 + jnp.exp(-gc))) * u).astype(BF16)
        for n in range(D_MODEL // FFN_NOUT):
            ns = slice(n * FFN_NOUT, (n + 1) * FFN_NOUT)
            o_ref[0, :, ns] += g2[:, ns] * _dot(act, wd_ref[cs, ns].astype(BF16))

    @pl.when(j == last_j)
    def _():
        o_ref[0] = _rms(o_ref[0]) * fw_ref[...]


def _conv_ffn_final(h2, w_up, conv_w, conv_b, w_down, x1, mod3, final_w):
    B = h2.shape[0]
    nf = D_FF // FFN_TF
    ni = SEQ // FFN_TM
    assert nf >= FFN_X1_STEPS
    halo_per_tile = FFN_TM // FFN_HALO
    n_halo = SEQ // FFN_HALO
    return pl.pallas_call(
        _ffn_kernel,
        grid=(B, ni, nf),
        in_specs=[
            pl.BlockSpec((1, FFN_TM, D_MODEL), lambda b, i, j: (b, i, 0),
                         pipeline_mode=pl.Buffered(1)),
            pl.BlockSpec((1, FFN_HALO, D_MODEL),
                         lambda b, i, j: (b, jnp.maximum(i * halo_per_tile - 1, 0), 0)),
            pl.BlockSpec((1, FFN_HALO, D_MODEL),
                         lambda b, i, j: (b, jnp.minimum((i + 1) * halo_per_tile, n_halo - 1), 0)),
            pl.BlockSpec((D_MODEL, FFN_TF), lambda b, i, j: (0, j)),
            pl.BlockSpec((D_MODEL, FFN_TF), lambda b, i, j: (0, nf + j)),
            pl.BlockSpec((CONV_W, FFN_TF), lambda b, i, j: (0, j)),
            pl.BlockSpec((1, FFN_TF), lambda b, i, j: (0, j)),
            pl.BlockSpec((FFN_TF, D_MODEL), lambda b, i, j: (j, 0)),
            pl.BlockSpec((1, FFN_X1_ROWS, D_MODEL),
                         lambda b, i, j: (b, i * FFN_X1_STEPS + jnp.minimum(j, FFN_X1_STEPS - 1), 0)),
            pl.BlockSpec((1, 1, D_MODEL), lambda b, i, j: (b, 0, 5)),
            pl.BlockSpec((1, D_MODEL), lambda b, i, j: (0, 0)),
        ],
        out_specs=pl.BlockSpec((1, FFN_TM, D_MODEL), lambda b, i, j: (b, i, 0)),
        out_shape=jax.ShapeDtypeStruct((B, SEQ, D_MODEL), F32),
        scratch_shapes=[pltpu.VMEM((FFN_TM + 2 * FFN_HALO, D_MODEL), BF16)],
        compiler_params=_cparams(3),
        name="conv_ffn",
    )(h2, h2, h2, w_up, w_up, conv_w, conv_b, w_down, x1, mod3, final_w)


def _rope_tables():
    pos = jnp.arange(SEQ)
    row = (pos // GRID_W).astype(F32)
    col = (pos % GRID_W).astype(F32)
    nf = ROPE_DIM // 4
    inv = ROPE_BASE ** (-jnp.arange(nf, dtype=F32) / nf)
    ar = row[:, None] * inv
    ac = col[:, None] * inv
    cos = jnp.concatenate([jnp.cos(ar), jnp.cos(ar), jnp.cos(ac), jnp.cos(ac)], axis=-1)
    sin = jnp.concatenate([-jnp.sin(ar), jnp.sin(ar), -jnp.sin(ac), jnp.sin(ac)], axis=-1)
    return jnp.tile(cos, (1, LANES // ROPE_DIM)), jnp.tile(sin, (1, LANES // ROPE_DIM))


def kernel(x, c, ctx, c_ctx, w_ada, b_ada, norm1_w, w_in, q_norm_w, kv_norm_w, w_uq, w_ukv,
           lambda_q1, lambda_k1, lambda_q2, lambda_k2, subln_w, w_o, norm2_w, w_up,
           conv_w, conv_b, w_down, final_w):
    B = x.shape[0]
    assert B <= CTX_ROW and x.shape == (B, SEQ, D_MODEL) and ctx.shape == (B, CTX_LEN, D_MODEL)
    l = 0
    c8 = jnp.concatenate([c, jnp.zeros((CTX_ROW - B, D_MODEL), F32), c_ctx[None, :],
                          jnp.zeros((MOD_ROWS - CTX_ROW - 1, D_MODEL), F32)], axis=0)
    mod = _ada(c8, w_ada[l], b_ada[l][None, :])
    mod3 = mod.reshape(MOD_ROWS, 1, N_MOD * D_MODEL)

    w_tail = jnp.pad(w_in[l][:, 3 * DA_WIDTH + Q_RANK + KV_RANK:], ((0, 0), (0, LANES - MLA_ROPE)))
    wuq_b = jnp.pad(w_uq[l].reshape(Q_RANK, MLA_HEADS, MLA_QK),
                    ((0, 0), (0, 0), (0, MLA_QK_PAD - MLA_QK))
                    ).reshape(Q_RANK, MLA_HEADS * MLA_QK_PAD).astype(BF16)
    wukv3 = w_ukv[l].reshape(KV_RANK, MLA_HEADS, MLA_NOPE + MLA_V)
    wukv_b = jnp.concatenate([wukv3[:, :, :MLA_NOPE].reshape(KV_RANK, MLA_HEADS * MLA_NOPE),
                              wukv3[:, :, MLA_NOPE:].reshape(KV_RANK, MLA_WIDTH)],
                             axis=1).astype(BF16)
    cos_t, sin_t = _rope_tables()

    q_da, k_da, v_da, q_mla, k_mla, v_mla = _mixer_inputs(
        ctx, x, mod3, norm1_w[l][None, :], cos_t, sin_t, w_in[l], w_tail,
        q_norm_w[l][None, :], kv_norm_w[l][None, :], wuq_b, wukv_b)
    o_da = _diff_attn(q_da, k_da, v_da, lambda_q1[l][None, :], lambda_k1[l][None, :],
                      lambda_q2[l][None, :], lambda_k2[l][None, :], subln_w[l][None, :])
    o_mla = _mla_attn(q_mla, k_mla, v_mla)
    x1, h2 = _out_proj(o_da, o_mla, w_o[l], x, mod3, norm2_w[l][None, :])
    return _conv_ffn_final(h2, w_up[l], conv_w[l], conv_b[l][None, :],
                           w_down[l], x1, mod3, final_w[None, :])
```

```python
import functools
import math

import jax
import jax.numpy as jnp
from jax import lax
from jax.experimental import pallas as pl
from jax.experimental.pallas import tpu as pltpu

D_MODEL = 2048
SEQ = 2048
CTX_LEN = 256
GRID_W = 64
HEAD_DIM = 128
DA_HEADS = 8
DA_HALF = 64
MLA_HEADS = 8
MLA_NOPE = 128
MLA_ROPE = 64
MLA_V = 128
Q_RANK = 384
KV_RANK = 256
ROPE_DIM = 64
ROPE_BASE = 10000.0
D_FF = 5632
CONV_W = 3
N_MOD = 6
EPS = 1e-6
DA_WIDTH = DA_HEADS * HEAD_DIM
MLA_WIDTH = MLA_HEADS * MLA_V
MLA_QK = MLA_NOPE + MLA_ROPE
MLA_QK_PAD = 256
LOG2E = math.log2(math.e)
DA_SCALE = LOG2E / math.sqrt(DA_HALF)
MLA_SCALE = LOG2E / math.sqrt(MLA_QK)
LAMBDA_INIT = 0.8 - 0.6 * math.exp(-0.3 * 0)
T_ALL = CTX_LEN + SEQ

LANES = 128
MOD_ROWS = 8
CTX_ROW = 4
VMEM_LIMIT = 60 * 1024 * 1024

TOK_TILE = 256
ADA_TN = 1024
ATT_TQ = 512
OUT_TM = 512
FFN_TM = 1024
FFN_TF = 512
FFN_SUB = 256
FFN_NOUT = 512
FFN_HALO = 16
FFN_X1_STEPS = 8
FFN_X1_ROWS = FFN_TM // FFN_X1_STEPS

F32 = jnp.float32
BF16 = jnp.bfloat16


def _dot(a, b):
    return jnp.dot(a, b, preferred_element_type=F32)


def _dot_nt(a, b):
    return lax.dot_general(a, b, (((1,), (1,)), ((), ())), preferred_element_type=F32)


def _rms(x):
    return x * lax.rsqrt(jnp.mean(x * x, axis=-1, keepdims=True) + EPS)


def _cparams(n_grid):
    return pltpu.CompilerParams(dimension_semantics=("arbitrary",) * n_grid,
                                vmem_limit_bytes=VMEM_LIMIT)


def _ada_kernel(c_ref, w_ref, b_ref, o_ref):
    c = c_ref[...]
    sc = c * (1.0 / (1.0 + jnp.exp(-c)))
    o_ref[...] = _dot(sc.astype(BF16), w_ref[...].astype(BF16)) + b_ref[...]


def _ada(c8, w_ada, b_ada):
    n = w_ada.shape[1]
    return pl.pallas_call(
        _ada_kernel,
        grid=(n // ADA_TN,),
        in_specs=[pl.BlockSpec((MOD_ROWS, D_MODEL), lambda j: (0, 0)),
                  pl.BlockSpec((D_MODEL, ADA_TN), lambda j: (0, j)),
                  pl.BlockSpec((1, ADA_TN), lambda j: (0, j))],
        out_specs=pl.BlockSpec((MOD_ROWS, ADA_TN), lambda j: (0, j)),
        out_shape=jax.ShapeDtypeStruct((MOD_ROWS, n), F32),
        compiler_params=_cparams(1),
        name="ada",
    )(c8, w_ada, b_ada)


def _rope_chunk(xc, cos, sin, lo_mask):
    up = pltpu.roll(xc, LANES - 16, 1)
    dn = pltpu.roll(xc, 16, 1)
    return xc * cos + jnp.where(lo_mask, up, dn) * sin


def _mixer_kernel(ctx_ref, x_ref, sh_ref, sc_ref, n1_ref, cos_ref, sin_ref, win_ref, wtail_ref,
                  qn_ref, kvn_ref, wuq_ref, wukv_ref,
                  qda_ref, kda_ref, vda_ref, qm_ref, km_ref, vm_ref):
    t = pl.program_id(1)
    is_ctx = t == 0
    xt = jnp.where(is_ctx, ctx_ref[0], x_ref[0])
    h = _rms(xt) * n1_ref[...]
    h = h * (1.0 + sc_ref[0]) + sh_ref[0]
    hb = h.astype(BF16)
    cos = jnp.where(is_ctx, 1.0, cos_ref[...])
    sin = jnp.where(is_ctx, 0.0, sin_ref[...])
    lane = lax.broadcasted_iota(jnp.int32, (TOK_TILE, LANES), 1)
    lo_mask = (lane % 32) < 16
    o1, o2, o3 = DA_WIDTH, 2 * DA_WIDTH, 3 * DA_WIDTH
    o4 = o3 + Q_RANK
    o5 = o4 + KV_RANK

    k = _dot(hb, win_ref[:, o1:o2].astype(BF16))
    for hh in range(DA_HEADS):
        sl = slice(hh * LANES, (hh + 1) * LANES)
        kda_ref[0, :, sl] = _rope_chunk(k[:, sl], cos, sin, lo_mask).astype(BF16)
    vda_ref[0] = _dot(hb, win_ref[:, o2:o3].astype(BF16)).astype(BF16)

    w_low = jnp.concatenate([win_ref[:, o3:o5].astype(BF16), wtail_ref[...].astype(BF16)], axis=1)
    low = _dot(hb, w_low)
    ckv = _rms(low[:, Q_RANK:Q_RANK + KV_RANK]) * kvn_ref[...]
    kv = _dot(ckv.astype(BF16), wukv_ref[...])
    kr = _rope_chunk(low[:, Q_RANK + KV_RANK:], cos, sin, lo_mask).astype(BF16)
    for hh in range(MLA_HEADS):
        base = hh * MLA_QK_PAD
        km_ref[0, :, base:base + LANES] = kv[:, hh * LANES:(hh + 1) * LANES].astype(BF16)
        km_ref[0, :, base + LANES:base + 2 * LANES] = kr
    vm_ref[0] = kv[:, MLA_WIDTH:].astype(BF16)

    q = _dot(hb, win_ref[:, 0:o1].astype(BF16))
    for hh in range(DA_HEADS):
        sl = slice(hh * LANES, (hh + 1) * LANES)
        qda_ref[0, :, sl] = (_rope_chunk(q[:, sl], cos, sin, lo_mask) * DA_SCALE).astype(BF16)
    cq = _rms(low[:, :Q_RANK]) * qn_ref[...]
    qm = _dot(cq.astype(BF16), wuq_ref[...])
    for hh in range(MLA_HEADS):
        base = hh * MLA_QK_PAD
        qm_ref[0, :, base:base + LANES] = (qm[:, base:base + LANES] * MLA_SCALE).astype(BF16)
        qr = _rope_chunk(qm[:, base + LANES:base + 2 * LANES], cos, sin, lo_mask)
        qm_ref[0, :, base + LANES:base + 2 * LANES] = (qr * MLA_SCALE).astype(BF16)


def _mixer_inputs(ctx, x, mod3, n1, cos_t, sin_t, w_in, w_tail, qn, kvn, wuq_b, wukv_b):
    B = x.shape[0]
    nt = T_ALL // TOK_TILE

    def lat(t):
        return jnp.maximum(t - 1, 0)

    def const(shape):
        return pl.BlockSpec(shape, lambda b, t: (0,) * len(shape), pipeline_mode=pl.Buffered(1))

    in_specs = [
        pl.BlockSpec((1, CTX_LEN, D_MODEL), lambda b, t: (b, 0, 0)),
        pl.BlockSpec((1, TOK_TILE, D_MODEL), lambda b, t: (b, lat(t), 0)),
        pl.BlockSpec((1, 1, D_MODEL), lambda b, t: (jnp.where(t == 0, CTX_ROW, b), 0, 0)),
        pl.BlockSpec((1, 1, D_MODEL), lambda b, t: (jnp.where(t == 0, CTX_ROW, b), 0, 1)),
        const((1, D_MODEL)),
        pl.BlockSpec((TOK_TILE, LANES), lambda b, t: (lat(t), 0)),
        pl.BlockSpec((TOK_TILE, LANES), lambda b, t: (lat(t), 0)),
        const(w_in.shape),
        const(w_tail.shape),
        const((1, Q_RANK)),
        const((1, KV_RANK)),
        const(wuq_b.shape),
        const(wukv_b.shape),
    ]
    out_specs = [
        pl.BlockSpec((1, TOK_TILE, DA_WIDTH), lambda b, t: (b, lat(t), 0)),
        pl.BlockSpec((1, TOK_TILE, DA_WIDTH), lambda b, t: (b, t, 0)),
        pl.BlockSpec((1, TOK_TILE, DA_WIDTH), lambda b, t: (b, t, 0)),
        pl.BlockSpec((1, TOK_TILE, MLA_HEADS * MLA_QK_PAD), lambda b, t: (b, lat(t), 0)),
        pl.BlockSpec((1, TOK_TILE, MLA_HEADS * MLA_QK_PAD), lambda b, t: (b, t, 0)),
        pl.BlockSpec((1, TOK_TILE, MLA_WIDTH), lambda b, t: (b, t, 0)),
    ]
    out_shape = [
        jax.ShapeDtypeStruct((B, SEQ, DA_WIDTH), BF16),
        jax.ShapeDtypeStruct((B, T_ALL, DA_WIDTH), BF16),
        jax.ShapeDtypeStruct((B, T_ALL, DA_WIDTH), BF16),
        jax.ShapeDtypeStruct((B, SEQ, MLA_HEADS * MLA_QK_PAD), BF16),
        jax.ShapeDtypeStruct((B, T_ALL, MLA_HEADS * MLA_QK_PAD), BF16),
        jax.ShapeDtypeStruct((B, T_ALL, MLA_WIDTH), BF16),
    ]
    return pl.pallas_call(
        _mixer_kernel,
        grid=(B, nt),
        in_specs=in_specs,
        out_specs=out_specs,
        out_shape=out_shape,
        compiler_params=_cparams(2),
        name="mixer_in",
    )(ctx, x, mod3, mod3, n1, cos_t, sin_t, w_in, w_tail, qn, kvn, wuq_b, wukv_b)


def _fill_values_with_ones(v_ref, vx_ref):
    width = v_ref.shape[-1]
    vx_ref[:, 0:width] = v_ref[0]
    ones_lane = lax.broadcasted_iota(jnp.int32, (T_ALL, width), 1) == 0
    vx_ref[:, width:] = jnp.where(ones_lane, 1.0, 0.0).astype(BF16)


def _softmax_weighted(s, vx_ref):
    width = vx_ref.shape[-1] // 2
    p = jnp.exp2(s - jnp.max(s, axis=-1, keepdims=True))
    ox = _dot(p.astype(BF16), vx_ref[...])
    return ox[:, 0:width] * (1.0 / ox[:, width:width + 1])


def _pipelined_tiles(scores, finish):
    n_tiles = SEQ // ATT_TQ
    s = scores(0)
    for n in range(n_tiles):
        s_next = scores(n + 1) if n + 1 < n_tiles else None
        finish(n, s)
        s = s_next


def _diff_attn_kernel(q_ref, k_ref, v_ref, lq1_ref, lk1_ref, lq2_ref, lk2_ref, sub_ref, o_ref,
                      vx_ref):
    lam = (jnp.exp(jnp.sum(lq1_ref[...] * lk1_ref[...], axis=-1, keepdims=True))
           - jnp.exp(jnp.sum(lq2_ref[...] * lk2_ref[...], axis=-1, keepdims=True))
           + LAMBDA_INIT)
    lane = lax.broadcasted_iota(jnp.int32, (ATT_TQ, HEAD_DIM), 1)
    _fill_values_with_ones(v_ref, vx_ref)

    def scores(n):
        q = q_ref[0, n * ATT_TQ:(n + 1) * ATT_TQ, :]
        zero = jnp.zeros_like(q)
        q1 = jnp.where(lane < DA_HALF, q, zero)
        q2 = jnp.where(lane >= DA_HALF, q, zero)
        return _dot_nt(q1, k_ref[0]), _dot_nt(q2, k_ref[0])

    def finish(n, s):
        o = _softmax_weighted(s[0], vx_ref) - lam * _softmax_weighted(s[1], vx_ref)
        o = _rms(o) * sub_ref[...] * (1.0 - LAMBDA_INIT)
        o_ref[0, n * ATT_TQ:(n + 1) * ATT_TQ, :] = o.astype(BF16)

    _pipelined_tiles(scores, finish)


def _diff_attn(q, k, v, lq1, lk1, lq2, lk2, subln):
    B = q.shape[0]
    small = lambda n: pl.BlockSpec((1, n), lambda b, h: (0, 0))
    return pl.pallas_call(
        _diff_attn_kernel,
        grid=(B, DA_HEADS),
        in_specs=[pl.BlockSpec((1, SEQ, HEAD_DIM), lambda b, h: (b, 0, h)),
                  pl.BlockSpec((1, T_ALL, HEAD_DIM), lambda b, h: (b, 0, h)),
                  pl.BlockSpec((1, T_ALL, HEAD_DIM), lambda b, h: (b, 0, h)),
                  small(DA_HALF), small(DA_HALF), small(DA_HALF), small(DA_HALF),
                  small(HEAD_DIM)],
        out_specs=pl.BlockSpec((1, SEQ, HEAD_DIM), lambda b, h: (b, 0, h)),
        out_shape=jax.ShapeDtypeStruct((B, SEQ, DA_WIDTH), BF16),
        scratch_shapes=[pltpu.VMEM((T_ALL, 2 * HEAD_DIM), BF16)],
        compiler_params=_cparams(2),
        name="diff_attn",
    )(q, k, v, lq1, lk1, lq2, lk2, subln)


def _mla_attn_kernel(q_ref, k_ref, v_ref, o_ref, vx_ref):
    _fill_values_with_ones(v_ref, vx_ref)

    def scores(n):
        return _dot_nt(q_ref[0, n * ATT_TQ:(n + 1) * ATT_TQ, :], k_ref[0])

    def finish(n, s):
        o_ref[0, n * ATT_TQ:(n + 1) * ATT_TQ, :] = _softmax_weighted(s, vx_ref).astype(BF16)

    _pipelined_tiles(scores, finish)


def _mla_attn(q, k, v):
    B = q.shape[0]
    return pl.pallas_call(
        _mla_attn_kernel,
        grid=(B, MLA_HEADS),
        in_specs=[pl.BlockSpec((1, SEQ, MLA_QK_PAD), lambda b, h: (b, 0, h)),
                  pl.BlockSpec((1, T_ALL, MLA_QK_PAD), lambda b, h: (b, 0, h)),
                  pl.BlockSpec((1, T_ALL, MLA_V), lambda b, h: (b, 0, h))],
        out_specs=pl.BlockSpec((1, SEQ, MLA_V), lambda b, h: (b, 0, h)),
        out_shape=jax.ShapeDtypeStruct((B, SEQ, MLA_WIDTH), BF16),
        scratch_shapes=[pltpu.VMEM((T_ALL, 2 * MLA_V), BF16)],
        compiler_params=_cparams(2),
        name="mla_attn",
    )(q, k, v)


def _out_proj_kernel(oda_ref, omla_ref, wo_ref, x_ref, g1_ref, sh2_ref, sc2_ref, n2_ref,
                     x1_ref, h2_ref):
    y = (_dot(oda_ref[0], wo_ref[0:DA_WIDTH, :].astype(BF16))
         + _dot(omla_ref[0], wo_ref[DA_WIDTH:, :].astype(BF16)))
    x1 = x_ref[0] + g1_ref[0] * y
    x1_ref[0] = x1
    h2 = _rms(x1) * n2_ref[...]
    h2_ref[0] = (h2 * (1.0 + sc2_ref[0]) + sh2_ref[0]).astype(BF16)


def _out_proj(o_da, o_mla, w_o, x, mod3, n2):
    B = x.shape[0]
    tile = lambda w: pl.BlockSpec((1, OUT_TM, w), lambda b, i: (b, i, 0))
    modrow = lambda col: pl.BlockSpec((1, 1, D_MODEL), lambda b, i: (b, 0, col))
    return pl.pallas_call(
        _out_proj_kernel,
        grid=(B, SEQ // OUT_TM),
        in_specs=[tile(DA_WIDTH), tile(MLA_WIDTH),
                  pl.BlockSpec(w_o.shape, lambda b, i: (0, 0), pipeline_mode=pl.Buffered(1)),
                  tile(D_MODEL), modrow(2), modrow(3), modrow(4),
                  pl.BlockSpec((1, D_MODEL), lambda b, i: (0, 0))],
        out_specs=[tile(D_MODEL), tile(D_MODEL)],
        out_shape=[jax.ShapeDtypeStruct((B, SEQ, D_MODEL), F32),
                   jax.ShapeDtypeStruct((B, SEQ, D_MODEL), BF16)],
        compiler_params=_cparams(2),
        name="out_proj",
    )(o_da, o_mla, w_o, x, mod3, mod3, mod3, n2)


def _ffn_kernel(h_ref, top_ref, bot_ref, wg_ref, wu_ref, cw_ref, cb_ref, wd_ref, x1_ref, g2_ref,
                fw_ref, o_ref, hs_ref):
    i = pl.program_id(1)
    j = pl.program_id(2)
    last_i = pl.num_programs(1) - 1
    last_j = pl.num_programs(2) - 1

    @pl.when(j == 0)
    def _():
        top = top_ref[0]
        bot = bot_ref[0]
        hs_ref[0:FFN_HALO, :] = jnp.where(i == 0, jnp.zeros_like(top), top)
        hs_ref[FFN_HALO:FFN_HALO + FFN_TM, :] = h_ref[0]
        hs_ref[FFN_HALO + FFN_TM:, :] = jnp.where(i == last_i, jnp.zeros_like(bot), bot)
        o_ref[0] = jnp.zeros((FFN_TM, D_MODEL), F32)

    @pl.when(j < FFN_X1_STEPS)
    def _():
        rows = pl.ds(pl.multiple_of(j * FFN_X1_ROWS, FFN_X1_ROWS), FFN_X1_ROWS)
        o_ref[0, rows, :] += x1_ref[0]

    lo = FFN_HALO
    g2 = g2_ref[0]
    for c in range(FFN_TF // FFN_SUB):
        cs = slice(c * FFN_SUB, (c + 1) * FFN_SUB)
        g = _dot(hs_ref[...], wg_ref[:, cs].astype(BF16))
        u = _dot(hs_ref[lo:lo + FFN_TM, :], wu_ref[:, cs].astype(BF16))
        g_prev = pltpu.roll(g, 1, 0)
        g_next = pltpu.roll(g, FFN_TM + 2 * FFN_HALO - 1, 0)
        gc = (g_prev[lo:lo + FFN_TM] * cw_ref[0:1, cs]
              + g[lo:lo + FFN_TM] * cw_ref[1:2, cs]
              + g_next[lo:lo + FFN_TM] * cw_ref[2:3, cs]
              + cb_ref[:, cs])
        act = (gc * (1.0 / (1.0 + jnp.exp(-gc))) * u).astype(BF16)
        for n in range(D_MODEL // FFN_NOUT):
            ns = slice(n * FFN_NOUT, (n + 1) * FFN_NOUT)
            o_ref[0, :, ns] += g2[:, ns] * _dot(act, wd_ref[cs, ns].astype(BF16))

    @pl.when(j == last_j)
    def _():
        o_ref[0] = _rms(o_ref[0]) * fw_ref[...]


def _conv_ffn_final(h2, w_up, conv_w, conv_b, w_down, x1, mod3, final_w):
    B = h2.shape[0]
    nf = D_FF // FFN_TF
    ni = SEQ // FFN_TM
    assert nf >= FFN_X1_STEPS
    halo_per_tile = FFN_TM // FFN_HALO
    n_halo = SEQ // FFN_HALO
    return pl.pallas_call(
        _ffn_kernel,
        grid=(B, ni, nf),
        in_specs=[
            pl.BlockSpec((1, FFN_TM, D_MODEL), lambda b, i, j: (b, i, 0),
                         pipeline_mode=pl.Buffered(1)),
            pl.BlockSpec((1, FFN_HALO, D_MODEL),
                         lambda b, i, j: (b, jnp.maximum(i * halo_per_tile - 1, 0), 0)),
            pl.BlockSpec((1, FFN_HALO, D_MODEL),
                         lambda b, i, j: (b, jnp.minimum((i + 1) * halo_per_tile, n_halo - 1), 0)),
            pl.BlockSpec((D_MODEL, FFN_TF), lambda b, i, j: (0, j)),
            pl.BlockSpec((D_MODEL, FFN_TF), lambda b, i, j: (0, nf + j)),
            pl.BlockSpec((CONV_W, FFN_TF), lambda b, i, j: (0, j)),
            pl.BlockSpec((1, FFN_TF), lambda b, i, j: (0, j)),
            pl.BlockSpec((FFN_TF, D_MODEL), lambda b, i, j: (j, 0)),
            pl.BlockSpec((1, FFN_X1_ROWS, D_MODEL),
                         lambda b, i, j: (b, i * FFN_X1_STEPS + jnp.minimum(j, FFN_X1_STEPS - 1), 0)),
            pl.BlockSpec((1, 1, D_MODEL), lambda b, i, j: (b, 0, 5)),
            pl.BlockSpec((1, D_MODEL), lambda b, i, j: (0, 0)),
        ],
        out_specs=pl.BlockSpec((1, FFN_TM, D_MODEL), lambda b, i, j: (b, i, 0)),
        out_shape=jax.ShapeDtypeStruct((B, SEQ, D_MODEL), F32),
        scratch_shapes=[pltpu.VMEM((FFN_TM + 2 * FFN_HALO, D_MODEL), BF16)],
        compiler_params=_cparams(3),
        name="conv_ffn",
    )(h2, h2, h2, w_up, w_up, conv_w, conv_b, w_down, x1, mod3, final_w)


def _rope_tables():
    pos = jnp.arange(SEQ)
    row = (pos // GRID_W).astype(F32)
    col = (pos % GRID_W).astype(F32)
    nf = ROPE_DIM // 4
    inv = ROPE_BASE ** (-jnp.arange(nf, dtype=F32) / nf)
    ar = row[:, None] * inv
    ac = col[:, None] * inv
    cos = jnp.concatenate([jnp.cos(ar), jnp.cos(ar), jnp.cos(ac), jnp.cos(ac)], axis=-1)
    sin = jnp.concatenate([-jnp.sin(ar), jnp.sin(ar), -jnp.sin(ac), jnp.sin(ac)], axis=-1)
    return jnp.tile(cos, (1, LANES // ROPE_DIM)), jnp.tile(sin, (1, LANES // ROPE_DIM))


def kernel(x, c, ctx, c_ctx, w_ada, b_ada, norm1_w, w_in, q_norm_w, kv_norm_w, w_uq, w_ukv,
           lambda_q1, lambda_k1, lambda_q2, lambda_k2, subln_w, w_o, norm2_w, w_up,
           conv_w, conv_b, w_down, final_w):
    B = x.shape[0]
    assert B <= CTX_ROW and x.shape == (B, SEQ, D_MODEL) and ctx.shape == (B, CTX_LEN, D_MODEL)
    l = 0
    c8 = jnp.concatenate([c, jnp.zeros((CTX_ROW - B, D_MODEL), F32), c_ctx[None, :],
                          jnp.zeros((MOD_ROWS - CTX_ROW - 1, D_MODEL), F32)], axis=0)
    mod = _ada(c8, w_ada[l], b_ada[l][None, :])
    mod3 = mod.reshape(MOD_ROWS, 1, N_MOD * D_MODEL)

    w_tail = jnp.pad(w_in[l][:, 3 * DA_WIDTH + Q_RANK + KV_RANK:], ((0, 0), (0, LANES - MLA_ROPE)))
    wuq_b = jnp.pad(w_uq[l].reshape(Q_RANK, MLA_HEADS, MLA_QK),
                    ((0, 0), (0, 0), (0, MLA_QK_PAD - MLA_QK))
                    ).reshape(Q_RANK, MLA_HEADS * MLA_QK_PAD).astype(BF16)
    wukv3 = w_ukv[l].reshape(KV_RANK, MLA_HEADS, MLA_NOPE + MLA_V)
    wukv_b = jnp.concatenate([wukv3[:, :, :MLA_NOPE].reshape(KV_RANK, MLA_HEADS * MLA_NOPE),
                              wukv3[:, :, MLA_NOPE:].reshape(KV_RANK, MLA_WIDTH)],
                             axis=1).astype(BF16)
    cos_t, sin_t = _rope_tables()

    q_da, k_da, v_da, q_mla, k_mla, v_mla = _mixer_inputs(
        ctx, x, mod3, norm1_w[l][None, :], cos_t, sin_t, w_in[l], w_tail,
        q_norm_w[l][None, :], kv_norm_w[l][None, :], wuq_b, wukv_b)
    o_da = _diff_attn(q_da, k_da, v_da, lambda_q1[l][None, :], lambda_k1[l][None, :],
                      lambda_q2[l][None, :], lambda_k2[l][None, :], subln_w[l][None, :])
    o_mla = _mla_attn(q_mla, k_mla, v_mla)
    x1, h2 = _out_proj(o_da, o_mla, w_o[l], x, mod3, norm2_w[l][None, :])
    return _conv_ffn_final(h2, w_up[l], conv_w[l], conv_b[l][None, :],
                           w_down[l], x1, mod3, final_w[None, :])
```

```python
import functools
import math

import jax
import jax.numpy as jnp
from jax import lax
from jax.experimental import pallas as pl
from jax.experimental.pallas import tpu as pltpu

D_MODEL = 2048
SEQ = 2048
CTX_LEN = 256
GRID_W = 64
HEAD_DIM = 128
DA_HEADS = 8
DA_HALF = 64
MLA_HEADS = 8
MLA_NOPE = 128
MLA_ROPE = 64
MLA_V = 128
Q_RANK = 384
KV_RANK = 256
ROPE_DIM = 64
ROPE_BASE = 10000.0
D_FF = 5632
CONV_W = 3
N_MOD = 6
EPS = 1e-6
DA_WIDTH = DA_HEADS * HEAD_DIM
MLA_WIDTH = MLA_HEADS * MLA_V
MLA_QK = MLA_NOPE + MLA_ROPE
MLA_QK_PAD = 256
LOG2E = math.log2(math.e)
DA_SCALE = LOG2E / math.sqrt(DA_HALF)
MLA_SCALE = LOG2E / math.sqrt(MLA_QK)
LAMBDA_INIT = 0.8 - 0.6 * math.exp(-0.3 * 0)
T_ALL = CTX_LEN + SEQ

LANES = 128
MOD_ROWS = 8
CTX_ROW = 4
VMEM_LIMIT = 60 * 1024 * 1024

TOK_TILE = 256
ADA_TN = 1024
ATT_TQ = 256
OUT_TM = 512
FFN_TM = 1024
FFN_TF = 512
FFN_SUB = 256
FFN_NOUT = 512
FFN_HALO = 16
FFN_X1_STEPS = 8
FFN_X1_ROWS = FFN_TM // FFN_X1_STEPS

F32 = jnp.float32
BF16 = jnp.bfloat16


def _dot(a, b):
    return jnp.dot(a, b, preferred_element_type=F32)


def _dot_nt(a, b):
    return lax.dot_general(a, b, (((1,), (1,)), ((), ())), preferred_element_type=F32)


def _rms(x):
    return x * lax.rsqrt(jnp.mean(x * x, axis=-1, keepdims=True) + EPS)


def _cparams(n_grid):
    return pltpu.CompilerParams(dimension_semantics=("arbitrary",) * n_grid,
                                vmem_limit_bytes=VMEM_LIMIT)


def _ada_kernel(c_ref, w_ref, b_ref, o_ref):
    c = c_ref[...]
    sc = c * (1.0 / (1.0 + jnp.exp(-c)))
    o_ref[...] = _dot(sc.astype(BF16), w_ref[...].astype(BF16)) + b_ref[...]


def _ada(c8, w_ada, b_ada):
    n = w_ada.shape[1]
    return pl.pallas_call(
        _ada_kernel,
        grid=(n // ADA_TN,),
        in_specs=[pl.BlockSpec((MOD_ROWS, D_MODEL), lambda j: (0, 0)),
                  pl.BlockSpec((D_MODEL, ADA_TN), lambda j: (0, j)),
                  pl.BlockSpec((1, ADA_TN), lambda j: (0, j))],
        out_specs=pl.BlockSpec((MOD_ROWS, ADA_TN), lambda j: (0, j)),
        out_shape=jax.ShapeDtypeStruct((MOD_ROWS, n), F32),
        compiler_params=_cparams(1),
        name="ada",
    )(c8, w_ada, b_ada)


def _rope_chunk(xc, cos, sin, lo_mask):
    up = pltpu.roll(xc, LANES - 16, 1)
    dn = pltpu.roll(xc, 16, 1)
    return xc * cos + jnp.where(lo_mask, up, dn) * sin


def _mixer_kernel(ctx_ref, x_ref, sh_ref, sc_ref, n1_ref, cos_ref, sin_ref, win_ref, wtail_ref,
                  qn_ref, kvn_ref, wuq_ref, wukv_ref,
                  qda_ref, kda_ref, vda_ref, qm_ref, km_ref, vm_ref):
    t = pl.program_id(1)
    is_ctx = t == 0
    xt = jnp.where(is_ctx, ctx_ref[0], x_ref[0])
    h = _rms(xt) * n1_ref[...]
    h = h * (1.0 + sc_ref[0]) + sh_ref[0]
    hb = h.astype(BF16)
    cos = jnp.where(is_ctx, 1.0, cos_ref[...])
    sin = jnp.where(is_ctx, 0.0, sin_ref[...])
    lane = lax.broadcasted_iota(jnp.int32, (TOK_TILE, LANES), 1)
    lo_mask = (lane % 32) < 16
    o1, o2, o3 = DA_WIDTH, 2 * DA_WIDTH, 3 * DA_WIDTH
    o4 = o3 + Q_RANK
    o5 = o4 + KV_RANK

    k = _dot(hb, win_ref[:, o1:o2].astype(BF16))
    for hh in range(DA_HEADS):
        sl = slice(hh * LANES, (hh + 1) * LANES)
        kda_ref[0, hh] = _rope_chunk(k[:, sl], cos, sin, lo_mask).astype(BF16)
    v = _dot(hb, win_ref[:, o2:o3].astype(BF16))
    for hh in range(DA_HEADS):
        vda_ref[0, hh] = v[:, hh * LANES:(hh + 1) * LANES].astype(BF16)

    w_low = jnp.concatenate([win_ref[:, o3:o5].astype(BF16), wtail_ref[...].astype(BF16)], axis=1)
    low = _dot(hb, w_low)
    ckv = _rms(low[:, Q_RANK:Q_RANK + KV_RANK]) * kvn_ref[...]
    kv = _dot(ckv.astype(BF16), wukv_ref[...])
    kr = _rope_chunk(low[:, Q_RANK + KV_RANK:], cos, sin, lo_mask).astype(BF16)
    for hh in range(MLA_HEADS):
        base = hh * MLA_QK_PAD
        km_ref[0, hh, :, 0:LANES] = kv[:, hh * LANES:(hh + 1) * LANES].astype(BF16)
        km_ref[0, hh, :, LANES:] = kr
        vm_ref[0, hh] = kv[:, MLA_WIDTH + hh * LANES:MLA_WIDTH + (hh + 1) * LANES].astype(BF16)

    q = _dot(hb, win_ref[:, 0:o1].astype(BF16))
    for hh in range(DA_HEADS):
        sl = slice(hh * LANES, (hh + 1) * LANES)
        qda_ref[0, hh] = (_rope_chunk(q[:, sl], cos, sin, lo_mask) * DA_SCALE).astype(BF16)
    cq = _rms(low[:, :Q_RANK]) * qn_ref[...]
    qm = _dot(cq.astype(BF16), wuq_ref[...])
    for hh in range(MLA_HEADS):
        base = hh * MLA_QK_PAD
        qm_ref[0, hh, :, 0:LANES] = (qm[:, base:base + LANES] * MLA_SCALE).astype(BF16)
        qr = _rope_chunk(qm[:, base + LANES:base + 2 * LANES], cos, sin, lo_mask)
        qm_ref[0, hh, :, LANES:] = (qr * MLA_SCALE).astype(BF16)


def _mixer_inputs(ctx, x, mod3, n1, cos_t, sin_t, w_in, w_tail, qn, kvn, wuq_b, wukv_b):
    B = x.shape[0]
    nt = T_ALL // TOK_TILE

    def lat(t):
        return jnp.maximum(t - 1, 0)

    def const(shape):
        return pl.BlockSpec(shape, lambda b, t: (0,) * len(shape), pipeline_mode=pl.Buffered(1))

    in_specs = [
        pl.BlockSpec((1, CTX_LEN, D_MODEL), lambda b, t: (b, 0, 0)),
        pl.BlockSpec((1, TOK_TILE, D_MODEL), lambda b, t: (b, lat(t), 0)),
        pl.BlockSpec((1, 1, D_MODEL), lambda b, t: (jnp.where(t == 0, CTX_ROW, b), 0, 0)),
        pl.BlockSpec((1, 1, D_MODEL), lambda b, t: (jnp.where(t == 0, CTX_ROW, b), 0, 1)),
        const((1, D_MODEL)),
        pl.BlockSpec((TOK_TILE, LANES), lambda b, t: (lat(t), 0)),
        pl.BlockSpec((TOK_TILE, LANES), lambda b, t: (lat(t), 0)),
        const(w_in.shape),
        const(w_tail.shape),
        const((1, Q_RANK)),
        const((1, KV_RANK)),
        const(wuq_b.shape),
        const(wukv_b.shape),
    ]
    out_specs = [
        pl.BlockSpec((1, DA_HEADS, TOK_TILE, HEAD_DIM), lambda b, t: (b, 0, lat(t), 0)),
        pl.BlockSpec((1, DA_HEADS, TOK_TILE, HEAD_DIM), lambda b, t: (b, 0, t, 0)),
        pl.BlockSpec((1, DA_HEADS, TOK_TILE, HEAD_DIM), lambda b, t: (b, 0, t, 0)),
        pl.BlockSpec((1, MLA_HEADS, TOK_TILE, MLA_QK_PAD), lambda b, t: (b, 0, lat(t), 0)),
        pl.BlockSpec((1, MLA_HEADS, TOK_TILE, MLA_QK_PAD), lambda b, t: (b, 0, t, 0)),
        pl.BlockSpec((1, MLA_HEADS, TOK_TILE, MLA_V), lambda b, t: (b, 0, t, 0)),
    ]
    out_shape = [
        jax.ShapeDtypeStruct((B, DA_HEADS, SEQ, HEAD_DIM), BF16),
        jax.ShapeDtypeStruct((B, DA_HEADS, T_ALL, HEAD_DIM), BF16),
        jax.ShapeDtypeStruct((B, DA_HEADS, T_ALL, HEAD_DIM), BF16),
        jax.ShapeDtypeStruct((B, MLA_HEADS, SEQ, MLA_QK_PAD), BF16),
        jax.ShapeDtypeStruct((B, MLA_HEADS, T_ALL, MLA_QK_PAD), BF16),
        jax.ShapeDtypeStruct((B, MLA_HEADS, T_ALL, MLA_V), BF16),
    ]
    return pl.pallas_call(
        _mixer_kernel,
        grid=(B, nt),
        in_specs=in_specs,
        out_specs=out_specs,
        out_shape=out_shape,
        compiler_params=_cparams(2),
        name="mixer_in",
    )(ctx, x, mod3, mod3, n1, cos_t, sin_t, w_in, w_tail, qn, kvn, wuq_b, wukv_b)


def _fill_values_with_ones(v_ref, vx_ref):
    width = v_ref.shape[-1]
    vx_ref[:, 0:width] = v_ref[0, 0]
    ones_lane = lax.broadcasted_iota(jnp.int32, (T_ALL, width), 1) == 0
    vx_ref[:, width:] = jnp.where(ones_lane, 1.0, 0.0).astype(BF16)


def _softmax_weighted(s, vx_ref):
    width = vx_ref.shape[-1] // 2
    p = jnp.exp2(s - jnp.max(s, axis=-1, keepdims=True))
    ox = _dot(p.astype(BF16), vx_ref[...])
    return ox[:, 0:width] * (1.0 / ox[:, width:width + 1])


def _pipelined_tiles(scores, finish):
    n_tiles = SEQ // ATT_TQ
    s = scores(0)
    for n in range(n_tiles):
        s_next = scores(n + 1) if n + 1 < n_tiles else None
        finish(n, s)
        s = s_next


def _diff_attn_kernel(q_ref, k_ref, v_ref, lq1_ref, lk1_ref, lq2_ref, lk2_ref, sub_ref, o_ref,
                      vx_ref):
    lam = (jnp.exp(jnp.sum(lq1_ref[...] * lk1_ref[...], axis=-1, keepdims=True))
           - jnp.exp(jnp.sum(lq2_ref[...] * lk2_ref[...], axis=-1, keepdims=True))
           + LAMBDA_INIT)
    lane = lax.broadcasted_iota(jnp.int32, (ATT_TQ, HEAD_DIM), 1)
    _fill_values_with_ones(v_ref, vx_ref)

    def scores(n):
        q = q_ref[0, 0, n * ATT_TQ:(n + 1) * ATT_TQ, :]
        zero = jnp.zeros_like(q)
        q1 = jnp.where(lane < DA_HALF, q, zero)
        q2 = jnp.where(lane >= DA_HALF, q, zero)
        return _dot_nt(q1, k_ref[0, 0]), _dot_nt(q2, k_ref[0, 0])

    def finish(n, s):
        o = _softmax_weighted(s[0], vx_ref) - lam * _softmax_weighted(s[1], vx_ref)
        o = _rms(o) * sub_ref[...] * (1.0 - LAMBDA_INIT)
        o_ref[0, n * ATT_TQ:(n + 1) * ATT_TQ, :] = o.astype(BF16)

    _pipelined_tiles(scores, finish)


def _diff_attn(q, k, v, lq1, lk1, lq2, lk2, subln):
    B = q.shape[0]
    small = lambda n: pl.BlockSpec((1, n), lambda b, h: (0, 0))
    return pl.pallas_call(
        _diff_attn_kernel,
        grid=(B, DA_HEADS),
        in_specs=[pl.BlockSpec((1, 1, SEQ, HEAD_DIM), lambda b, h: (b, h, 0, 0)),
                  pl.BlockSpec((1, 1, T_ALL, HEAD_DIM), lambda b, h: (b, h, 0, 0)),
                  pl.BlockSpec((1, 1, T_ALL, HEAD_DIM), lambda b, h: (b, h, 0, 0)),
                  small(DA_HALF), small(DA_HALF), small(DA_HALF), small(DA_HALF),
                  small(HEAD_DIM)],
        out_specs=pl.BlockSpec((1, SEQ, HEAD_DIM), lambda b, h: (b, 0, h)),
        out_shape=jax.ShapeDtypeStruct((B, SEQ, DA_WIDTH), BF16),
        scratch_shapes=[pltpu.VMEM((T_ALL, 2 * HEAD_DIM), BF16)],
        compiler_params=_cparams(2),
        name="diff_attn",
    )(q, k, v, lq1, lk1, lq2, lk2, subln)


def _mla_attn_kernel(q_ref, k_ref, v_ref, o_ref, vx_ref):
    _fill_values_with_ones(v_ref, vx_ref)

    def scores(n):
        return _dot_nt(q_ref[0, 0, n * ATT_TQ:(n + 1) * ATT_TQ, :], k_ref[0, 0])

    def finish(n, s):
        o_ref[0, n * ATT_TQ:(n + 1) * ATT_TQ, :] = _softmax_weighted(s, vx_ref).astype(BF16)

    _pipelined_tiles(scores, finish)


def _mla_attn(q, k, v):
    B = q.shape[0]
    return pl.pallas_call(
        _mla_attn_kernel,
        grid=(B, MLA_HEADS),
        in_specs=[pl.BlockSpec((1, 1, SEQ, MLA_QK_PAD), lambda b, h: (b, h, 0, 0)),
                  pl.BlockSpec((1, 1, T_ALL, MLA_QK_PAD), lambda b, h: (b, h, 0, 0)),
                  pl.BlockSpec((1, 1, T_ALL, MLA_V), lambda b, h: (b, h, 0, 0))],
        out_specs=pl.BlockSpec((1, SEQ, MLA_V), lambda b, h: (b, 0, h)),
        out_shape=jax.ShapeDtypeStruct((B, SEQ, MLA_WIDTH), BF16),
        scratch_shapes=[pltpu.VMEM((T_ALL, 2 * MLA_V), BF16)],
        compiler_params=_cparams(2),
        name="mla_attn",
    )(q, k, v)


def _out_proj_kernel(oda_ref, omla_ref, wo_ref, x_ref, g1_ref, sh2_ref, sc2_ref, n2_ref,
                     x1_ref, h2_ref):
    y = (_dot(oda_ref[0], wo_ref[0:DA_WIDTH, :].astype(BF16))
         + _dot(omla_ref[0], wo_ref[DA_WIDTH:, :].astype(BF16)))
    x1 = x_ref[0] + g1_ref[0] * y
    x1_ref[0] = x1
    h2 = _rms(x1) * n2_ref[...]
    h2_ref[0] = (h2 * (1.0 + sc2_ref[0]) + sh2_ref[0]).astype(BF16)


def _out_proj(o_da, o_mla, w_o, x, mod3, n2):
    B = x.shape[0]
    tile = lambda w: pl.BlockSpec((1, OUT_TM, w), lambda b, i: (b, i, 0))
    modrow = lambda col: pl.BlockSpec((1, 1, D_MODEL), lambda b, i: (b, 0, col))
    return pl.pallas_call(
        _out_proj_kernel,
        grid=(B, SEQ // OUT_TM),
        in_specs=[tile(DA_WIDTH), tile(MLA_WIDTH),
                  pl.BlockSpec(w_o.shape, lambda b, i: (0, 0), pipeline_mode=pl.Buffered(1)),
                  tile(D_MODEL), modrow(2), modrow(3), modrow(4),
                  pl.BlockSpec((1, D_MODEL), lambda b, i: (0, 0))],
        out_specs=[tile(D_MODEL), tile(D_MODEL)],
        out_shape=[jax.ShapeDtypeStruct((B, SEQ, D_MODEL), F32),
                   jax.ShapeDtypeStruct((B, SEQ, D_MODEL), BF16)],
        compiler_params=_cparams(2),
        name="out_proj",
    )(o_da, o_mla, w_o, x, mod3, mod3, mod3, n2)


def _ffn_kernel(h_ref, top_ref, bot_ref, wg_ref, wu_ref, cw_ref, cb_ref, wd_ref, x1_ref, g2_ref,
                fw_ref, o_ref, hs_ref):
    i = pl.program_id(1)
    j = pl.program_id(2)
    last_i = pl.num_programs(1) - 1
    last_j = pl.num_programs(2) - 1

    @pl.when(j == 0)
    def _():
        top = top_ref[0]
        bot = bot_ref[0]
        hs_ref[0:FFN_HALO, :] = jnp.where(i == 0, jnp.zeros_like(top), top)
        hs_ref[FFN_HALO:FFN_HALO + FFN_TM, :] = h_ref[0]
        hs_ref[FFN_HALO + FFN_TM:, :] = jnp.where(i == last_i, jnp.zeros_like(bot), bot)
        o_ref[0] = jnp.zeros((FFN_TM, D_MODEL), F32)

    @pl.when(j < FFN_X1_STEPS)
    def _():
        rows = pl.ds(pl.multiple_of(j * FFN_X1_ROWS, FFN_X1_ROWS), FFN_X1_ROWS)
        o_ref[0, rows, :] += x1_ref[0]

    lo = FFN_HALO
    g2 = g2_ref[0]
    for c in range(FFN_TF // FFN_SUB):
        cs = slice(c * FFN_SUB, (c + 1) * FFN_SUB)
        g = _dot(hs_ref[...], wg_ref[:, cs].astype(BF16))
        u = _dot(hs_ref[lo:lo + FFN_TM, :], wu_ref[:, cs].astype(BF16))
        g_prev = pltpu.roll(g, 1, 0)
        g_next = pltpu.roll(g, FFN_TM + 2 * FFN_HALO - 1, 0)
        gc = (g_prev[lo:lo + FFN_TM] * cw_ref[0:1, cs]
              + g[lo:lo + FFN_TM] * cw_ref[1:2, cs]
              + g_next[lo:lo + FFN_TM] * cw_ref[2:3, cs]
              + cb_ref[:, cs])
        act = (gc * (1.0 / (1.0 + jnp.exp(-gc))) * u).astype(BF16)
        for n in range(D_MODEL // FFN_NOUT):
            ns = slice(n * FFN_NOUT, (n + 1) * FFN_NOUT)
            o_ref[0, :, ns] += g2[:, ns] * _dot(act, wd_ref[cs, ns].astype(BF16))

    @pl.when(j == last_j)
    def _():
        o_ref[0] = _rms(o_ref[0]) * fw_ref[...]


def _conv_ffn_final(h2, w_up, conv_w, conv_b, w_down, x1, mod3, final_w):
    B = h2.shape[0]
    nf = D_FF // FFN_TF
    ni = SEQ // FFN_TM
    assert nf >= FFN_X1_STEPS
    halo_per_tile = FFN_TM // FFN_HALO
    n_halo = SEQ // FFN_HALO
    return pl.pallas_call(
        _ffn_kernel,
        grid=(B, ni, nf),
        in_specs=[
            pl.BlockSpec((1, FFN_TM, D_MODEL), lambda b, i, j: (b, i, 0),
                         pipeline_mode=pl.Buffered(1)),
            pl.BlockSpec((1, FFN_HALO, D_MODEL),
                         lambda b, i, j: (b, jnp.maximum(i * halo_per_tile - 1, 0), 0)),
            pl.BlockSpec((1, FFN_HALO, D_MODEL),
                         lambda b, i, j: (b, jnp.minimum((i + 1) * halo_per_tile, n_halo - 1), 0)),
            pl.BlockSpec((D_MODEL, FFN_TF), lambda b, i, j: (0, j)),
            pl.BlockSpec((D_MODEL, FFN_TF), lambda b, i, j: (0, nf + j)),
            pl.BlockSpec((CONV_W, FFN_TF), lambda b, i, j: (0, j)),
            pl.BlockSpec((1, FFN_TF), lambda b, i, j: (0, j)),
            pl.BlockSpec((FFN_TF, D_MODEL), lambda b, i, j: (j, 0)),
            pl.BlockSpec((1, FFN_X1_ROWS, D_MODEL),
                         lambda b, i, j: (b, i * FFN_X1_STEPS + jnp.minimum(j, FFN_X1_STEPS - 1), 0)),
            pl.BlockSpec((1, 1, D_MODEL), lambda b, i, j: (b, 0, 5)),
            pl.BlockSpec((1, D_MODEL), lambda b, i, j: (0, 0)),
        ],
        out_specs=pl.BlockSpec((1, FFN_TM, D_MODEL), lambda b, i, j: (b, i, 0)),
        out_shape=jax.ShapeDtypeStruct((B, SEQ, D_MODEL), F32),
        scratch_shapes=[pltpu.VMEM((FFN_TM + 2 * FFN_HALO, D_MODEL), BF16)],
        compiler_params=_cparams(3),
        name="conv_ffn",
    )(h2, h2, h2, w_up, w_up, conv_w, conv_b, w_down, x1, mod3, final_w)


def _rope_tables():
    pos = jnp.arange(SEQ)
    row = (pos // GRID_W).astype(F32)
    col = (pos % GRID_W).astype(F32)
    nf = ROPE_DIM // 4
    inv = ROPE_BASE ** (-jnp.arange(nf, dtype=F32) / nf)
    ar = row[:, None] * inv
    ac = col[:, None] * inv
    cos = jnp.concatenate([jnp.cos(ar), jnp.cos(ar), jnp.cos(ac), jnp.cos(ac)], axis=-1)
    sin = jnp.concatenate([-jnp.sin(ar), jnp.sin(ar), -jnp.sin(ac), jnp.sin(ac)], axis=-1)
    return jnp.tile(cos, (1, LANES // ROPE_DIM)), jnp.tile(sin, (1, LANES // ROPE_DIM))


def kernel(x, c, ctx, c_ctx, w_ada, b_ada, norm1_w, w_in, q_norm_w, kv_norm_w, w_uq, w_ukv,
           lambda_q1, lambda_k1, lambda_q2, lambda_k2, subln_w, w_o, norm2_w, w_up,
           conv_w, conv_b, w_down, final_w):
    B = x.shape[0]
    assert B <= CTX_ROW and x.shape == (B, SEQ, D_MODEL) and ctx.shape == (B, CTX_LEN, D_MODEL)
    l = 0
    c8 = jnp.concatenate([c, jnp.zeros((CTX_ROW - B, D_MODEL), F32), c_ctx[None, :],
                          jnp.zeros((MOD_ROWS - CTX_ROW - 1, D_MODEL), F32)], axis=0)
    mod = _ada(c8, w_ada[l], b_ada[l][None, :])
    mod3 = mod.reshape(MOD_ROWS, 1, N_MOD * D_MODEL)

    w_tail = jnp.pad(w_in[l][:, 3 * DA_WIDTH + Q_RANK + KV_RANK:], ((0, 0), (0, LANES - MLA_ROPE)))
    wuq_b = jnp.pad(w_uq[l].reshape(Q_RANK, MLA_HEADS, MLA_QK),
                    ((0, 0), (0, 0), (0, MLA_QK_PAD - MLA_QK))
                    ).reshape(Q_RANK, MLA_HEADS * MLA_QK_PAD).astype(BF16)
    wukv3 = w_ukv[l].reshape(KV_RANK, MLA_HEADS, MLA_NOPE + MLA_V)
    wukv_b = jnp.concatenate([wukv3[:, :, :MLA_NOPE].reshape(KV_RANK, MLA_HEADS * MLA_NOPE),
                              wukv3[:, :, MLA_NOPE:].reshape(KV_RANK, MLA_WIDTH)],
                             axis=1).astype(BF16)
    cos_t, sin_t = _rope_tables()

    q_da, k_da, v_da, q_mla, k_mla, v_mla = _mixer_inputs(
        ctx, x, mod3, norm1_w[l][None, :], cos_t, sin_t, w_in[l], w_tail,
        q_norm_w[l][None, :], kv_norm_w[l][None, :], wuq_b, wukv_b)
    o_da = _diff_attn(q_da, k_da, v_da, lambda_q1[l][None, :], lambda_k1[l][None, :],
                      lambda_q2[l][None, :], lambda_k2[l][None, :], subln_w[l][None, :])
    o_mla = _mla_attn(q_mla, k_mla, v_mla)
    x1, h2 = _out_proj(o_da, o_mla, w_o[l], x, mod3, norm2_w[l][None, :])
    return _conv_ffn_final(h2, w_up[l], conv_w[l], conv_b[l][None, :],
                           w_down[l], x1, mod3, final_w[None, :])
```

```python
import functools
import math

import jax
import jax.numpy as jnp
from jax import lax
from jax.experimental import pallas as pl
from jax.experimental.pallas import tpu as pltpu

D_MODEL = 2048
SEQ = 2048
CTX_LEN = 256
GRID_W = 64
HEAD_DIM = 128
DA_HEADS = 8
DA_HALF = 64
MLA_HEADS = 8
MLA_NOPE = 128
MLA_ROPE = 64
MLA_V = 128
Q_RANK = 384
KV_RANK = 256
ROPE_DIM = 64
ROPE_BASE = 10000.0
D_FF = 5632
CONV_W = 3
N_MOD = 6
EPS = 1e-6
DA_WIDTH = DA_HEADS * HEAD_DIM
MLA_WIDTH = MLA_HEADS * MLA_V
MLA_QK = MLA_NOPE + MLA_ROPE
MLA_QK_PAD = 256
LOG2E = math.log2(math.e)
DA_SCALE = LOG2E / math.sqrt(DA_HALF)
MLA_SCALE = LOG2E / math.sqrt(MLA_QK)
LAMBDA_INIT = 0.8 - 0.6 * math.exp(-0.3 * 0)
T_ALL = CTX_LEN + SEQ

LANES = 128
MOD_ROWS = 8
CTX_ROW = 4
VMEM_LIMIT = 60 * 1024 * 1024

TOK_TILE = 256
ADA_TN = 1024
ATT_TQ = 256
OUT_TM = 512
FFN_TM = 1024
FFN_TF = 512
FFN_SUB = 256
FFN_NOUT = 512
FFN_HALO = 16
FFN_X1_STEPS = 8
FFN_X1_ROWS = FFN_TM // FFN_X1_STEPS

F32 = jnp.float32
BF16 = jnp.bfloat16


def _dot(a, b):
    return jnp.dot(a, b, preferred_element_type=F32)


def _dot_nt(a, b):
    return lax.dot_general(a, b, (((1,), (1,)), ((), ())), preferred_element_type=F32)


def _rms(x):
    return x * lax.rsqrt(jnp.mean(x * x, axis=-1, keepdims=True) + EPS)


def _cparams(n_grid):
    return pltpu.CompilerParams(dimension_semantics=("arbitrary",) * n_grid,
                                vmem_limit_bytes=VMEM_LIMIT)


def _ada_kernel(c_ref, w_ref, b_ref, o_ref):
    c = c_ref[...]
    sc = c * (1.0 / (1.0 + jnp.exp(-c)))
    o_ref[...] = _dot(sc.astype(BF16), w_ref[...].astype(BF16)) + b_ref[...]


def _ada(c8, w_ada, b_ada):
    n = w_ada.shape[1]
    return pl.pallas_call(
        _ada_kernel,
        grid=(n // ADA_TN,),
        in_specs=[pl.BlockSpec((MOD_ROWS, D_MODEL), lambda j: (0, 0)),
                  pl.BlockSpec((D_MODEL, ADA_TN), lambda j: (0, j)),
                  pl.BlockSpec((1, ADA_TN), lambda j: (0, j))],
        out_specs=pl.BlockSpec((MOD_ROWS, ADA_TN), lambda j: (0, j)),
        out_shape=jax.ShapeDtypeStruct((MOD_ROWS, n), F32),
        compiler_params=_cparams(1),
        name="ada",
    )(c8, w_ada, b_ada)


def _rope_chunk(xc, cos, sin, lo_mask):
    up = pltpu.roll(xc, LANES - 16, 1)
    dn = pltpu.roll(xc, 16, 1)
    return xc * cos + jnp.where(lo_mask, up, dn) * sin


def _mixer_kernel(ctx_ref, x_ref, sh_ref, sc_ref, n1_ref, cos_ref, sin_ref, win_ref, wtail_ref,
                  qn_ref, kvn_ref, wuq_ref, wukv_ref,
                  qda_ref, kda_ref, vda_ref, qm_ref, km_ref, vm_ref):
    t = pl.program_id(1)
    is_ctx = t == 0
    xt = jnp.where(is_ctx, ctx_ref[0], x_ref[0])
    h = _rms(xt) * n1_ref[...]
    h = h * (1.0 + sc_ref[0]) + sh_ref[0]
    hb = h.astype(BF16)
    cos = jnp.where(is_ctx, 1.0, cos_ref[...])
    sin = jnp.where(is_ctx, 0.0, sin_ref[...])
    lane = lax.broadcasted_iota(jnp.int32, (TOK_TILE, LANES), 1)
    lo_mask = (lane % 32) < 16
    o1, o2, o3 = DA_WIDTH, 2 * DA_WIDTH, 3 * DA_WIDTH
    o4 = o3 + Q_RANK
    o5 = o4 + KV_RANK

    k = _dot(hb, win_ref[:, o1:o2].astype(BF16))
    for hh in range(DA_HEADS):
        sl = slice(hh * LANES, (hh + 1) * LANES)
        kda_ref[0, hh] = _rope_chunk(k[:, sl], cos, sin, lo_mask).astype(BF16)
    v = _dot(hb, win_ref[:, o2:o3].astype(BF16))
    for hh in range(DA_HEADS):
        vda_ref[0, hh] = v[:, hh * LANES:(hh + 1) * LANES].astype(BF16)

    w_low = jnp.concatenate([win_ref[:, o3:o5].astype(BF16), wtail_ref[...].astype(BF16)], axis=1)
    low = _dot(hb, w_low)
    ckv = _rms(low[:, Q_RANK:Q_RANK + KV_RANK]) * kvn_ref[...]
    kv = _dot(ckv.astype(BF16), wukv_ref[...])
    kr_t = _rope_chunk(low[:, Q_RANK + KV_RANK:], cos, sin, lo_mask).T.astype(BF16)
    for hh in range(MLA_HEADS):
        km_ref[0, hh, 0:LANES, :] = kv[:, hh * LANES:(hh + 1) * LANES].T.astype(BF16)
        km_ref[0, hh, LANES:, :] = kr_t
        vm_ref[0, hh] = kv[:, MLA_WIDTH + hh * LANES:MLA_WIDTH + (hh + 1) * LANES].astype(BF16)

    q = _dot(hb, win_ref[:, 0:o1].astype(BF16))
    for hh in range(DA_HEADS):
        sl = slice(hh * LANES, (hh + 1) * LANES)
        qda_ref[0, hh] = (_rope_chunk(q[:, sl], cos, sin, lo_mask) * DA_SCALE).astype(BF16)
    cq = _rms(low[:, :Q_RANK]) * qn_ref[...]
    qm = _dot(cq.astype(BF16), wuq_ref[...])
    for hh in range(MLA_HEADS):
        base = hh * MLA_QK_PAD
        qm_ref[0, hh, :, 0:LANES] = (qm[:, base:base + LANES] * MLA_SCALE).astype(BF16)
        qr = _rope_chunk(qm[:, base + LANES:base + 2 * LANES], cos, sin, lo_mask)
        qm_ref[0, hh, :, LANES:] = (qr * MLA_SCALE).astype(BF16)


def _mixer_inputs(ctx, x, mod3, n1, cos_t, sin_t, w_in, w_tail, qn, kvn, wuq_b, wukv_b):
    B = x.shape[0]
    nt = T_ALL // TOK_TILE

    def lat(t):
        return jnp.maximum(t - 1, 0)

    def const(shape):
        return pl.BlockSpec(shape, lambda b, t: (0,) * len(shape), pipeline_mode=pl.Buffered(1))

    in_specs = [
        pl.BlockSpec((1, CTX_LEN, D_MODEL), lambda b, t: (b, 0, 0)),
        pl.BlockSpec((1, TOK_TILE, D_MODEL), lambda b, t: (b, lat(t), 0)),
        pl.BlockSpec((1, 1, D_MODEL), lambda b, t: (jnp.where(t == 0, CTX_ROW, b), 0, 0)),
        pl.BlockSpec((1, 1, D_MODEL), lambda b, t: (jnp.where(t == 0, CTX_ROW, b), 0, 1)),
        const((1, D_MODEL)),
        pl.BlockSpec((TOK_TILE, LANES), lambda b, t: (lat(t), 0)),
        pl.BlockSpec((TOK_TILE, LANES), lambda b, t: (lat(t), 0)),
        const(w_in.shape),
        const(w_tail.shape),
        const((1, Q_RANK)),
        const((1, KV_RANK)),
        const(wuq_b.shape),
        const(wukv_b.shape),
    ]
    out_specs = [
        pl.BlockSpec((1, DA_HEADS, TOK_TILE, HEAD_DIM), lambda b, t: (b, 0, lat(t), 0)),
        pl.BlockSpec((1, DA_HEADS, TOK_TILE, HEAD_DIM), lambda b, t: (b, 0, t, 0)),
        pl.BlockSpec((1, DA_HEADS, TOK_TILE, HEAD_DIM), lambda b, t: (b, 0, t, 0)),
        pl.BlockSpec((1, MLA_HEADS, TOK_TILE, MLA_QK_PAD), lambda b, t: (b, 0, lat(t), 0)),
        pl.BlockSpec((1, MLA_HEADS, MLA_QK_PAD, TOK_TILE), lambda b, t: (b, 0, 0, t)),
        pl.BlockSpec((1, MLA_HEADS, TOK_TILE, MLA_V), lambda b, t: (b, 0, t, 0)),
    ]
    out_shape = [
        jax.ShapeDtypeStruct((B, DA_HEADS, SEQ, HEAD_DIM), BF16),
        jax.ShapeDtypeStruct((B, DA_HEADS, T_ALL, HEAD_DIM), BF16),
        jax.ShapeDtypeStruct((B, DA_HEADS, T_ALL, HEAD_DIM), BF16),
        jax.ShapeDtypeStruct((B, MLA_HEADS, SEQ, MLA_QK_PAD), BF16),
        jax.ShapeDtypeStruct((B, MLA_HEADS, MLA_QK_PAD, T_ALL), BF16),
        jax.ShapeDtypeStruct((B, MLA_HEADS, T_ALL, MLA_V), BF16),
    ]
    return pl.pallas_call(
        _mixer_kernel,
        grid=(B, nt),
        in_specs=in_specs,
        out_specs=out_specs,
        out_shape=out_shape,
        compiler_params=_cparams(2),
        name="mixer_in",
    )(ctx, x, mod3, mod3, n1, cos_t, sin_t, w_in, w_tail, qn, kvn, wuq_b, wukv_b)


def _fill_values_with_ones(v_ref, vx_ref):
    width = v_ref.shape[-1]
    vx_ref[:, 0:width] = v_ref[0, 0]
    ones_lane = lax.broadcasted_iota(jnp.int32, (T_ALL, width), 1) == 0
    vx_ref[:, width:] = jnp.where(ones_lane, 1.0, 0.0).astype(BF16)


def _softmax_weighted(s, vx_ref):
    width = vx_ref.shape[-1] // 2
    p = jnp.exp2(s - jnp.max(s, axis=-1, keepdims=True))
    ox = _dot(p.astype(BF16), vx_ref[...])
    return ox[:, 0:width] * (1.0 / ox[:, width:width + 1])


def _pipelined_tiles(scores, finish):
    n_tiles = SEQ // ATT_TQ
    s = scores(0)
    for n in range(n_tiles):
        s_next = scores(n + 1) if n + 1 < n_tiles else None
        finish(n, s)
        s = s_next


def _diff_attn_kernel(q_ref, k_ref, v_ref, lq1_ref, lk1_ref, lq2_ref, lk2_ref, sub_ref, o_ref,
                      vx_ref):
    lam = (jnp.exp(jnp.sum(lq1_ref[...] * lk1_ref[...], axis=-1, keepdims=True))
           - jnp.exp(jnp.sum(lq2_ref[...] * lk2_ref[...], axis=-1, keepdims=True))
           + LAMBDA_INIT)
    lane = lax.broadcasted_iota(jnp.int32, (ATT_TQ, HEAD_DIM), 1)
    _fill_values_with_ones(v_ref, vx_ref)

    def scores(n):
        q = q_ref[0, 0, n * ATT_TQ:(n + 1) * ATT_TQ, :]
        zero = jnp.zeros_like(q)
        q1 = jnp.where(lane < DA_HALF, q, zero)
        q2 = jnp.where(lane >= DA_HALF, q, zero)
        return _dot_nt(q1, k_ref[0, 0]), _dot_nt(q2, k_ref[0, 0])

    def finish(n, s):
        o = _softmax_weighted(s[0], vx_ref) - lam * _softmax_weighted(s[1], vx_ref)
        o = _rms(o) * sub_ref[...] * (1.0 - LAMBDA_INIT)
        o_ref[0, n * ATT_TQ:(n + 1) * ATT_TQ, :] = o.astype(BF16)

    _pipelined_tiles(scores, finish)


def _diff_attn(q, k, v, lq1, lk1, lq2, lk2, subln):
    B = q.shape[0]
    small = lambda n: pl.BlockSpec((1, n), lambda b, h: (0, 0))
    return pl.pallas_call(
        _diff_attn_kernel,
        grid=(B, DA_HEADS),
        in_specs=[pl.BlockSpec((1, 1, SEQ, HEAD_DIM), lambda b, h: (b, h, 0, 0)),
                  pl.BlockSpec((1, 1, T_ALL, HEAD_DIM), lambda b, h: (b, h, 0, 0)),
                  pl.BlockSpec((1, 1, T_ALL, HEAD_DIM), lambda b, h: (b, h, 0, 0)),
                  small(DA_HALF), small(DA_HALF), small(DA_HALF), small(DA_HALF),
                  small(HEAD_DIM)],
        out_specs=pl.BlockSpec((1, SEQ, HEAD_DIM), lambda b, h: (b, 0, h)),
        out_shape=jax.ShapeDtypeStruct((B, SEQ, DA_WIDTH), BF16),
        scratch_shapes=[pltpu.VMEM((T_ALL, 2 * HEAD_DIM), BF16)],
        compiler_params=_cparams(2),
        name="diff_attn",
    )(q, k, v, lq1, lk1, lq2, lk2, subln)


def _mla_attn_kernel(q_ref, k_ref, v_ref, o_ref, vx_ref):
    _fill_values_with_ones(v_ref, vx_ref)

    def scores(n):
        return _dot(q_ref[0, 0, n * ATT_TQ:(n + 1) * ATT_TQ, :], k_ref[0, 0])

    def finish(n, s):
        o_ref[0, n * ATT_TQ:(n + 1) * ATT_TQ, :] = _softmax_weighted(s, vx_ref).astype(BF16)

    _pipelined_tiles(scores, finish)


def _mla_attn(q, k, v):
    B = q.shape[0]
    return pl.pallas_call(
        _mla_attn_kernel,
        grid=(B, MLA_HEADS),
        in_specs=[pl.BlockSpec((1, 1, SEQ, MLA_QK_PAD), lambda b, h: (b, h, 0, 0)),
                  pl.BlockSpec((1, 1, MLA_QK_PAD, T_ALL), lambda b, h: (b, h, 0, 0)),
                  pl.BlockSpec((1, 1, T_ALL, MLA_V), lambda b, h: (b, h, 0, 0))],
        out_specs=pl.BlockSpec((1, SEQ, MLA_V), lambda b, h: (b, 0, h)),
        out_shape=jax.ShapeDtypeStruct((B, SEQ, MLA_WIDTH), BF16),
        scratch_shapes=[pltpu.VMEM((T_ALL, 2 * MLA_V), BF16)],
        compiler_params=_cparams(2),
        name="mla_attn",
    )(q, k, v)


def _out_proj_kernel(oda_ref, omla_ref, wo_ref, x_ref, g1_ref, sh2_ref, sc2_ref, n2_ref,
                     x1_ref, h2_ref):
    y = (_dot(oda_ref[0], wo_ref[0:DA_WIDTH, :].astype(BF16))
         + _dot(omla_ref[0], wo_ref[DA_WIDTH:, :].astype(BF16)))
    x1 = x_ref[0] + g1_ref[0] * y
    x1_ref[0] = x1
    h2 = _rms(x1) * n2_ref[...]
    h2_ref[0] = (h2 * (1.0 + sc2_ref[0]) + sh2_ref[0]).astype(BF16)


def _out_proj(o_da, o_mla, w_o, x, mod3, n2):
    B = x.shape[0]
    tile = lambda w: pl.BlockSpec((1, OUT_TM, w), lambda b, i: (b, i, 0))
    modrow = lambda col: pl.BlockSpec((1, 1, D_MODEL), lambda b, i: (b, 0, col))
    return pl.pallas_call(
        _out_proj_kernel,
        grid=(B, SEQ // OUT_TM),
        in_specs=[tile(DA_WIDTH), tile(MLA_WIDTH),
                  pl.BlockSpec(w_o.shape, lambda b, i: (0, 0), pipeline_mode=pl.Buffered(1)),
                  tile(D_MODEL), modrow(2), modrow(3), modrow(4),
                  pl.BlockSpec((1, D_MODEL), lambda b, i: (0, 0))],
        out_specs=[tile(D_MODEL), tile(D_MODEL)],
        out_shape=[jax.ShapeDtypeStruct((B, SEQ, D_MODEL), F32),
                   jax.ShapeDtypeStruct((B, SEQ, D_MODEL), BF16)],
        compiler_params=_cparams(2),
        name="out_proj",
    )(o_da, o_mla, w_o, x, mod3, mod3, mod3, n2)


def _ffn_kernel(h_ref, top_ref, bot_ref, wg_ref, wu_ref, cw_ref, cb_ref, wd_ref, x1_ref, g2_ref,
                fw_ref, o_ref, hs_ref):
    i = pl.program_id(1)
    j = pl.program_id(2)
    last_i = pl.num_programs(1) - 1
    last_j = pl.num_programs(2) - 1

    @pl.when(j == 0)
    def _():
        top = top_ref[0]
        bot = bot_ref[0]
        hs_ref[0:FFN_HALO, :] = jnp.where(i == 0, jnp.zeros_like(top), top)
        hs_ref[FFN_HALO:FFN_HALO + FFN_TM, :] = h_ref[0]
        hs_ref[FFN_HALO + FFN_TM:, :] = jnp.where(i == last_i, jnp.zeros_like(bot), bot)
        o_ref[0] = jnp.zeros((FFN_TM, D_MODEL), F32)

    @pl.when(j < FFN_X1_STEPS)
    def _():
        rows = pl.ds(pl.multiple_of(j * FFN_X1_ROWS, FFN_X1_ROWS), FFN_X1_ROWS)
        o_ref[0, rows, :] += x1_ref[0]

    lo = FFN_HALO
    g2 = g2_ref[0]
    for c in range(FFN_TF // FFN_SUB):
        cs = slice(c * FFN_SUB, (c + 1) * FFN_SUB)
        g = _dot(hs_ref[...], wg_ref[:, cs].astype(BF16))
        u = _dot(hs_ref[lo:lo + FFN_TM, :], wu_ref[:, cs].astype(BF16))
        g_prev = pltpu.roll(g, 1, 0)
        g_next = pltpu.roll(g, FFN_TM + 2 * FFN_HALO - 1, 0)
        gc = (g_prev[lo:lo + FFN_TM] * cw_ref[0:1, cs]
              + g[lo:lo + FFN_TM] * cw_ref[1:2, cs]
              + g_next[lo:lo + FFN_TM] * cw_ref[2:3, cs]
              + cb_ref[:, cs])
        act = (gc * (1.0 / (1.0 + jnp.exp(-gc))) * u).astype(BF16)
        for n in range(D_MODEL // FFN_NOUT):
            ns = slice(n * FFN_NOUT, (n + 1) * FFN_NOUT)
            o_ref[0, :, ns] += g2[:, ns] * _dot(act, wd_ref[cs, ns].astype(BF16))

    @pl.when(j == last_j)
    def _():
        o_ref[0] = _rms(o_ref[0]) * fw_ref[...]


def _conv_ffn_final(h2, w_up, conv_w, conv_b, w_down, x1, mod3, final_w):
    B = h2.shape[0]
    nf = D_FF // FFN_TF
    ni = SEQ // FFN_TM
    assert nf >= FFN_X1_STEPS
    halo_per_tile = FFN_TM // FFN_HALO
    n_halo = SEQ // FFN_HALO
    return pl.pallas_call(
        _ffn_kernel,
        grid=(B, ni, nf),
        in_specs=[
            pl.BlockSpec((1, FFN_TM, D_MODEL), lambda b, i, j: (b, i, 0),
                         pipeline_mode=pl.Buffered(1)),
            pl.BlockSpec((1, FFN_HALO, D_MODEL),
                         lambda b, i, j: (b, jnp.maximum(i * halo_per_tile - 1, 0), 0)),
            pl.BlockSpec((1, FFN_HALO, D_MODEL),
                         lambda b, i, j: (b, jnp.minimum((i + 1) * halo_per_tile, n_halo - 1), 0)),
            pl.BlockSpec((D_MODEL, FFN_TF), lambda b, i, j: (0, j)),
            pl.BlockSpec((D_MODEL, FFN_TF), lambda b, i, j: (0, nf + j)),
            pl.BlockSpec((CONV_W, FFN_TF), lambda b, i, j: (0, j)),
            pl.BlockSpec((1, FFN_TF), lambda b, i, j: (0, j)),
            pl.BlockSpec((FFN_TF, D_MODEL), lambda b, i, j: (j, 0)),
            pl.BlockSpec((1, FFN_X1_ROWS, D_MODEL),
                         lambda b, i, j: (b, i * FFN_X1_STEPS + jnp.minimum(j, FFN_X1_STEPS - 1), 0)),
            pl.BlockSpec((1, 1, D_MODEL), lambda b, i, j: (b, 0, 5)),
            pl.BlockSpec((1, D_MODEL), lambda b, i, j: (0, 0)),
        ],
        out_specs=pl.BlockSpec((1, FFN_TM, D_MODEL), lambda b, i, j: (b, i, 0)),
        out_shape=jax.ShapeDtypeStruct((B, SEQ, D_MODEL), F32),
        scratch_shapes=[pltpu.VMEM((FFN_TM + 2 * FFN_HALO, D_MODEL), BF16)],
        compiler_params=_cparams(3),
        name="conv_ffn",
    )(h2, h2, h2, w_up, w_up, conv_w, conv_b, w_down, x1, mod3, final_w)


def _rope_tables():
    pos = jnp.arange(SEQ)
    row = (pos // GRID_W).astype(F32)
    col = (pos % GRID_W).astype(F32)
    nf = ROPE_DIM // 4
    inv = ROPE_BASE ** (-jnp.arange(nf, dtype=F32) / nf)
    ar = row[:, None] * inv
    ac = col[:, None] * inv
    cos = jnp.concatenate([jnp.cos(ar), jnp.cos(ar), jnp.cos(ac), jnp.cos(ac)], axis=-1)
    sin = jnp.concatenate([-jnp.sin(ar), jnp.sin(ar), -jnp.sin(ac), jnp.sin(ac)], axis=-1)
    return jnp.tile(cos, (1, LANES // ROPE_DIM)), jnp.tile(sin, (1, LANES // ROPE_DIM))


def kernel(x, c, ctx, c_ctx, w_ada, b_ada, norm1_w, w_in, q_norm_w, kv_norm_w, w_uq, w_ukv,
           lambda_q1, lambda_k1, lambda_q2, lambda_k2, subln_w, w_o, norm2_w, w_up,
           conv_w, conv_b, w_down, final_w):
    B = x.shape[0]
    assert B <= CTX_ROW and x.shape == (B, SEQ, D_MODEL) and ctx.shape == (B, CTX_LEN, D_MODEL)
    l = 0
    c8 = jnp.concatenate([c, jnp.zeros((CTX_ROW - B, D_MODEL), F32), c_ctx[None, :],
                          jnp.zeros((MOD_ROWS - CTX_ROW - 1, D_MODEL), F32)], axis=0)
    mod = _ada(c8, w_ada[l], b_ada[l][None, :])
    mod3 = mod.reshape(MOD_ROWS, 1, N_MOD * D_MODEL)

    w_tail = jnp.pad(w_in[l][:, 3 * DA_WIDTH + Q_RANK + KV_RANK:], ((0, 0), (0, LANES - MLA_ROPE)))
    wuq_b = jnp.pad(w_uq[l].reshape(Q_RANK, MLA_HEADS, MLA_QK),
                    ((0, 0), (0, 0), (0, MLA_QK_PAD - MLA_QK))
                    ).reshape(Q_RANK, MLA_HEADS * MLA_QK_PAD).astype(BF16)
    wukv3 = w_ukv[l].reshape(KV_RANK, MLA_HEADS, MLA_NOPE + MLA_V)
    wukv_b = jnp.concatenate([wukv3[:, :, :MLA_NOPE].reshape(KV_RANK, MLA_HEADS * MLA_NOPE),
                              wukv3[:, :, MLA_NOPE:].reshape(KV_RANK, MLA_WIDTH)],
                             axis=1).astype(BF16)
    cos_t, sin_t = _rope_tables()

    q_da, k_da, v_da, q_mla, k_mla, v_mla = _mixer_inputs(
        ctx, x, mod3, norm1_w[l][None, :], cos_t, sin_t, w_in[l], w_tail,
        q_norm_w[l][None, :], kv_norm_w[l][None, :], wuq_b, wukv_b)
    o_da = _diff_attn(q_da, k_da, v_da, lambda_q1[l][None, :], lambda_k1[l][None, :],
                      lambda_q2[l][None, :], lambda_k2[l][None, :], subln_w[l][None, :])
    o_mla = _mla_attn(q_mla, k_mla, v_mla)
    x1, h2 = _out_proj(o_da, o_mla, w_o[l], x, mod3, norm2_w[l][None, :])
    return _conv_ffn_final(h2, w_up[l], conv_w[l], conv_b[l][None, :],
                           w_down[l], x1, mod3, final_w[None, :])
```

```python
import functools
import math

import jax
import jax.numpy as jnp
from jax import lax
from jax.experimental import pallas as pl
from jax.experimental.pallas import tpu as pltpu

D_MODEL = 2048
SEQ = 2048
CTX_LEN = 256
GRID_W = 64
HEAD_DIM = 128
DA_HEADS = 8
DA_HALF = 64
MLA_HEADS = 8
MLA_NOPE = 128
MLA_ROPE = 64
MLA_V = 128
Q_RANK = 384
KV_RANK = 256
ROPE_DIM = 64
ROPE_BASE = 10000.0
D_FF = 5632
CONV_W = 3
N_MOD = 6
EPS = 1e-6
DA_WIDTH = DA_HEADS * HEAD_DIM
MLA_WIDTH = MLA_HEADS * MLA_V
MLA_QK = MLA_NOPE + MLA_ROPE
MLA_QK_PAD = 256
LOG2E = math.log2(math.e)
DA_SCALE = LOG2E / math.sqrt(DA_HALF)
MLA_SCALE = LOG2E / math.sqrt(MLA_QK)
LAMBDA_INIT = 0.8 - 0.6 * math.exp(-0.3 * 0)
T_ALL = CTX_LEN + SEQ

LANES = 128
MOD_ROWS = 8
CTX_ROW = 4
VMEM_LIMIT = 60 * 1024 * 1024

TOK_TILE = 256
ADA_TN = 1024
ATT_TQ = 256
OUT_TM = 512
FFN_TM = 1024
FFN_TF = 512
FFN_SUB = 256
FFN_NOUT = 512
FFN_HALO = 16
FFN_X1_STEPS = 8
FFN_X1_ROWS = FFN_TM // FFN_X1_STEPS

F32 = jnp.float32
BF16 = jnp.bfloat16


def _dot(a, b):
    return jnp.dot(a, b, preferred_element_type=F32)


def _dot_nt(a, b):
    return lax.dot_general(a, b, (((1,), (1,)), ((), ())), preferred_element_type=F32)


def _rms(x):
    return x * lax.rsqrt(jnp.mean(x * x, axis=-1, keepdims=True) + EPS)


def _cparams(n_grid):
    return pltpu.CompilerParams(dimension_semantics=("arbitrary",) * n_grid,
                                vmem_limit_bytes=VMEM_LIMIT)


def _ada_kernel(c_ref, w_ref, b_ref, o_ref):
    c = c_ref[...]
    sc = c * (1.0 / (1.0 + jnp.exp(-c)))
    o_ref[...] = _dot(sc.astype(BF16), w_ref[...].astype(BF16)) + b_ref[...]


def _ada(c8, w_ada, b_ada):
    n = w_ada.shape[1]
    return pl.pallas_call(
        _ada_kernel,
        grid=(n // ADA_TN,),
        in_specs=[pl.BlockSpec((MOD_ROWS, D_MODEL), lambda j: (0, 0)),
                  pl.BlockSpec((D_MODEL, ADA_TN), lambda j: (0, j)),
                  pl.BlockSpec((1, ADA_TN), lambda j: (0, j))],
        out_specs=pl.BlockSpec((MOD_ROWS, ADA_TN), lambda j: (0, j)),
        out_shape=jax.ShapeDtypeStruct((MOD_ROWS, n), F32),
        compiler_params=_cparams(1),
        name="ada",
    )(c8, w_ada, b_ada)


def _rope_chunk(xc, cos, sin, lo_mask):
    up = pltpu.roll(xc, LANES - 16, 1)
    dn = pltpu.roll(xc, 16, 1)
    return xc * cos + jnp.where(lo_mask, up, dn) * sin


def _mixer_kernel(ctx_ref, x_ref, sh_ref, sc_ref, n1_ref, cos_ref, sin_ref, win_ref, wtail_ref,
                  qn_ref, kvn_ref, wuq_ref, wukv_ref,
                  qda_ref, kda_ref, vda_ref, qm_ref, km_ref, vm_ref):
    t = pl.program_id(1)
    is_ctx = t == 0
    xt = jnp.where(is_ctx, ctx_ref[0], x_ref[0])
    h = _rms(xt) * n1_ref[...]
    h = h * (1.0 + sc_ref[0]) + sh_ref[0]
    hb = h.astype(BF16)
    cos = jnp.where(is_ctx, 1.0, cos_ref[...])
    sin = jnp.where(is_ctx, 0.0, sin_ref[...])
    lane = lax.broadcasted_iota(jnp.int32, (TOK_TILE, LANES), 1)
    lo_mask = (lane % 32) < 16
    o1, o2, o3 = DA_WIDTH, 2 * DA_WIDTH, 3 * DA_WIDTH
    o4 = o3 + Q_RANK
    o5 = o4 + KV_RANK

    k = _dot(hb, win_ref[:, o1:o2].astype(BF16))
    for hh in range(DA_HEADS):
        sl = slice(hh * LANES, (hh + 1) * LANES)
        kda_ref[0, hh] = _rope_chunk(k[:, sl], cos, sin, lo_mask).astype(BF16)
    v = _dot(hb, win_ref[:, o2:o3].astype(BF16))
    for hh in range(DA_HEADS):
        vda_ref[0, hh] = v[:, hh * LANES:(hh + 1) * LANES].T.astype(BF16)

    w_low = jnp.concatenate([win_ref[:, o3:o5].astype(BF16), wtail_ref[...].astype(BF16)], axis=1)
    low = _dot(hb, w_low)
    ckv = _rms(low[:, Q_RANK:Q_RANK + KV_RANK]) * kvn_ref[...]
    kv = _dot(ckv.astype(BF16), wukv_ref[...])
    kr = _rope_chunk(low[:, Q_RANK + KV_RANK:], cos, sin, lo_mask).astype(BF16)
    for hh in range(MLA_HEADS):
        km_ref[0, hh, :, 0:LANES] = kv[:, hh * LANES:(hh + 1) * LANES].astype(BF16)
        km_ref[0, hh, :, LANES:] = kr
        vm_ref[0, hh] = kv[:, MLA_WIDTH + hh * LANES:MLA_WIDTH + (hh + 1) * LANES].T.astype(BF16)

    q = _dot(hb, win_ref[:, 0:o1].astype(BF16))
    for hh in range(DA_HEADS):
        sl = slice(hh * LANES, (hh + 1) * LANES)
        qda_ref[0, hh] = (_rope_chunk(q[:, sl], cos, sin, lo_mask) * DA_SCALE).astype(BF16)
    cq = _rms(low[:, :Q_RANK]) * qn_ref[...]
    qm = _dot(cq.astype(BF16), wuq_ref[...])
    for hh in range(MLA_HEADS):
        base = hh * MLA_QK_PAD
        qm_ref[0, hh, :, 0:LANES] = (qm[:, base:base + LANES] * MLA_SCALE).astype(BF16)
        qr = _rope_chunk(qm[:, base + LANES:base + 2 * LANES], cos, sin, lo_mask)
        qm_ref[0, hh, :, LANES:] = (qr * MLA_SCALE).astype(BF16)


def _mixer_inputs(ctx, x, mod3, n1, cos_t, sin_t, w_in, w_tail, qn, kvn, wuq_b, wukv_b):
    B = x.shape[0]
    nt = T_ALL // TOK_TILE

    def lat(t):
        return jnp.maximum(t - 1, 0)

    def const(shape):
        return pl.BlockSpec(shape, lambda b, t: (0,) * len(shape), pipeline_mode=pl.Buffered(1))

    in_specs = [
        pl.BlockSpec((1, CTX_LEN, D_MODEL), lambda b, t: (b, 0, 0)),
        pl.BlockSpec((1, TOK_TILE, D_MODEL), lambda b, t: (b, lat(t), 0)),
        pl.BlockSpec((1, 1, D_MODEL), lambda b, t: (jnp.where(t == 0, CTX_ROW, b), 0, 0)),
        pl.BlockSpec((1, 1, D_MODEL), lambda b, t: (jnp.where(t == 0, CTX_ROW, b), 0, 1)),
        const((1, D_MODEL)),
        pl.BlockSpec((TOK_TILE, LANES), lambda b, t: (lat(t), 0)),
        pl.BlockSpec((TOK_TILE, LANES), lambda b, t: (lat(t), 0)),
        const(w_in.shape),
        const(w_tail.shape),
        const((1, Q_RANK)),
        const((1, KV_RANK)),
        const(wuq_b.shape),
        const(wukv_b.shape),
    ]
    out_specs = [
        pl.BlockSpec((1, DA_HEADS, TOK_TILE, HEAD_DIM), lambda b, t: (b, 0, lat(t), 0)),
        pl.BlockSpec((1, DA_HEADS, TOK_TILE, HEAD_DIM), lambda b, t: (b, 0, t, 0)),
        pl.BlockSpec((1, DA_HEADS, HEAD_DIM, TOK_TILE), lambda b, t: (b, 0, 0, t)),
        pl.BlockSpec((1, MLA_HEADS, TOK_TILE, MLA_QK_PAD), lambda b, t: (b, 0, lat(t), 0)),
        pl.BlockSpec((1, MLA_HEADS, TOK_TILE, MLA_QK_PAD), lambda b, t: (b, 0, t, 0)),
        pl.BlockSpec((1, MLA_HEADS, MLA_V, TOK_TILE), lambda b, t: (b, 0, 0, t)),
    ]
    out_shape = [
        jax.ShapeDtypeStruct((B, DA_HEADS, SEQ, HEAD_DIM), BF16),
        jax.ShapeDtypeStruct((B, DA_HEADS, T_ALL, HEAD_DIM), BF16),
        jax.ShapeDtypeStruct((B, DA_HEADS, HEAD_DIM, T_ALL), BF16),
        jax.ShapeDtypeStruct((B, MLA_HEADS, SEQ, MLA_QK_PAD), BF16),
        jax.ShapeDtypeStruct((B, MLA_HEADS, T_ALL, MLA_QK_PAD), BF16),
        jax.ShapeDtypeStruct((B, MLA_HEADS, MLA_V, T_ALL), BF16),
    ]
    return pl.pallas_call(
        _mixer_kernel,
        grid=(B, nt),
        in_specs=in_specs,
        out_specs=out_specs,
        out_shape=out_shape,
        compiler_params=_cparams(2),
        name="mixer_in",
    )(ctx, x, mod3, mod3, n1, cos_t, sin_t, w_in, w_tail, qn, kvn, wuq_b, wukv_b)


VX_ROWS = HEAD_DIM + 16


def _fill_values_with_ones(vt_ref, vx_ref):
    vx_ref[0:HEAD_DIM, :] = vt_ref[0, 0]
    first_row = lax.broadcasted_iota(jnp.int32, (VX_ROWS - HEAD_DIM, T_ALL), 0) == 0
    vx_ref[HEAD_DIM:, :] = jnp.where(first_row, 1.0, 0.0).astype(BF16)


KEY_SPLIT = 1280


def _scores_t(k_ref, q):
    return (_dot_nt(k_ref[0, 0, 0:KEY_SPLIT, :], q), _dot_nt(k_ref[0, 0, KEY_SPLIT:, :], q))


def _softmax_weighted_t(s_t, vx_ref):
    m = jnp.maximum(jnp.max(s_t[0], axis=0, keepdims=True), jnp.max(s_t[1], axis=0, keepdims=True))
    p0 = jnp.exp2(s_t[0] - m).astype(BF16)
    p1 = jnp.exp2(s_t[1] - m).astype(BF16)
    ox = _dot(vx_ref[:, 0:KEY_SPLIT], p0) + _dot(vx_ref[:, KEY_SPLIT:], p1)
    return ox[0:HEAD_DIM, :] * (1.0 / ox[HEAD_DIM:HEAD_DIM + 1, :])


def _pipelined_tiles(scores, finish):
    n_tiles = SEQ // ATT_TQ
    s = scores(0)
    for n in range(n_tiles):
        s_next = scores(n + 1) if n + 1 < n_tiles else None
        finish(n, s)
        s = s_next


def _diff_attn_kernel(q_ref, k_ref, vt_ref, lq1_ref, lk1_ref, lq2_ref, lk2_ref, sub_ref, o_ref,
                      vx_ref):
    lam = (jnp.exp(jnp.sum(lq1_ref[...] * lk1_ref[...], axis=-1, keepdims=True))
           - jnp.exp(jnp.sum(lq2_ref[...] * lk2_ref[...], axis=-1, keepdims=True))
           + LAMBDA_INIT)
    lane = lax.broadcasted_iota(jnp.int32, (ATT_TQ, HEAD_DIM), 1)
    _fill_values_with_ones(vt_ref, vx_ref)

    def scores(n):
        q = q_ref[0, 0, n * ATT_TQ:(n + 1) * ATT_TQ, :]
        zero = jnp.zeros_like(q)
        q1 = jnp.where(lane < DA_HALF, q, zero)
        q2 = jnp.where(lane >= DA_HALF, q, zero)
        return _scores_t(k_ref, q1), _scores_t(k_ref, q2)

    def finish(n, s):
        o_t = _softmax_weighted_t(s[0], vx_ref) - lam * _softmax_weighted_t(s[1], vx_ref)
        o = _rms(o_t.T) * sub_ref[...] * (1.0 - LAMBDA_INIT)
        o_ref[0, n * ATT_TQ:(n + 1) * ATT_TQ, :] = o.astype(BF16)

    _pipelined_tiles(scores, finish)


def _diff_attn(q, k, vt, lq1, lk1, lq2, lk2, subln):
    B = q.shape[0]
    small = lambda n: pl.BlockSpec((1, n), lambda b, h: (0, 0))
    return pl.pallas_call(
        _diff_attn_kernel,
        grid=(B, DA_HEADS),
        in_specs=[pl.BlockSpec((1, 1, SEQ, HEAD_DIM), lambda b, h: (b, h, 0, 0)),
                  pl.BlockSpec((1, 1, T_ALL, HEAD_DIM), lambda b, h: (b, h, 0, 0)),
                  pl.BlockSpec((1, 1, HEAD_DIM, T_ALL), lambda b, h: (b, h, 0, 0)),
                  small(DA_HALF), small(DA_HALF), small(DA_HALF), small(DA_HALF),
                  small(HEAD_DIM)],
        out_specs=pl.BlockSpec((1, SEQ, HEAD_DIM), lambda b, h: (b, 0, h)),
        out_shape=jax.ShapeDtypeStruct((B, SEQ, DA_WIDTH), BF16),
        scratch_shapes=[pltpu.VMEM((VX_ROWS, T_ALL), BF16)],
        compiler_params=_cparams(2),
        name="diff_attn",
    )(q, k, vt, lq1, lk1, lq2, lk2, subln)


def _mla_attn_kernel(q_ref, k_ref, vt_ref, o_ref, vx_ref):
    _fill_values_with_ones(vt_ref, vx_ref)

    def scores(n):
        return _scores_t(k_ref, q_ref[0, 0, n * ATT_TQ:(n + 1) * ATT_TQ, :])

    def finish(n, s):
        o_ref[0, n * ATT_TQ:(n + 1) * ATT_TQ, :] = _softmax_weighted_t(s, vx_ref).T.astype(BF16)

    _pipelined_tiles(scores, finish)


def _mla_attn(q, k, vt):
    B = q.shape[0]
    assert MLA_V == HEAD_DIM
    return pl.pallas_call(
        _mla_attn_kernel,
        grid=(B, MLA_HEADS),
        in_specs=[pl.BlockSpec((1, 1, SEQ, MLA_QK_PAD), lambda b, h: (b, h, 0, 0)),
                  pl.BlockSpec((1, 1, T_ALL, MLA_QK_PAD), lambda b, h: (b, h, 0, 0)),
                  pl.BlockSpec((1, 1, MLA_V, T_ALL), lambda b, h: (b, h, 0, 0))],
        out_specs=pl.BlockSpec((1, SEQ, MLA_V), lambda b, h: (b, 0, h)),
        out_shape=jax.ShapeDtypeStruct((B, SEQ, MLA_WIDTH), BF16),
        scratch_shapes=[pltpu.VMEM((VX_ROWS, T_ALL), BF16)],
        compiler_params=_cparams(2),
        name="mla_attn",
    )(q, k, vt)


def _out_proj_kernel(oda_ref, omla_ref, wo_ref, x_ref, g1_ref, sh2_ref, sc2_ref, n2_ref,
                     x1_ref, h2_ref):
    y = (_dot(oda_ref[0], wo_ref[0:DA_WIDTH, :].astype(BF16))
         + _dot(omla_ref[0], wo_ref[DA_WIDTH:, :].astype(BF16)))
    x1 = x_ref[0] + g1_ref[0] * y
    x1_ref[0] = x1
    h2 = _rms(x1) * n2_ref[...]
    h2_ref[0] = (h2 * (1.0 + sc2_ref[0]) + sh2_ref[0]).astype(BF16)


def _out_proj(o_da, o_mla, w_o, x, mod3, n2):
    B = x.shape[0]
    tile = lambda w: pl.BlockSpec((1, OUT_TM, w), lambda b, i: (b, i, 0))
    modrow = lambda col: pl.BlockSpec((1, 1, D_MODEL), lambda b, i: (b, 0, col))
    return pl.pallas_call(
        _out_proj_kernel,
        grid=(B, SEQ // OUT_TM),
        in_specs=[tile(DA_WIDTH), tile(MLA_WIDTH),
                  pl.BlockSpec(w_o.shape, lambda b, i: (0, 0), pipeline_mode=pl.Buffered(1)),
                  tile(D_MODEL), modrow(2), modrow(3), modrow(4),
                  pl.BlockSpec((1, D_MODEL), lambda b, i: (0, 0))],
        out_specs=[tile(D_MODEL), tile(D_MODEL)],
        out_shape=[jax.ShapeDtypeStruct((B, SEQ, D_MODEL), F32),
                   jax.ShapeDtypeStruct((B, SEQ, D_MODEL), BF16)],
        compiler_params=_cparams(2),
        name="out_proj",
    )(o_da, o_mla, w_o, x, mod3, mod3, mod3, n2)


def _ffn_kernel(h_ref, top_ref, bot_ref, wg_ref, wu_ref, cw_ref, cb_ref, wd_ref, x1_ref, g2_ref,
                fw_ref, o_ref, hs_ref):
    i = pl.program_id(1)
    j = pl.program_id(2)
    last_i = pl.num_programs(1) - 1
    last_j = pl.num_programs(2) - 1

    @pl.when(j == 0)
    def _():
        top = top_ref[0]
        bot = bot_ref[0]
        hs_ref[0:FFN_HALO, :] = jnp.where(i == 0, jnp.zeros_like(top), top)
        hs_ref[FFN_HALO:FFN_HALO + FFN_TM, :] = h_ref[0]
        hs_ref[FFN_HALO + FFN_TM:, :] = jnp.where(i == last_i, jnp.zeros_like(bot), bot)
        o_ref[0] = jnp.zeros((FFN_TM, D_MODEL), F32)

    @pl.when(j < FFN_X1_STEPS)
    def _():
        rows = pl.ds(pl.multiple_of(j * FFN_X1_ROWS, FFN_X1_ROWS), FFN_X1_ROWS)
        o_ref[0, rows, :] += x1_ref[0]

    lo = FFN_HALO
    g2 = g2_ref[0]
    for c in range(FFN_TF // FFN_SUB):
        cs = slice(c * FFN_SUB, (c + 1) * FFN_SUB)
        g = _dot(hs_ref[...], wg_ref[:, cs].astype(BF16))
        u = _dot(hs_ref[lo:lo + FFN_TM, :], wu_ref[:, cs].astype(BF16))
        g_prev = pltpu.roll(g, 1, 0)
        g_next = pltpu.roll(g, FFN_TM + 2 * FFN_HALO - 1, 0)
        gc = (g_prev[lo:lo + FFN_TM] * cw_ref[0:1, cs]
              + g[lo:lo + FFN_TM] * cw_ref[1:2, cs]
              + g_next[lo:lo + FFN_TM] * cw_ref[2:3, cs]
              + cb_ref[:, cs])
        act = (gc * (1.0 / (1.0 + jnp.exp(-gc))) * u).astype(BF16)
        for n in range(D_MODEL // FFN_NOUT):
            ns = slice(n * FFN_NOUT, (n + 1) * FFN_NOUT)
            o_ref[0, :, ns] += g2[:, ns] * _dot(act, wd_ref[cs, ns].astype(BF16))

    @pl.when(j == last_j)
    def _():
        o_ref[0] = _rms(o_ref[0]) * fw_ref[...]


def _conv_ffn_final(h2, w_up, conv_w, conv_b, w_down, x1, mod3, final_w):
    B = h2.shape[0]
    nf = D_FF // FFN_TF
    ni = SEQ // FFN_TM
    assert nf >= FFN_X1_STEPS
    halo_per_tile = FFN_TM // FFN_HALO
    n_halo = SEQ // FFN_HALO
    return pl.pallas_call(
        _ffn_kernel,
        grid=(B, ni, nf),
        in_specs=[
            pl.BlockSpec((1, FFN_TM, D_MODEL), lambda b, i, j: (b, i, 0),
                         pipeline_mode=pl.Buffered(1)),
            pl.BlockSpec((1, FFN_HALO, D_MODEL),
                         lambda b, i, j: (b, jnp.maximum(i * halo_per_tile - 1, 0), 0)),
            pl.BlockSpec((1, FFN_HALO, D_MODEL),
                         lambda b, i, j: (b, jnp.minimum((i + 1) * halo_per_tile, n_halo - 1), 0)),
            pl.BlockSpec((D_MODEL, FFN_TF), lambda b, i, j: (0, j)),
            pl.BlockSpec((D_MODEL, FFN_TF), lambda b, i, j: (0, nf + j)),
            pl.BlockSpec((CONV_W, FFN_TF), lambda b, i, j: (0, j)),
            pl.BlockSpec((1, FFN_TF), lambda b, i, j: (0, j)),
            pl.BlockSpec((FFN_TF, D_MODEL), lambda b, i, j: (j, 0)),
            pl.BlockSpec((1, FFN_X1_ROWS, D_MODEL),
                         lambda b, i, j: (b, i * FFN_X1_STEPS + jnp.minimum(j, FFN_X1_STEPS - 1), 0)),
            pl.BlockSpec((1, 1, D_MODEL), lambda b, i, j: (b, 0, 5)),
            pl.BlockSpec((1, D_MODEL), lambda b, i, j: (0, 0)),
        ],
        out_specs=pl.BlockSpec((1, FFN_TM, D_MODEL), lambda b, i, j: (b, i, 0)),
        out_shape=jax.ShapeDtypeStruct((B, SEQ, D_MODEL), F32),
        scratch_shapes=[pltpu.VMEM((FFN_TM + 2 * FFN_HALO, D_MODEL), BF16)],
        compiler_params=_cparams(3),
        name="conv_ffn",
    )(h2, h2, h2, w_up, w_up, conv_w, conv_b, w_down, x1, mod3, final_w)


def _rope_tables():
    pos = jnp.arange(SEQ)
    row = (pos // GRID_W).astype(F32)
    col = (pos % GRID_W).astype(F32)
    nf = ROPE_DIM // 4
    inv = ROPE_BASE ** (-jnp.arange(nf, dtype=F32) / nf)
    ar = row[:, None] * inv
    ac = col[:, None] * inv
    cos = jnp.concatenate([jnp.cos(ar), jnp.cos(ar), jnp.cos(ac), jnp.cos(ac)], axis=-1)
    sin = jnp.concatenate([-jnp.sin(ar), jnp.sin(ar), -jnp.sin(ac), jnp.sin(ac)], axis=-1)
    return jnp.tile(cos, (1, LANES // ROPE_DIM)), jnp.tile(sin, (1, LANES // ROPE_DIM))


def kernel(x, c, ctx, c_ctx, w_ada, b_ada, norm1_w, w_in, q_norm_w, kv_norm_w, w_uq, w_ukv,
           lambda_q1, lambda_k1, lambda_q2, lambda_k2, subln_w, w_o, norm2_w, w_up,
           conv_w, conv_b, w_down, final_w):
    B = x.shape[0]
    assert B <= CTX_ROW and x.shape == (B, SEQ, D_MODEL) and ctx.shape == (B, CTX_LEN, D_MODEL)
    l = 0
    c8 = jnp.concatenate([c, jnp.zeros((CTX_ROW - B, D_MODEL), F32), c_ctx[None, :],
                          jnp.zeros((MOD_ROWS - CTX_ROW - 1, D_MODEL), F32)], axis=0)
    mod = _ada(c8, w_ada[l], b_ada[l][None, :])
    mod3 = mod.reshape(MOD_ROWS, 1, N_MOD * D_MODEL)

    w_tail = jnp.pad(w_in[l][:, 3 * DA_WIDTH + Q_RANK + KV_RANK:], ((0, 0), (0, LANES - MLA_ROPE)))
    wuq_b = jnp.pad(w_uq[l].reshape(Q_RANK, MLA_HEADS, MLA_QK),
                    ((0, 0), (0, 0), (0, MLA_QK_PAD - MLA_QK))
                    ).reshape(Q_RANK, MLA_HEADS * MLA_QK_PAD).astype(BF16)
    wukv3 = w_ukv[l].reshape(KV_RANK, MLA_HEADS, MLA_NOPE + MLA_V)
    wukv_b = jnp.concatenate([wukv3[:, :, :MLA_NOPE].reshape(KV_RANK, MLA_HEADS * MLA_NOPE),
                              wukv3[:, :, MLA_NOPE:].reshape(KV_RANK, MLA_WIDTH)],
                             axis=1).astype(BF16)
    cos_t, sin_t = _rope_tables()

    q_da, k_da, v_da, q_mla, k_mla, v_mla = _mixer_inputs(
        ctx, x, mod3, norm1_w[l][None, :], cos_t, sin_t, w_in[l], w_tail,
        q_norm_w[l][None, :], kv_norm_w[l][None, :], wuq_b, wukv_b)
    o_da = _diff_attn(q_da, k_da, v_da, lambda_q1[l][None, :], lambda_k1[l][None, :],
                      lambda_q2[l][None, :], lambda_k2[l][None, :], subln_w[l][None, :])
    o_mla = _mla_attn(q_mla, k_mla, v_mla)
    x1, h2 = _out_proj(o_da, o_mla, w_o[l], x, mod3, norm2_w[l][None, :])
    return _conv_ffn_final(h2, w_up[l], conv_w[l], conv_b[l][None, :],
                           w_down[l], x1, mod3, final_w[None, :])
```

```python
import functools
import math

import jax
import jax.numpy as jnp
from jax import lax
from jax.experimental import pallas as pl
from jax.experimental.pallas import tpu as pltpu

D_MODEL = 2048
SEQ = 2048
CTX_LEN = 256
GRID_W = 64
HEAD_DIM = 128
DA_HEADS = 8
DA_HALF = 64
MLA_HEADS = 8
MLA_NOPE = 128
MLA_ROPE = 64
MLA_V = 128
Q_RANK = 384
KV_RANK = 256
ROPE_DIM = 64
ROPE_BASE = 10000.0
D_FF = 5632
CONV_W = 3
N_MOD = 6
EPS = 1e-6
DA_WIDTH = DA_HEADS * HEAD_DIM
MLA_WIDTH = MLA_HEADS * MLA_V
MLA_QK = MLA_NOPE + MLA_ROPE
MLA_QK_PAD = 256
IN_WIDTH = 3 * DA_WIDTH + Q_RANK + KV_RANK + MLA_ROPE
LOG2E = math.log2(math.e)
DA_SCALE = LOG2E / math.sqrt(DA_HALF)
MLA_SCALE = LOG2E / math.sqrt(MLA_QK)
LAMBDA_INIT = 0.8 - 0.6 * math.exp(-0.3 * 0)
T_ALL = CTX_LEN + SEQ

LANES = 128
MOD_ROWS = 8
CTX_ROW = 4
VMEM_LIMIT = 60 * 1024 * 1024

TOK_TILE = 256
ADA_TN = 1024
ATT_TQ = 256
OUT_TM = 512
FFN_TM = 1024
FFN_TF = 512
FFN_SUB = 256
FFN_NOUT = 512
FFN_HALO = 16
FFN_X1_STEPS = 8
FFN_X1_ROWS = FFN_TM // FFN_X1_STEPS

F32 = jnp.float32
BF16 = jnp.bfloat16


def _dot(a, b):
    return jnp.dot(a, b, preferred_element_type=F32)


def _dot_nt(a, b):
    return lax.dot_general(a, b, (((1,), (1,)), ((), ())), preferred_element_type=F32)


def _rms(x):
    return x * lax.rsqrt(jnp.mean(x * x, axis=-1, keepdims=True) + EPS)


def _cparams(n_grid):
    return pltpu.CompilerParams(dimension_semantics=("arbitrary",) * n_grid,
                                vmem_limit_bytes=VMEM_LIMIT)


def _ada_kernel(c_ref, w_ref, b_ref, o_ref):
    c = c_ref[...]
    sc = c * (1.0 / (1.0 + jnp.exp(-c)))
    o_ref[...] = _dot(sc.astype(BF16), w_ref[...].astype(BF16)) + b_ref[...]


def _ada(c8, w_ada, b_ada):
    n = w_ada.shape[1]
    return pl.pallas_call(
        _ada_kernel,
        grid=(n // ADA_TN,),
        in_specs=[pl.BlockSpec((MOD_ROWS, D_MODEL), lambda j: (0, 0)),
                  pl.BlockSpec((D_MODEL, ADA_TN), lambda j: (0, j)),
                  pl.BlockSpec((1, ADA_TN), lambda j: (0, j))],
        out_specs=pl.BlockSpec((MOD_ROWS, ADA_TN), lambda j: (0, j)),
        out_shape=jax.ShapeDtypeStruct((MOD_ROWS, n), F32),
        compiler_params=_cparams(1),
        name="ada",
    )(c8, w_ada, b_ada)


def _rope_chunk(xc, cos, sin, lo_mask):
    up = pltpu.roll(xc, LANES - 16, 1)
    dn = pltpu.roll(xc, 16, 1)
    return xc * cos + jnp.where(lo_mask, up, dn) * sin


def _mixer_kernel(ctx_ref, x_ref, sh_ref, sc_ref, n1_ref, cos_ref, sin_ref, wt_ref,
                  qn_ref, kvn_ref, wuq_ref, wukv_ref,
                  qda_ref, kda_ref, vda_ref, qm_ref, km_ref, vm_ref):
    t = pl.program_id(1)
    is_ctx = t == 0
    xt = jnp.where(is_ctx, ctx_ref[0], x_ref[0])
    h = _rms(xt) * n1_ref[...]
    h = h * (1.0 + sc_ref[0]) + sh_ref[0]
    hb = h.astype(BF16)
    cos = jnp.where(is_ctx, 1.0, cos_ref[...])
    sin = jnp.where(is_ctx, 0.0, sin_ref[...])
    lane = lax.broadcasted_iota(jnp.int32, (TOK_TILE, LANES), 1)
    lo_mask = (lane % 32) < 16
    o1, o2, o3 = DA_WIDTH, 2 * DA_WIDTH, 3 * DA_WIDTH
    o4 = o3 + Q_RANK
    o5 = o4 + KV_RANK

    k = _dot_nt(hb, wt_ref[o1:o2, :].astype(BF16))
    for hh in range(DA_HEADS):
        sl = slice(hh * LANES, (hh + 1) * LANES)
        kda_ref[0, :, sl] = _rope_chunk(k[:, sl], cos, sin, lo_mask).astype(BF16)
    vda_ref[0] = _dot_nt(hb, wt_ref[o2:o3, :].astype(BF16)).astype(BF16)

    w_low = jnp.concatenate([wt_ref[o3:IN_WIDTH, :].astype(BF16),
                             jnp.zeros((LANES - MLA_ROPE, D_MODEL), BF16)], axis=0)
    low = _dot_nt(hb, w_low)
    ckv = _rms(low[:, Q_RANK:Q_RANK + KV_RANK]) * kvn_ref[...]
    kv = _dot(ckv.astype(BF16), wukv_ref[...])
    kr = _rope_chunk(low[:, Q_RANK + KV_RANK:], cos, sin, lo_mask).astype(BF16)
    for hh in range(MLA_HEADS):
        base = hh * MLA_QK_PAD
        km_ref[0, :, base:base + LANES] = kv[:, hh * LANES:(hh + 1) * LANES].astype(BF16)
        km_ref[0, :, base + LANES:base + 2 * LANES] = kr
    vm_ref[0] = kv[:, MLA_WIDTH:].astype(BF16)

    q = _dot_nt(hb, wt_ref[0:o1, :].astype(BF16))
    for hh in range(DA_HEADS):
        sl = slice(hh * LANES, (hh + 1) * LANES)
        qda_ref[0, :, sl] = (_rope_chunk(q[:, sl], cos, sin, lo_mask) * DA_SCALE).astype(BF16)
    cq = _rms(low[:, :Q_RANK]) * qn_ref[...]
    qm = _dot(cq.astype(BF16), wuq_ref[...])
    for hh in range(MLA_HEADS):
        base = hh * MLA_QK_PAD
        qm_ref[0, :, base:base + LANES] = (qm[:, base:base + LANES] * MLA_SCALE).astype(BF16)
        qr = _rope_chunk(qm[:, base + LANES:base + 2 * LANES], cos, sin, lo_mask)
        qm_ref[0, :, base + LANES:base + 2 * LANES] = (qr * MLA_SCALE).astype(BF16)


def _mixer_inputs(ctx, x, mod3, n1, cos_t, sin_t, w_in_t, qn, kvn, wuq_b, wukv_b):
    B = x.shape[0]
    nt = T_ALL // TOK_TILE

    def lat(t):
        return jnp.maximum(t - 1, 0)

    def const(shape):
        return pl.BlockSpec(shape, lambda b, t: (0,) * len(shape), pipeline_mode=pl.Buffered(1))

    in_specs = [
        pl.BlockSpec((1, CTX_LEN, D_MODEL), lambda b, t: (b, 0, 0)),
        pl.BlockSpec((1, TOK_TILE, D_MODEL), lambda b, t: (b, lat(t), 0)),
        pl.BlockSpec((1, 1, D_MODEL), lambda b, t: (jnp.where(t == 0, CTX_ROW, b), 0, 0)),
        pl.BlockSpec((1, 1, D_MODEL), lambda b, t: (jnp.where(t == 0, CTX_ROW, b), 0, 1)),
        const((1, D_MODEL)),
        pl.BlockSpec((TOK_TILE, LANES), lambda b, t: (lat(t), 0)),
        pl.BlockSpec((TOK_TILE, LANES), lambda b, t: (lat(t), 0)),
        const(w_in_t.shape),
        const((1, Q_RANK)),
        const((1, KV_RANK)),
        const(wuq_b.shape),
        const(wukv_b.shape),
    ]
    out_specs = [
        pl.BlockSpec((1, TOK_TILE, DA_WIDTH), lambda b, t: (b, lat(t), 0)),
        pl.BlockSpec((1, TOK_TILE, DA_WIDTH), lambda b, t: (b, t, 0)),
        pl.BlockSpec((1, TOK_TILE, DA_WIDTH), lambda b, t: (b, t, 0)),
        pl.BlockSpec((1, TOK_TILE, MLA_HEADS * MLA_QK_PAD), lambda b, t: (b, lat(t), 0)),
        pl.BlockSpec((1, TOK_TILE, MLA_HEADS * MLA_QK_PAD), lambda b, t: (b, t, 0)),
        pl.BlockSpec((1, TOK_TILE, MLA_WIDTH), lambda b, t: (b, t, 0)),
    ]
    out_shape = [
        jax.ShapeDtypeStruct((B, SEQ, DA_WIDTH), BF16),
        jax.ShapeDtypeStruct((B, T_ALL, DA_WIDTH), BF16),
        jax.ShapeDtypeStruct((B, T_ALL, DA_WIDTH), BF16),
        jax.ShapeDtypeStruct((B, SEQ, MLA_HEADS * MLA_QK_PAD), BF16),
        jax.ShapeDtypeStruct((B, T_ALL, MLA_HEADS * MLA_QK_PAD), BF16),
        jax.ShapeDtypeStruct((B, T_ALL, MLA_WIDTH), BF16),
    ]
    return pl.pallas_call(
        _mixer_kernel,
        grid=(B, nt),
        in_specs=in_specs,
        out_specs=out_specs,
        out_shape=out_shape,
        compiler_params=_cparams(2),
        name="mixer_in",
    )(ctx, x, mod3, mod3, n1, cos_t, sin_t, w_in_t, qn, kvn, wuq_b, wukv_b)


def _fill_values_with_ones(v_ref, vx_ref):
    width = v_ref.shape[-1]
    vx_ref[:, 0:width] = v_ref[0]
    ones_lane = lax.broadcasted_iota(jnp.int32, (T_ALL, width), 1) == 0
    vx_ref[:, width:] = jnp.where(ones_lane, 1.0, 0.0).astype(BF16)


def _softmax_weighted(s, vx_ref):
    width = vx_ref.shape[-1] // 2
    p = jnp.exp2(s - jnp.max(s, axis=-1, keepdims=True))
    ox = _dot(p.astype(BF16), vx_ref[...])
    return ox[:, 0:width] * (1.0 / ox[:, width:width + 1])


def _pipelined_tiles(scores, finish):
    n_tiles = SEQ // ATT_TQ
    s = scores(0)
    for n in range(n_tiles):
        s_next = scores(n + 1) if n + 1 < n_tiles else None
        finish(n, s)
        s = s_next


def _diff_attn_kernel(q_ref, k_ref, v_ref, lq1_ref, lk1_ref, lq2_ref, lk2_ref, sub_ref, o_ref,
                      vx_ref):
    lam = (jnp.exp(jnp.sum(lq1_ref[...] * lk1_ref[...], axis=-1, keepdims=True))
           - jnp.exp(jnp.sum(lq2_ref[...] * lk2_ref[...], axis=-1, keepdims=True))
           + LAMBDA_INIT)
    lane = lax.broadcasted_iota(jnp.int32, (ATT_TQ, HEAD_DIM), 1)
    _fill_values_with_ones(v_ref, vx_ref)

    def scores(n):
        q = q_ref[0, n * ATT_TQ:(n + 1) * ATT_TQ, :]
        zero = jnp.zeros_like(q)
        q1 = jnp.where(lane < DA_HALF, q, zero)
        q2 = jnp.where(lane >= DA_HALF, q, zero)
        return _dot_nt(q1, k_ref[0]), _dot_nt(q2, k_ref[0])

    def finish(n, s):
        o = _softmax_weighted(s[0], vx_ref) - lam * _softmax_weighted(s[1], vx_ref)
        o = _rms(o) * sub_ref[...] * (1.0 - LAMBDA_INIT)
        o_ref[0, n * ATT_TQ:(n + 1) * ATT_TQ, :] = o.astype(BF16)

    _pipelined_tiles(scores, finish)


def _diff_attn(q, k, v, lq1, lk1, lq2, lk2, subln):
    B = q.shape[0]
    small = lambda n: pl.BlockSpec((1, n), lambda b, h: (0, 0))
    return pl.pallas_call(
        _diff_attn_kernel,
        grid=(B, DA_HEADS),
        in_specs=[pl.BlockSpec((1, SEQ, HEAD_DIM), lambda b, h: (b, 0, h)),
                  pl.BlockSpec((1, T_ALL, HEAD_DIM), lambda b, h: (b, 0, h)),
                  pl.BlockSpec((1, T_ALL, HEAD_DIM), lambda b, h: (b, 0, h)),
                  small(DA_HALF), small(DA_HALF), small(DA_HALF), small(DA_HALF),
                  small(HEAD_DIM)],
        out_specs=pl.BlockSpec((1, SEQ, HEAD_DIM), lambda b, h: (b, 0, h)),
        out_shape=jax.ShapeDtypeStruct((B, SEQ, DA_WIDTH), BF16),
        scratch_shapes=[pltpu.VMEM((T_ALL, 2 * HEAD_DIM), BF16)],
        compiler_params=_cparams(2),
        name="diff_attn",
    )(q, k, v, lq1, lk1, lq2, lk2, subln)


def _mla_attn_kernel(q_ref, k_ref, v_ref, o_ref, vx_ref):
    _fill_values_with_ones(v_ref, vx_ref)

    def scores(n):
        return _dot_nt(q_ref[0, n * ATT_TQ:(n + 1) * ATT_TQ, :], k_ref[0])

    def finish(n, s):
        o_ref[0, n * ATT_TQ:(n + 1) * ATT_TQ, :] = _softmax_weighted(s, vx_ref).astype(BF16)

    _pipelined_tiles(scores, finish)


def _mla_attn(q, k, v):
    B = q.shape[0]
    return pl.pallas_call(
        _mla_attn_kernel,
        grid=(B, MLA_HEADS),
        in_specs=[pl.BlockSpec((1, SEQ, MLA_QK_PAD), lambda b, h: (b, 0, h)),
                  pl.BlockSpec((1, T_ALL, MLA_QK_PAD), lambda b, h: (b, 0, h)),
                  pl.BlockSpec((1, T_ALL, MLA_V), lambda b, h: (b, 0, h))],
        out_specs=pl.BlockSpec((1, SEQ, MLA_V), lambda b, h: (b, 0, h)),
        out_shape=jax.ShapeDtypeStruct((B, SEQ, MLA_WIDTH), BF16),
        scratch_shapes=[pltpu.VMEM((T_ALL, 2 * MLA_V), BF16)],
        compiler_params=_cparams(2),
        name="mla_attn",
    )(q, k, v)


def _out_proj_kernel(oda_ref, omla_ref, wo_ref, x_ref, g1_ref, sh2_ref, sc2_ref, n2_ref,
                     x1_ref, h2_ref):
    y = (_dot(oda_ref[0], wo_ref[0:DA_WIDTH, :].astype(BF16))
         + _dot(omla_ref[0], wo_ref[DA_WIDTH:, :].astype(BF16)))
    x1 = x_ref[0] + g1_ref[0] * y
    x1_ref[0] = x1
    h2 = _rms(x1) * n2_ref[...]
    h2_ref[0] = (h2 * (1.0 + sc2_ref[0]) + sh2_ref[0]).astype(BF16)


def _out_proj(o_da, o_mla, w_o, x, mod3, n2):
    B = x.shape[0]
    tile = lambda w: pl.BlockSpec((1, OUT_TM, w), lambda b, i: (b, i, 0))
    modrow = lambda col: pl.BlockSpec((1, 1, D_MODEL), lambda b, i: (b, 0, col))
    return pl.pallas_call(
        _out_proj_kernel,
        grid=(B, SEQ // OUT_TM),
        in_specs=[tile(DA_WIDTH), tile(MLA_WIDTH),
                  pl.BlockSpec(w_o.shape, lambda b, i: (0, 0), pipeline_mode=pl.Buffered(1)),
                  tile(D_MODEL), modrow(2), modrow(3), modrow(4),
                  pl.BlockSpec((1, D_MODEL), lambda b, i: (0, 0))],
        out_specs=[tile(D_MODEL), tile(D_MODEL)],
        out_shape=[jax.ShapeDtypeStruct((B, SEQ, D_MODEL), F32),
                   jax.ShapeDtypeStruct((B, SEQ, D_MODEL), BF16)],
        compiler_params=_cparams(2),
        name="out_proj",
    )(o_da, o_mla, w_o, x, mod3, mod3, mod3, n2)


def _ffn_kernel(h_ref, top_ref, bot_ref, wg_ref, wu_ref, cw_ref, cb_ref, wd_ref, x1_ref, g2_ref,
                fw_ref, o_ref, hs_ref):
    i = pl.program_id(1)
    j = pl.program_id(2)
    last_i = pl.num_programs(1) - 1
    last_j = pl.num_programs(2) - 1

    @pl.when(j == 0)
    def _():
        top = top_ref[0]
        bot = bot_ref[0]
        hs_ref[0:FFN_HALO, :] = jnp.where(i == 0, jnp.zeros_like(top), top)
        hs_ref[FFN_HALO:FFN_HALO + FFN_TM, :] = h_ref[0]
        hs_ref[FFN_HALO + FFN_TM:, :] = jnp.where(i == last_i, jnp.zeros_like(bot), bot)
        o_ref[0] = jnp.zeros((FFN_TM, D_MODEL), F32)

    x1_rows = jnp.minimum(j, FFN_X1_STEPS - 1) * FFN_X1_ROWS
    rows = pl.ds(pl.multiple_of(x1_rows, FFN_X1_ROWS), FFN_X1_ROWS)
    o_ref[0, rows, :] += jnp.where(j < FFN_X1_STEPS, x1_ref[0], 0.0)

    lo = FFN_HALO
    g2 = g2_ref[0]
    for c in range(FFN_TF // FFN_SUB):
        cs = slice(c * FFN_SUB, (c + 1) * FFN_SUB)
        g = _dot(hs_ref[...], wg_ref[:, cs].astype(BF16))
        u = _dot(hs_ref[lo:lo + FFN_TM, :], wu_ref[:, cs].astype(BF16))
        g_prev = pltpu.roll(g, 1, 0)
        g_next = pltpu.roll(g, FFN_TM + 2 * FFN_HALO - 1, 0)
        gc = (g_prev[lo:lo + FFN_TM] * cw_ref[0:1, cs]
              + g[lo:lo + FFN_TM] * cw_ref[1:2, cs]
              + g_next[lo:lo + FFN_TM] * cw_ref[2:3, cs]
              + cb_ref[:, cs])
        act = (gc * (1.0 / (1.0 + jnp.exp(-gc))) * u).astype(BF16)
        for n in range(D_MODEL // FFN_NOUT):
            ns = slice(n * FFN_NOUT, (n + 1) * FFN_NOUT)
            o_ref[0, :, ns] += g2[:, ns] * _dot(act, wd_ref[cs, ns].astype(BF16))

    @pl.when(j == last_j)
    def _():
        o_ref[0] = _rms(o_ref[0]) * fw_ref[...]


def _conv_ffn_final(h2, w_up, conv_w, conv_b, w_down, x1, mod3, final_w):
    B = h2.shape[0]
    nf = D_FF // FFN_TF
    ni = SEQ // FFN_TM
    assert nf >= FFN_X1_STEPS
    halo_per_tile = FFN_TM // FFN_HALO
    n_halo = SEQ // FFN_HALO
    return pl.pallas_call(
        _ffn_kernel,
        grid=(B, ni, nf),
        in_specs=[
            pl.BlockSpec((1, FFN_TM, D_MODEL), lambda b, i, j: (b, i, 0),
                         pipeline_mode=pl.Buffered(1)),
            pl.BlockSpec((1, FFN_HALO, D_MODEL),
                         lambda b, i, j: (b, jnp.maximum(i * halo_per_tile - 1, 0), 0)),
            pl.BlockSpec((1, FFN_HALO, D_MODEL),
                         lambda b, i, j: (b, jnp.minimum((i + 1) * halo_per_tile, n_halo - 1), 0)),
            pl.BlockSpec((D_MODEL, FFN_TF), lambda b, i, j: (0, j)),
            pl.BlockSpec((D_MODEL, FFN_TF), lambda b, i, j: (0, nf + j)),
            pl.BlockSpec((CONV_W, FFN_TF), lambda b, i, j: (0, j)),
            pl.BlockSpec((1, FFN_TF), lambda b, i, j: (0, j)),
            pl.BlockSpec((FFN_TF, D_MODEL), lambda b, i, j: (j, 0)),
            pl.BlockSpec((1, FFN_X1_ROWS, D_MODEL),
                         lambda b, i, j: (b, i * FFN_X1_STEPS + jnp.minimum(j, FFN_X1_STEPS - 1), 0)),
            pl.BlockSpec((1, 1, D_MODEL), lambda b, i, j: (b, 0, 5)),
            pl.BlockSpec((1, D_MODEL), lambda b, i, j: (0, 0)),
        ],
        out_specs=pl.BlockSpec((1, FFN_TM, D_MODEL), lambda b, i, j: (b, i, 0)),
        out_shape=jax.ShapeDtypeStruct((B, SEQ, D_MODEL), F32),
        scratch_shapes=[pltpu.VMEM((FFN_TM + 2 * FFN_HALO, D_MODEL), BF16)],
        compiler_params=_cparams(3),
        name="conv_ffn",
    )(h2, h2, h2, w_up, w_up, conv_w, conv_b, w_down, x1, mod3, final_w)


def _rope_tables():
    pos = jnp.arange(SEQ)
    row = (pos // GRID_W).astype(F32)
    col = (pos % GRID_W).astype(F32)
    nf = ROPE_DIM // 4
    inv = ROPE_BASE ** (-jnp.arange(nf, dtype=F32) / nf)
    ar = row[:, None] * inv
    ac = col[:, None] * inv
    cos = jnp.concatenate([jnp.cos(ar), jnp.cos(ar), jnp.cos(ac), jnp.cos(ac)], axis=-1)
    sin = jnp.concatenate([-jnp.sin(ar), jnp.sin(ar), -jnp.sin(ac), jnp.sin(ac)], axis=-1)
    return jnp.tile(cos, (1, LANES // ROPE_DIM)), jnp.tile(sin, (1, LANES // ROPE_DIM))


def kernel(x, c, ctx, c_ctx, w_ada, b_ada, norm1_w, w_in, q_norm_w, kv_norm_w, w_uq, w_ukv,
           lambda_q1, lambda_k1, lambda_q2, lambda_k2, subln_w, w_o, norm2_w, w_up,
           conv_w, conv_b, w_down, final_w):
    B = x.shape[0]
    assert B <= CTX_ROW and x.shape == (B, SEQ, D_MODEL) and ctx.shape == (B, CTX_LEN, D_MODEL)
    assert w_in.shape[1:] == (D_MODEL, IN_WIDTH)
    l = 0
    c8 = jnp.concatenate([c, jnp.zeros((CTX_ROW - B, D_MODEL), F32), c_ctx[None, :],
                          jnp.zeros((MOD_ROWS - CTX_ROW - 1, D_MODEL), F32)], axis=0)
    mod = _ada(c8, w_ada[l], b_ada[l][None, :])
    mod3 = mod.reshape(MOD_ROWS, 1, N_MOD * D_MODEL)

    w_in_t = jnp.swapaxes(w_in[l], 0, 1)
    wuq_b = jnp.pad(w_uq[l].reshape(Q_RANK, MLA_HEADS, MLA_QK),
                    ((0, 0), (0, 0), (0, MLA_QK_PAD - MLA_QK))
                    ).reshape(Q_RANK, MLA_HEADS * MLA_QK_PAD).astype(BF16)
    wukv3 = w_ukv[l].reshape(KV_RANK, MLA_HEADS, MLA_NOPE + MLA_V)
    wukv_b = jnp.concatenate([wukv3[:, :, :MLA_NOPE].reshape(KV_RANK, MLA_HEADS * MLA_NOPE),
                              wukv3[:, :, MLA_NOPE:].reshape(KV_RANK, MLA_WIDTH)],
                             axis=1).astype(BF16)
    cos_t, sin_t = _rope_tables()

    q_da, k_da, v_da, q_mla, k_mla, v_mla = _mixer_inputs(
        ctx, x, mod3, norm1_w[l][None, :], cos_t, sin_t, w_in_t,
        q_norm_w[l][None, :], kv_norm_w[l][None, :], wuq_b, wukv_b)
    o_da = _diff_attn(q_da, k_da, v_da, lambda_q1[l][None, :], lambda_k1[l][None, :],
                      lambda_q2[l][None, :], lambda_k2[l][None, :], subln_w[l][None, :])
    o_mla = _mla_attn(q_mla, k_mla, v_mla)
    x1, h2 = _out_proj(o_da, o_mla, w_o[l], x, mod3, norm2_w[l][None, :])
    return _conv_ffn_final(h2, w_up[l], conv_w[l], conv_b[l][None, :],
                           w_down[l], x1, mod3, final_w[None, :])
```

```python
import functools
import math

import jax
import jax.numpy as jnp
from jax import lax
from jax.experimental import pallas as pl
from jax.experimental.pallas import tpu as pltpu

D_MODEL = 2048
SEQ = 2048
CTX_LEN = 256
GRID_W = 64
HEAD_DIM = 128
DA_HEADS = 8
DA_HALF = 64
MLA_HEADS = 8
MLA_NOPE = 128
MLA_ROPE = 64
MLA_V = 128
Q_RANK = 384
KV_RANK = 256
ROPE_DIM = 64
ROPE_BASE = 10000.0
D_FF = 5632
CONV_W = 3
N_MOD = 6
EPS = 1e-6
DA_WIDTH = DA_HEADS * HEAD_DIM
MLA_WIDTH = MLA_HEADS * MLA_V
MLA_QK = MLA_NOPE + MLA_ROPE
MLA_QK_PAD = 256
IN_WIDTH = 3 * DA_WIDTH + Q_RANK + KV_RANK + MLA_ROPE
LOG2E = math.log2(math.e)
DA_SCALE = LOG2E / math.sqrt(DA_HALF)
MLA_SCALE = LOG2E / math.sqrt(MLA_QK)
LAMBDA_INIT = 0.8 - 0.6 * math.exp(-0.3 * 0)
T_ALL = CTX_LEN + SEQ

LANES = 128
MOD_ROWS = 8
CTX_ROW = 4
VMEM_LIMIT = 60 * 1024 * 1024

TOK_TILE = 256
ADA_TN = 1024
ATT_TQ = 256
ATT_HEADS = 2
OUT_TM = 512
FFN_TM = 1024
FFN_TF = 512
FFN_SUB = 256
FFN_NOUT = 512
FFN_HALO = 16
FFN_X1_STEPS = 8
FFN_X1_ROWS = FFN_TM // FFN_X1_STEPS

F32 = jnp.float32
BF16 = jnp.bfloat16


def _dot(a, b):
    return jnp.dot(a, b, preferred_element_type=F32)


def _dot_nt(a, b):
    return lax.dot_general(a, b, (((1,), (1,)), ((), ())), preferred_element_type=F32)


def _rms(x):
    return x * lax.rsqrt(jnp.mean(x * x, axis=-1, keepdims=True) + EPS)


def _cparams(n_grid):
    return pltpu.CompilerParams(dimension_semantics=("arbitrary",) * n_grid,
                                vmem_limit_bytes=VMEM_LIMIT)


def _ada_kernel(c_ref, w_ref, b_ref, o_ref):
    c = c_ref[...]
    sc = c * (1.0 / (1.0 + jnp.exp(-c)))
    o_ref[...] = _dot(sc.astype(BF16), w_ref[...].astype(BF16)) + b_ref[...]


def _ada(c8, w_ada, b_ada):
    n = w_ada.shape[1]
    return pl.pallas_call(
        _ada_kernel,
        grid=(n // ADA_TN,),
        in_specs=[pl.BlockSpec((MOD_ROWS, D_MODEL), lambda j: (0, 0)),
                  pl.BlockSpec((D_MODEL, ADA_TN), lambda j: (0, j)),
                  pl.BlockSpec((1, ADA_TN), lambda j: (0, j))],
        out_specs=pl.BlockSpec((MOD_ROWS, ADA_TN), lambda j: (0, j)),
        out_shape=jax.ShapeDtypeStruct((MOD_ROWS, n), F32),
        compiler_params=_cparams(1),
        name="ada",
    )(c8, w_ada, b_ada)


def _rope_chunk(xc, cos, sin, lo_mask):
    up = pltpu.roll(xc, LANES - 16, 1)
    dn = pltpu.roll(xc, 16, 1)
    return xc * cos + jnp.where(lo_mask, up, dn) * sin


def _mixer_kernel(ctx_ref, x_ref, sh_ref, sc_ref, n1_ref, cos_ref, sin_ref, wt_ref,
                  qn_ref, kvn_ref, wuq_ref, wukv_ref,
                  qda_ref, kda_ref, vda_ref, qm_ref, km_ref, vm_ref):
    t = pl.program_id(1)
    is_ctx = t == 0
    xt = jnp.where(is_ctx, ctx_ref[0], x_ref[0])
    h = _rms(xt) * n1_ref[...]
    h = h * (1.0 + sc_ref[0]) + sh_ref[0]
    hb = h.astype(BF16)
    cos = jnp.where(is_ctx, 1.0, cos_ref[...])
    sin = jnp.where(is_ctx, 0.0, sin_ref[...])
    lane = lax.broadcasted_iota(jnp.int32, (TOK_TILE, LANES), 1)
    lo_mask = (lane % 32) < 16
    o1, o2, o3 = DA_WIDTH, 2 * DA_WIDTH, 3 * DA_WIDTH
    o4 = o3 + Q_RANK
    o5 = o4 + KV_RANK

    k = _dot_nt(hb, wt_ref[o1:o2, :].astype(BF16))
    for hh in range(DA_HEADS):
        sl = slice(hh * LANES, (hh + 1) * LANES)
        kda_ref[0, :, sl] = _rope_chunk(k[:, sl], cos, sin, lo_mask).astype(BF16)
    vda_ref[0] = _dot_nt(hb, wt_ref[o2:o3, :].astype(BF16)).astype(BF16)

    w_low = jnp.concatenate([wt_ref[o3:IN_WIDTH, :].astype(BF16),
                             jnp.zeros((LANES - MLA_ROPE, D_MODEL), BF16)], axis=0)
    low = _dot_nt(hb, w_low)
    ckv = _rms(low[:, Q_RANK:Q_RANK + KV_RANK]) * kvn_ref[...]
    kv = _dot(ckv.astype(BF16), wukv_ref[...])
    kr = _rope_chunk(low[:, Q_RANK + KV_RANK:], cos, sin, lo_mask).astype(BF16)
    for hh in range(MLA_HEADS):
        base = hh * MLA_QK_PAD
        km_ref[0, :, base:base + LANES] = kv[:, hh * LANES:(hh + 1) * LANES].astype(BF16)
        km_ref[0, :, base + LANES:base + 2 * LANES] = kr
    vm_ref[0] = kv[:, MLA_WIDTH:].astype(BF16)

    q = _dot_nt(hb, wt_ref[0:o1, :].astype(BF16))
    for hh in range(DA_HEADS):
        sl = slice(hh * LANES, (hh + 1) * LANES)
        qda_ref[0, :, sl] = (_rope_chunk(q[:, sl], cos, sin, lo_mask) * DA_SCALE).astype(BF16)
    cq = _rms(low[:, :Q_RANK]) * qn_ref[...]
    qm = _dot(cq.astype(BF16), wuq_ref[...])
    for hh in range(MLA_HEADS):
        base = hh * MLA_QK_PAD
        qm_ref[0, :, base:base + LANES] = (qm[:, base:base + LANES] * MLA_SCALE).astype(BF16)
        qr = _rope_chunk(qm[:, base + LANES:base + 2 * LANES], cos, sin, lo_mask)
        qm_ref[0, :, base + LANES:base + 2 * LANES] = (qr * MLA_SCALE).astype(BF16)


def _mixer_inputs(ctx, x, mod3, n1, cos_t, sin_t, w_in_t, qn, kvn, wuq_b, wukv_b):
    B = x.shape[0]
    nt = T_ALL // TOK_TILE

    def lat(t):
        return jnp.maximum(t - 1, 0)

    def const(shape):
        return pl.BlockSpec(shape, lambda b, t: (0,) * len(shape), pipeline_mode=pl.Buffered(1))

    in_specs = [
        pl.BlockSpec((1, CTX_LEN, D_MODEL), lambda b, t: (b, 0, 0)),
        pl.BlockSpec((1, TOK_TILE, D_MODEL), lambda b, t: (b, lat(t), 0)),
        pl.BlockSpec((1, 1, D_MODEL), lambda b, t: (jnp.where(t == 0, CTX_ROW, b), 0, 0)),
        pl.BlockSpec((1, 1, D_MODEL), lambda b, t: (jnp.where(t == 0, CTX_ROW, b), 0, 1)),
        const((1, D_MODEL)),
        pl.BlockSpec((TOK_TILE, LANES), lambda b, t: (lat(t), 0)),
        pl.BlockSpec((TOK_TILE, LANES), lambda b, t: (lat(t), 0)),
        const(w_in_t.shape),
        const((1, Q_RANK)),
        const((1, KV_RANK)),
        const(wuq_b.shape),
        const(wukv_b.shape),
    ]
    out_specs = [
        pl.BlockSpec((1, TOK_TILE, DA_WIDTH), lambda b, t: (b, lat(t), 0)),
        pl.BlockSpec((1, TOK_TILE, DA_WIDTH), lambda b, t: (b, t, 0)),
        pl.BlockSpec((1, TOK_TILE, DA_WIDTH), lambda b, t: (b, t, 0)),
        pl.BlockSpec((1, TOK_TILE, MLA_HEADS * MLA_QK_PAD), lambda b, t: (b, lat(t), 0)),
        pl.BlockSpec((1, TOK_TILE, MLA_HEADS * MLA_QK_PAD), lambda b, t: (b, t, 0)),
        pl.BlockSpec((1, TOK_TILE, MLA_WIDTH), lambda b, t: (b, t, 0)),
    ]
    out_shape = [
        jax.ShapeDtypeStruct((B, SEQ, DA_WIDTH), BF16),
        jax.ShapeDtypeStruct((B, T_ALL, DA_WIDTH), BF16),
        jax.ShapeDtypeStruct((B, T_ALL, DA_WIDTH), BF16),
        jax.ShapeDtypeStruct((B, SEQ, MLA_HEADS * MLA_QK_PAD), BF16),
        jax.ShapeDtypeStruct((B, T_ALL, MLA_HEADS * MLA_QK_PAD), BF16),
        jax.ShapeDtypeStruct((B, T_ALL, MLA_WIDTH), BF16),
    ]
    return pl.pallas_call(
        _mixer_kernel,
        grid=(B, nt),
        in_specs=in_specs,
        out_specs=out_specs,
        out_shape=out_shape,
        compiler_params=_cparams(2),
        name="mixer_in",
    )(ctx, x, mod3, mod3, n1, cos_t, sin_t, w_in_t, qn, kvn, wuq_b, wukv_b)


def _fill_values_with_ones(v_ref, vx_ref):
    width = vx_ref.shape[-1] // 2
    ones_lane = lax.broadcasted_iota(jnp.int32, (T_ALL, width), 1) == 0
    for hh in range(ATT_HEADS):
        vx_ref[hh, :, 0:width] = v_ref[0, :, hh * width:(hh + 1) * width]
        vx_ref[hh, :, width:] = jnp.where(ones_lane, 1.0, 0.0).astype(BF16)


def _softmax_weighted(s, vx):
    width = vx.shape[-1] // 2
    p = jnp.exp2(s - jnp.max(s, axis=-1, keepdims=True))
    ox = _dot(p.astype(BF16), vx)
    return ox[:, 0:width] * (1.0 / ox[:, width:width + 1])


def _pipelined_tiles(scores, finish):
    n_tiles = SEQ // ATT_TQ
    s = [scores(0, hh) for hh in range(ATT_HEADS)]
    for n in range(n_tiles):
        s_next = [scores(n + 1, hh) for hh in range(ATT_HEADS)] if n + 1 < n_tiles else None
        for hh in range(ATT_HEADS):
            finish(n, hh, s[hh])
        s = s_next


def _diff_attn_kernel(q_ref, k_ref, v_ref, lq1_ref, lk1_ref, lq2_ref, lk2_ref, sub_ref, o_ref,
                      vx_ref):
    lam = (jnp.exp(jnp.sum(lq1_ref[...] * lk1_ref[...], axis=-1, keepdims=True))
           - jnp.exp(jnp.sum(lq2_ref[...] * lk2_ref[...], axis=-1, keepdims=True))
           + LAMBDA_INIT)
    lane = lax.broadcasted_iota(jnp.int32, (ATT_TQ, HEAD_DIM), 1)
    _fill_values_with_ones(v_ref, vx_ref)

    def scores(n, hh):
        cols = slice(hh * HEAD_DIM, (hh + 1) * HEAD_DIM)
        q = q_ref[0, n * ATT_TQ:(n + 1) * ATT_TQ, cols]
        zero = jnp.zeros_like(q)
        q1 = jnp.where(lane < DA_HALF, q, zero)
        q2 = jnp.where(lane >= DA_HALF, q, zero)
        k = k_ref[0, :, cols]
        return _dot_nt(q1, k), _dot_nt(q2, k)

    def finish(n, hh, s):
        o = _softmax_weighted(s[0], vx_ref[hh]) - lam * _softmax_weighted(s[1], vx_ref[hh])
        o = _rms(o) * sub_ref[...] * (1.0 - LAMBDA_INIT)
        o_ref[0, n * ATT_TQ:(n + 1) * ATT_TQ, hh * HEAD_DIM:(hh + 1) * HEAD_DIM] = o.astype(BF16)

    _pipelined_tiles(scores, finish)


def _diff_attn(q, k, v, lq1, lk1, lq2, lk2, subln):
    B = q.shape[0]
    small = lambda n: pl.BlockSpec((1, n), lambda b, h: (0, 0))
    width = ATT_HEADS * HEAD_DIM
    return pl.pallas_call(
        _diff_attn_kernel,
        grid=(B, DA_HEADS // ATT_HEADS),
        in_specs=[pl.BlockSpec((1, SEQ, width), lambda b, h: (b, 0, h)),
                  pl.BlockSpec((1, T_ALL, width), lambda b, h: (b, 0, h)),
                  pl.BlockSpec((1, T_ALL, width), lambda b, h: (b, 0, h)),
                  small(DA_HALF), small(DA_HALF), small(DA_HALF), small(DA_HALF),
                  small(HEAD_DIM)],
        out_specs=pl.BlockSpec((1, SEQ, width), lambda b, h: (b, 0, h)),
        out_shape=jax.ShapeDtypeStruct((B, SEQ, DA_WIDTH), BF16),
        scratch_shapes=[pltpu.VMEM((ATT_HEADS, T_ALL, 2 * HEAD_DIM), BF16)],
        compiler_params=_cparams(2),
        name="diff_attn",
    )(q, k, v, lq1, lk1, lq2, lk2, subln)


def _mla_attn_kernel(q_ref, k_ref, v_ref, o_ref, vx_ref):
    _fill_values_with_ones(v_ref, vx_ref)

    def scores(n, hh):
        cols = slice(hh * MLA_QK_PAD, (hh + 1) * MLA_QK_PAD)
        return _dot_nt(q_ref[0, n * ATT_TQ:(n + 1) * ATT_TQ, cols], k_ref[0, :, cols])

    def finish(n, hh, s):
        o = _softmax_weighted(s, vx_ref[hh])
        o_ref[0, n * ATT_TQ:(n + 1) * ATT_TQ, hh * MLA_V:(hh + 1) * MLA_V] = o.astype(BF16)

    _pipelined_tiles(scores, finish)


def _mla_attn(q, k, v):
    B = q.shape[0]
    return pl.pallas_call(
        _mla_attn_kernel,
        grid=(B, MLA_HEADS // ATT_HEADS),
        in_specs=[pl.BlockSpec((1, SEQ, ATT_HEADS * MLA_QK_PAD), lambda b, h: (b, 0, h)),
                  pl.BlockSpec((1, T_ALL, ATT_HEADS * MLA_QK_PAD), lambda b, h: (b, 0, h)),
                  pl.BlockSpec((1, T_ALL, ATT_HEADS * MLA_V), lambda b, h: (b, 0, h))],
        out_specs=pl.BlockSpec((1, SEQ, ATT_HEADS * MLA_V), lambda b, h: (b, 0, h)),
        out_shape=jax.ShapeDtypeStruct((B, SEQ, MLA_WIDTH), BF16),
        scratch_shapes=[pltpu.VMEM((ATT_HEADS, T_ALL, 2 * MLA_V), BF16)],
        compiler_params=_cparams(2),
        name="mla_attn",
    )(q, k, v)


def _out_proj_kernel(oda_ref, omla_ref, wo_ref, x_ref, g1_ref, sh2_ref, sc2_ref, n2_ref,
                     x1_ref, h2_ref):
    y = (_dot(oda_ref[0], wo_ref[0:DA_WIDTH, :].astype(BF16))
         + _dot(omla_ref[0], wo_ref[DA_WIDTH:, :].astype(BF16)))
    x1 = x_ref[0] + g1_ref[0] * y
    x1_ref[0] = x1
    h2 = _rms(x1) * n2_ref[...]
    h2_ref[0] = (h2 * (1.0 + sc2_ref[0]) + sh2_ref[0]).astype(BF16)


def _out_proj(o_da, o_mla, w_o, x, mod3, n2):
    B = x.shape[0]
    tile = lambda w: pl.BlockSpec((1, OUT_TM, w), lambda b, i: (b, i, 0))
    modrow = lambda col: pl.BlockSpec((1, 1, D_MODEL), lambda b, i: (b, 0, col))
    return pl.pallas_call(
        _out_proj_kernel,
        grid=(B, SEQ // OUT_TM),
        in_specs=[tile(DA_WIDTH), tile(MLA_WIDTH),
                  pl.BlockSpec(w_o.shape, lambda b, i: (0, 0), pipeline_mode=pl.Buffered(1)),
                  tile(D_MODEL), modrow(2), modrow(3), modrow(4),
                  pl.BlockSpec((1, D_MODEL), lambda b, i: (0, 0))],
        out_specs=[tile(D_MODEL), tile(D_MODEL)],
        out_shape=[jax.ShapeDtypeStruct((B, SEQ, D_MODEL), F32),
                   jax.ShapeDtypeStruct((B, SEQ, D_MODEL), BF16)],
        compiler_params=_cparams(2),
        name="out_proj",
    )(o_da, o_mla, w_o, x, mod3, mod3, mod3, n2)


def _ffn_kernel(h_ref, top_ref, bot_ref, wg_ref, wu_ref, cw_ref, cb_ref, wd_ref, x1_ref, g2_ref,
                fw_ref, o_ref, hs_ref):
    i = pl.program_id(1)
    j = pl.program_id(2)
    last_i = pl.num_programs(1) - 1
    last_j = pl.num_programs(2) - 1

    @pl.when(j == 0)
    def _():
        top = top_ref[0]
        bot = bot_ref[0]
        hs_ref[0:FFN_HALO, :] = jnp.where(i == 0, jnp.zeros_like(top), top)
        hs_ref[FFN_HALO:FFN_HALO + FFN_TM, :] = h_ref[0]
        hs_ref[FFN_HALO + FFN_TM:, :] = jnp.where(i == last_i, jnp.zeros_like(bot), bot)
        o_ref[0] = jnp.zeros((FFN_TM, D_MODEL), F32)

    @pl.when(j < FFN_X1_STEPS)
    def _():
        rows = pl.ds(pl.multiple_of(j * FFN_X1_ROWS, FFN_X1_ROWS), FFN_X1_ROWS)
        o_ref[0, rows, :] += x1_ref[0]

    lo = FFN_HALO
    g2 = g2_ref[0]
    for c in range(FFN_TF // FFN_SUB):
        cs = slice(c * FFN_SUB, (c + 1) * FFN_SUB)
        g = _dot(hs_ref[...], wg_ref[:, cs].astype(BF16))
        u = _dot(hs_ref[lo:lo + FFN_TM, :], wu_ref[:, cs].astype(BF16))
        g_prev = pltpu.roll(g, 1, 0)
        g_next = pltpu.roll(g, FFN_TM + 2 * FFN_HALO - 1, 0)
        gc = (g_prev[lo:lo + FFN_TM] * cw_ref[0:1, cs]
              + g[lo:lo + FFN_TM] * cw_ref[1:2, cs]
              + g_next[lo:lo + FFN_TM] * cw_ref[2:3, cs]
              + cb_ref[:, cs])
        act = (gc * (1.0 / (1.0 + jnp.exp(-gc))) * u).astype(BF16)
        for n in range(D_MODEL // FFN_NOUT):
            ns = slice(n * FFN_NOUT, (n + 1) * FFN_NOUT)
            o_ref[0, :, ns] += g2[:, ns] * _dot(act, wd_ref[cs, ns].astype(BF16))

    @pl.when(j == last_j)
    def _():
        o_ref[0] = _rms(o_ref[0]) * fw_ref[...]


def _conv_ffn_final(h2, w_up, conv_w, conv_b, w_down, x1, mod3, final_w):
    B = h2.shape[0]
    nf = D_FF // FFN_TF
    ni = SEQ // FFN_TM
    assert nf >= FFN_X1_STEPS
    halo_per_tile = FFN_TM // FFN_HALO
    n_halo = SEQ // FFN_HALO
    return pl.pallas_call(
        _ffn_kernel,
        grid=(B, ni, nf),
        in_specs=[
            pl.BlockSpec((1, FFN_TM, D_MODEL), lambda b, i, j: (b, i, 0),
                         pipeline_mode=pl.Buffered(1)),
            pl.BlockSpec((1, FFN_HALO, D_MODEL),
                         lambda b, i, j: (b, jnp.maximum(i * halo_per_tile - 1, 0), 0)),
            pl.BlockSpec((1, FFN_HALO, D_MODEL),
                         lambda b, i, j: (b, jnp.minimum((i + 1) * halo_per_tile, n_halo - 1), 0)),
            pl.BlockSpec((D_MODEL, FFN_TF), lambda b, i, j: (0, j)),
            pl.BlockSpec((D_MODEL, FFN_TF), lambda b, i, j: (0, nf + j)),
            pl.BlockSpec((CONV_W, FFN_TF), lambda b, i, j: (0, j)),
            pl.BlockSpec((1, FFN_TF), lambda b, i, j: (0, j)),
            pl.BlockSpec((FFN_TF, D_MODEL), lambda b, i, j: (j, 0)),
            pl.BlockSpec((1, FFN_X1_ROWS, D_MODEL),
                         lambda b, i, j: (b, i * FFN_X1_STEPS + jnp.minimum(j, FFN_X1_STEPS - 1), 0)),
            pl.BlockSpec((1, 1, D_MODEL), lambda b, i, j: (b, 0, 5)),
            pl.BlockSpec((1, D_MODEL), lambda b, i, j: (0, 0)),
        ],
        out_specs=pl.BlockSpec((1, FFN_TM, D_MODEL), lambda b, i, j: (b, i, 0)),
        out_shape=jax.ShapeDtypeStruct((B, SEQ, D_MODEL), F32),
        scratch_shapes=[pltpu.VMEM((FFN_TM + 2 * FFN_HALO, D_MODEL), BF16)],
        compiler_params=_cparams(3),
        name="conv_ffn",
    )(h2, h2, h2, w_up, w_up, conv_w, conv_b, w_down, x1, mod3, final_w)


def _rope_tables():
    pos = jnp.arange(SEQ)
    row = (pos // GRID_W).astype(F32)
    col = (pos % GRID_W).astype(F32)
    nf = ROPE_DIM // 4
    inv = ROPE_BASE ** (-jnp.arange(nf, dtype=F32) / nf)
    ar = row[:, None] * inv
    ac = col[:, None] * inv
    cos = jnp.concatenate([jnp.cos(ar), jnp.cos(ar), jnp.cos(ac), jnp.cos(ac)], axis=-1)
    sin = jnp.concatenate([-jnp.sin(ar), jnp.sin(ar), -jnp.sin(ac), jnp.sin(ac)], axis=-1)
    return jnp.tile(cos, (1, LANES // ROPE_DIM)), jnp.tile(sin, (1, LANES // ROPE_DIM))


def kernel(x, c, ctx, c_ctx, w_ada, b_ada, norm1_w, w_in, q_norm_w, kv_norm_w, w_uq, w_ukv,
           lambda_q1, lambda_k1, lambda_q2, lambda_k2, subln_w, w_o, norm2_w, w_up,
           conv_w, conv_b, w_down, final_w):
    B = x.shape[0]
    assert B <= CTX_ROW and x.shape == (B, SEQ, D_MODEL) and ctx.shape == (B, CTX_LEN, D_MODEL)
    assert w_in.shape[1:] == (D_MODEL, IN_WIDTH)
    l = 0
    c8 = jnp.concatenate([c, jnp.zeros((CTX_ROW - B, D_MODEL), F32), c_ctx[None, :],
                          jnp.zeros((MOD_ROWS - CTX_ROW - 1, D_MODEL), F32)], axis=0)
    mod = _ada(c8, w_ada[l], b_ada[l][None, :])
    mod3 = mod.reshape(MOD_ROWS, 1, N_MOD * D_MODEL)

    w_in_t = jnp.swapaxes(w_in[l], 0, 1)
    wuq_b = jnp.pad(w_uq[l].reshape(Q_RANK, MLA_HEADS, MLA_QK),
                    ((0, 0), (0, 0), (0, MLA_QK_PAD - MLA_QK))
                    ).reshape(Q_RANK, MLA_HEADS * MLA_QK_PAD).astype(BF16)
    wukv3 = w_ukv[l].reshape(KV_RANK, MLA_HEADS, MLA_NOPE + MLA_V)
    wukv_b = jnp.concatenate([wukv3[:, :, :MLA_NOPE].reshape(KV_RANK, MLA_HEADS * MLA_NOPE),
                              wukv3[:, :, MLA_NOPE:].reshape(KV_RANK, MLA_WIDTH)],
                             axis=1).astype(BF16)
    cos_t, sin_t = _rope_tables()

    q_da, k_da, v_da, q_mla, k_mla, v_mla = _mixer_inputs(
        ctx, x, mod3, norm1_w[l][None, :], cos_t, sin_t, w_in_t,
        q_norm_w[l][None, :], kv_norm_w[l][None, :], wuq_b, wukv_b)
    o_da = _diff_attn(q_da, k_da, v_da, lambda_q1[l][None, :], lambda_k1[l][None, :],
                      lambda_q2[l][None, :], lambda_k2[l][None, :], subln_w[l][None, :])
    o_mla = _mla_attn(q_mla, k_mla, v_mla)
    x1, h2 = _out_proj(o_da, o_mla, w_o[l], x, mod3, norm2_w[l][None, :])
    return _conv_ffn_final(h2, w_up[l], conv_w[l], conv_b[l][None, :],
                           w_down[l], x1, mod3, final_w[None, :])
```

```python
import functools
import math

import jax
import jax.numpy as jnp
from jax import lax
from jax.experimental import pallas as pl
from jax.experimental.pallas import tpu as pltpu

D_MODEL = 2048
SEQ = 2048
CTX_LEN = 256
GRID_W = 64
HEAD_DIM = 128
DA_HEADS = 8
DA_HALF = 64
MLA_HEADS = 8
MLA_NOPE = 128
MLA_ROPE = 64
MLA_V = 128
Q_RANK = 384
KV_RANK = 256
ROPE_DIM = 64
ROPE_BASE = 10000.0
D_FF = 5632
CONV_W = 3
N_MOD = 6
EPS = 1e-6
DA_WIDTH = DA_HEADS * HEAD_DIM
MLA_WIDTH = MLA_HEADS * MLA_V
MLA_QK = MLA_NOPE + MLA_ROPE
MLA_QK_PAD = 256
IN_WIDTH = 3 * DA_WIDTH + Q_RANK + KV_RANK + MLA_ROPE
LOG2E = math.log2(math.e)
DA_SCALE = LOG2E / math.sqrt(DA_HALF)
MLA_SCALE = LOG2E / math.sqrt(MLA_QK)
LAMBDA_INIT = 0.8 - 0.6 * math.exp(-0.3 * 0)
T_ALL = CTX_LEN + SEQ

LANES = 128
MOD_ROWS = 8
CTX_ROW = 4
VMEM_LIMIT = 60 * 1024 * 1024

TOK_TILE = 256
ADA_TN = 1024
ATT_TQ = 256
DA_STEP_HEADS = 1
MLA_STEP_HEADS = 4
OUT_TM = 512
FFN_TM = 1024
FFN_TF = 512
FFN_SUB = 256
FFN_NOUT = 512
FFN_HALO = 16
FFN_X1_STEPS = 8
FFN_X1_ROWS = FFN_TM // FFN_X1_STEPS

F32 = jnp.float32
BF16 = jnp.bfloat16


def _dot(a, b):
    return jnp.dot(a, b, preferred_element_type=F32)


def _dot_nt(a, b):
    return lax.dot_general(a, b, (((1,), (1,)), ((), ())), preferred_element_type=F32)


def _rms(x):
    return x * lax.rsqrt(jnp.mean(x * x, axis=-1, keepdims=True) + EPS)


def _cparams(n_grid):
    return pltpu.CompilerParams(dimension_semantics=("arbitrary",) * n_grid,
                                vmem_limit_bytes=VMEM_LIMIT)


def _ada_kernel(c_ref, w_ref, b_ref, o_ref):
    c = c_ref[...]
    sc = c * (1.0 / (1.0 + jnp.exp(-c)))
    o_ref[...] = _dot(sc.astype(BF16), w_ref[...].astype(BF16)) + b_ref[...]


def _ada(c8, w_ada, b_ada):
    n = w_ada.shape[1]
    return pl.pallas_call(
        _ada_kernel,
        grid=(n // ADA_TN,),
        in_specs=[pl.BlockSpec((MOD_ROWS, D_MODEL), lambda j: (0, 0)),
                  pl.BlockSpec((D_MODEL, ADA_TN), lambda j: (0, j)),
                  pl.BlockSpec((1, ADA_TN), lambda j: (0, j))],
        out_specs=pl.BlockSpec((MOD_ROWS, ADA_TN), lambda j: (0, j)),
        out_shape=jax.ShapeDtypeStruct((MOD_ROWS, n), F32),
        compiler_params=_cparams(1),
        name="ada",
    )(c8, w_ada, b_ada)


def _rope_chunk(xc, cos, sin, lo_mask):
    up = pltpu.roll(xc, LANES - 16, 1)
    dn = pltpu.roll(xc, 16, 1)
    return xc * cos + jnp.where(lo_mask, up, dn) * sin


def _mixer_kernel(ctx_ref, x_ref, sh_ref, sc_ref, n1_ref, cos_ref, sin_ref, wt_ref,
                  qn_ref, kvn_ref, wuq_ref, wukv_ref,
                  qda_ref, kda_ref, vda_ref, qm_ref, km_ref, vm_ref):
    t = pl.program_id(1)
    is_ctx = t == 0
    xt = jnp.where(is_ctx, ctx_ref[0], x_ref[0])
    h = _rms(xt) * n1_ref[...]
    h = h * (1.0 + sc_ref[0]) + sh_ref[0]
    hb = h.astype(BF16)
    cos = jnp.where(is_ctx, 1.0, cos_ref[...])
    sin = jnp.where(is_ctx, 0.0, sin_ref[...])
    lane = lax.broadcasted_iota(jnp.int32, (TOK_TILE, LANES), 1)
    lo_mask = (lane % 32) < 16
    o1, o2, o3 = DA_WIDTH, 2 * DA_WIDTH, 3 * DA_WIDTH
    o4 = o3 + Q_RANK
    o5 = o4 + KV_RANK

    k = _dot_nt(hb, wt_ref[o1:o2, :].astype(BF16))
    for hh in range(DA_HEADS):
        sl = slice(hh * LANES, (hh + 1) * LANES)
        kda_ref[0, :, sl] = _rope_chunk(k[:, sl], cos, sin, lo_mask).astype(BF16)
    vda_ref[0] = _dot_nt(hb, wt_ref[o2:o3, :].astype(BF16)).astype(BF16)

    w_low = jnp.concatenate([wt_ref[o3:IN_WIDTH, :].astype(BF16),
                             jnp.zeros((LANES - MLA_ROPE, D_MODEL), BF16)], axis=0)
    low = _dot_nt(hb, w_low)
    ckv = _rms(low[:, Q_RANK:Q_RANK + KV_RANK]) * kvn_ref[...]
    kv = _dot(ckv.astype(BF16), wukv_ref[...])
    kr = _rope_chunk(low[:, Q_RANK + KV_RANK:], cos, sin, lo_mask).astype(BF16)
    for hh in range(MLA_HEADS):
        base = hh * MLA_QK_PAD
        km_ref[0, :, base:base + LANES] = kv[:, hh * LANES:(hh + 1) * LANES].astype(BF16)
        km_ref[0, :, base + LANES:base + 2 * LANES] = kr
    vm_ref[0] = kv[:, MLA_WIDTH:].astype(BF16)

    q = _dot_nt(hb, wt_ref[0:o1, :].astype(BF16))
    for hh in range(DA_HEADS):
        sl = slice(hh * LANES, (hh + 1) * LANES)
        qda_ref[0, :, sl] = (_rope_chunk(q[:, sl], cos, sin, lo_mask) * DA_SCALE).astype(BF16)
    cq = _rms(low[:, :Q_RANK]) * qn_ref[...]
    qm = _dot(cq.astype(BF16), wuq_ref[...])
    for hh in range(MLA_HEADS):
        base = hh * MLA_QK_PAD
        qm_ref[0, :, base:base + LANES] = (qm[:, base:base + LANES] * MLA_SCALE).astype(BF16)
        qr = _rope_chunk(qm[:, base + LANES:base + 2 * LANES], cos, sin, lo_mask)
        qm_ref[0, :, base + LANES:base + 2 * LANES] = (qr * MLA_SCALE).astype(BF16)


def _mixer_inputs(ctx, x, mod3, n1, cos_t, sin_t, w_in_t, qn, kvn, wuq_b, wukv_b):
    B = x.shape[0]
    nt = T_ALL // TOK_TILE

    def lat(t):
        return jnp.maximum(t - 1, 0)

    def const(shape):
        return pl.BlockSpec(shape, lambda b, t: (0,) * len(shape), pipeline_mode=pl.Buffered(1))

    in_specs = [
        pl.BlockSpec((1, CTX_LEN, D_MODEL), lambda b, t: (b, 0, 0)),
        pl.BlockSpec((1, TOK_TILE, D_MODEL), lambda b, t: (b, lat(t), 0)),
        pl.BlockSpec((1, 1, D_MODEL), lambda b, t: (jnp.where(t == 0, CTX_ROW, b), 0, 0)),
        pl.BlockSpec((1, 1, D_MODEL), lambda b, t: (jnp.where(t == 0, CTX_ROW, b), 0, 1)),
        const((1, D_MODEL)),
        pl.BlockSpec((TOK_TILE, LANES), lambda b, t: (lat(t), 0)),
        pl.BlockSpec((TOK_TILE, LANES), lambda b, t: (lat(t), 0)),
        const(w_in_t.shape),
        const((1, Q_RANK)),
        const((1, KV_RANK)),
        const(wuq_b.shape),
        const(wukv_b.shape),
    ]
    out_specs = [
        pl.BlockSpec((1, TOK_TILE, DA_WIDTH), lambda b, t: (b, lat(t), 0)),
        pl.BlockSpec((1, TOK_TILE, DA_WIDTH), lambda b, t: (b, t, 0)),
        pl.BlockSpec((1, TOK_TILE, DA_WIDTH), lambda b, t: (b, t, 0)),
        pl.BlockSpec((1, TOK_TILE, MLA_HEADS * MLA_QK_PAD), lambda b, t: (b, lat(t), 0)),
        pl.BlockSpec((1, TOK_TILE, MLA_HEADS * MLA_QK_PAD), lambda b, t: (b, t, 0)),
        pl.BlockSpec((1, TOK_TILE, MLA_WIDTH), lambda b, t: (b, t, 0)),
    ]
    out_shape = [
        jax.ShapeDtypeStruct((B, SEQ, DA_WIDTH), BF16),
        jax.ShapeDtypeStruct((B, T_ALL, DA_WIDTH), BF16),
        jax.ShapeDtypeStruct((B, T_ALL, DA_WIDTH), BF16),
        jax.ShapeDtypeStruct((B, SEQ, MLA_HEADS * MLA_QK_PAD), BF16),
        jax.ShapeDtypeStruct((B, T_ALL, MLA_HEADS * MLA_QK_PAD), BF16),
        jax.ShapeDtypeStruct((B, T_ALL, MLA_WIDTH), BF16),
    ]
    return pl.pallas_call(
        _mixer_kernel,
        grid=(B, nt),
        in_specs=in_specs,
        out_specs=out_specs,
        out_shape=out_shape,
        compiler_params=_cparams(2),
        name="mixer_in",
    )(ctx, x, mod3, mod3, n1, cos_t, sin_t, w_in_t, qn, kvn, wuq_b, wukv_b)


def _fill_values_with_ones(v_ref, vx_ref):
    width = vx_ref.shape[-1] // 2
    ones_lane = lax.broadcasted_iota(jnp.int32, (T_ALL, width), 1) == 0
    for hh in range(vx_ref.shape[0]):
        vx_ref[hh, :, 0:width] = v_ref[0, :, hh * width:(hh + 1) * width]
        vx_ref[hh, :, width:] = jnp.where(ones_lane, 1.0, 0.0).astype(BF16)


def _softmax_weighted(s, vx):
    width = vx.shape[-1] // 2
    p = jnp.exp2(s - jnp.max(s, axis=-1, keepdims=True))
    ox = _dot(p.astype(BF16), vx)
    return ox[:, 0:width] * (1.0 / ox[:, width:width + 1])


def _pipelined_tiles(scores, finish, n_heads):
    n_tiles = SEQ // ATT_TQ
    s = [scores(0, hh) for hh in range(n_heads)]
    for n in range(n_tiles):
        s_next = [scores(n + 1, hh) for hh in range(n_heads)] if n + 1 < n_tiles else None
        for hh in range(n_heads):
            finish(n, hh, s[hh])
        s = s_next


def _diff_attn_kernel(q_ref, k_ref, v_ref, lq1_ref, lk1_ref, lq2_ref, lk2_ref, sub_ref, o_ref,
                      vx_ref):
    lam = (jnp.exp(jnp.sum(lq1_ref[...] * lk1_ref[...], axis=-1, keepdims=True))
           - jnp.exp(jnp.sum(lq2_ref[...] * lk2_ref[...], axis=-1, keepdims=True))
           + LAMBDA_INIT)
    lane = lax.broadcasted_iota(jnp.int32, (ATT_TQ, HEAD_DIM), 1)
    _fill_values_with_ones(v_ref, vx_ref)

    def scores(n, hh):
        cols = slice(hh * HEAD_DIM, (hh + 1) * HEAD_DIM)
        q = q_ref[0, n * ATT_TQ:(n + 1) * ATT_TQ, cols]
        zero = jnp.zeros_like(q)
        q1 = jnp.where(lane < DA_HALF, q, zero)
        q2 = jnp.where(lane >= DA_HALF, q, zero)
        k = k_ref[0, :, cols]
        return _dot_nt(q1, k), _dot_nt(q2, k)

    def finish(n, hh, s):
        o = _softmax_weighted(s[0], vx_ref[hh]) - lam * _softmax_weighted(s[1], vx_ref[hh])
        o = _rms(o) * sub_ref[...] * (1.0 - LAMBDA_INIT)
        o_ref[0, n * ATT_TQ:(n + 1) * ATT_TQ, hh * HEAD_DIM:(hh + 1) * HEAD_DIM] = o.astype(BF16)

    _pipelined_tiles(scores, finish, DA_STEP_HEADS)


def _diff_attn(q, k, v, lq1, lk1, lq2, lk2, subln):
    B = q.shape[0]
    small = lambda n: pl.BlockSpec((1, n), lambda b, h: (0, 0))
    width = DA_STEP_HEADS * HEAD_DIM
    return pl.pallas_call(
        _diff_attn_kernel,
        grid=(B, DA_HEADS // DA_STEP_HEADS),
        in_specs=[pl.BlockSpec((1, SEQ, width), lambda b, h: (b, 0, h)),
                  pl.BlockSpec((1, T_ALL, width), lambda b, h: (b, 0, h)),
                  pl.BlockSpec((1, T_ALL, width), lambda b, h: (b, 0, h)),
                  small(DA_HALF), small(DA_HALF), small(DA_HALF), small(DA_HALF),
                  small(HEAD_DIM)],
        out_specs=pl.BlockSpec((1, SEQ, width), lambda b, h: (b, 0, h)),
        out_shape=jax.ShapeDtypeStruct((B, SEQ, DA_WIDTH), BF16),
        scratch_shapes=[pltpu.VMEM((DA_STEP_HEADS, T_ALL, 2 * HEAD_DIM), BF16)],
        compiler_params=_cparams(2),
        name="diff_attn",
    )(q, k, v, lq1, lk1, lq2, lk2, subln)


def _mla_attn_kernel(q_ref, k_ref, v_ref, o_ref, vx_ref):
    _fill_values_with_ones(v_ref, vx_ref)

    def scores(n, hh):
        cols = slice(hh * MLA_QK_PAD, (hh + 1) * MLA_QK_PAD)
        return _dot_nt(q_ref[0, n * ATT_TQ:(n + 1) * ATT_TQ, cols], k_ref[0, :, cols])

    def finish(n, hh, s):
        o = _softmax_weighted(s, vx_ref[hh])
        o_ref[0, n * ATT_TQ:(n + 1) * ATT_TQ, hh * MLA_V:(hh + 1) * MLA_V] = o.astype(BF16)

    _pipelined_tiles(scores, finish, MLA_STEP_HEADS)


def _mla_attn(q, k, v):
    B = q.shape[0]
    return pl.pallas_call(
        _mla_attn_kernel,
        grid=(B, MLA_HEADS // MLA_STEP_HEADS),
        in_specs=[pl.BlockSpec((1, SEQ, MLA_STEP_HEADS * MLA_QK_PAD), lambda b, h: (b, 0, h)),
                  pl.BlockSpec((1, T_ALL, MLA_STEP_HEADS * MLA_QK_PAD), lambda b, h: (b, 0, h)),
                  pl.BlockSpec((1, T_ALL, MLA_STEP_HEADS * MLA_V), lambda b, h: (b, 0, h))],
        out_specs=pl.BlockSpec((1, SEQ, MLA_STEP_HEADS * MLA_V), lambda b, h: (b, 0, h)),
        out_shape=jax.ShapeDtypeStruct((B, SEQ, MLA_WIDTH), BF16),
        scratch_shapes=[pltpu.VMEM((MLA_STEP_HEADS, T_ALL, 2 * MLA_V), BF16)],
        compiler_params=_cparams(2),
        name="mla_attn",
    )(q, k, v)


def _out_proj_kernel(oda_ref, omla_ref, wo_ref, x_ref, g1_ref, sh2_ref, sc2_ref, n2_ref,
                     x1_ref, h2_ref):
    y = (_dot(oda_ref[0], wo_ref[0:DA_WIDTH, :].astype(BF16))
         + _dot(omla_ref[0], wo_ref[DA_WIDTH:, :].astype(BF16)))
    x1 = x_ref[0] + g1_ref[0] * y
    x1_ref[0] = x1
    h2 = _rms(x1) * n2_ref[...]
    h2_ref[0] = (h2 * (1.0 + sc2_ref[0]) + sh2_ref[0]).astype(BF16)


def _out_proj(o_da, o_mla, w_o, x, mod3, n2):
    B = x.shape[0]
    tile = lambda w: pl.BlockSpec((1, OUT_TM, w), lambda b, i: (b, i, 0))
    modrow = lambda col: pl.BlockSpec((1, 1, D_MODEL), lambda b, i: (b, 0, col))
    return pl.pallas_call(
        _out_proj_kernel,
        grid=(B, SEQ // OUT_TM),
        in_specs=[tile(DA_WIDTH), tile(MLA_WIDTH),
                  pl.BlockSpec(w_o.shape, lambda b, i: (0, 0), pipeline_mode=pl.Buffered(1)),
                  tile(D_MODEL), modrow(2), modrow(3), modrow(4),
                  pl.BlockSpec((1, D_MODEL), lambda b, i: (0, 0))],
        out_specs=[tile(D_MODEL), tile(D_MODEL)],
        out_shape=[jax.ShapeDtypeStruct((B, SEQ, D_MODEL), F32),
                   jax.ShapeDtypeStruct((B, SEQ, D_MODEL), BF16)],
        compiler_params=_cparams(2),
        name="out_proj",
    )(o_da, o_mla, w_o, x, mod3, mod3, mod3, n2)


def _ffn_kernel(h_ref, top_ref, bot_ref, wg_ref, wu_ref, cw_ref, cb_ref, wd_ref, x1_ref, g2_ref,
                fw_ref, o_ref, hs_ref):
    i = pl.program_id(1)
    j = pl.program_id(2)
    last_i = pl.num_programs(1) - 1
    last_j = pl.num_programs(2) - 1

    @pl.when(j == 0)
    def _():
        top = top_ref[0]
        bot = bot_ref[0]
        hs_ref[0:FFN_HALO, :] = jnp.where(i == 0, jnp.zeros_like(top), top)
        hs_ref[FFN_HALO:FFN_HALO + FFN_TM, :] = h_ref[0]
        hs_ref[FFN_HALO + FFN_TM:, :] = jnp.where(i == last_i, jnp.zeros_like(bot), bot)
        o_ref[0] = jnp.zeros((FFN_TM, D_MODEL), F32)

    @pl.when(j < FFN_X1_STEPS)
    def _():
        rows = pl.ds(pl.multiple_of(j * FFN_X1_ROWS, FFN_X1_ROWS), FFN_X1_ROWS)
        o_ref[0, rows, :] += x1_ref[0]

    lo = FFN_HALO
    g2 = g2_ref[0]
    for c in range(FFN_TF // FFN_SUB):
        cs = slice(c * FFN_SUB, (c + 1) * FFN_SUB)
        g = _dot(hs_ref[...], wg_ref[:, cs].astype(BF16))
        u = _dot(hs_ref[lo:lo + FFN_TM, :], wu_ref[:, cs].astype(BF16))
        g_prev = pltpu.roll(g, 1, 0)
        g_next = pltpu.roll(g, FFN_TM + 2 * FFN_HALO - 1, 0)
        gc = (g_prev[lo:lo + FFN_TM] * cw_ref[0:1, cs]
              + g[lo:lo + FFN_TM] * cw_ref[1:2, cs]
              + g_next[lo:lo + FFN_TM] * cw_ref[2:3, cs]
              + cb_ref[:, cs])
        act = (gc * (1.0 / (1.0 + jnp.exp(-gc))) * u).astype(BF16)
        for n in range(D_MODEL // FFN_NOUT):
            ns = slice(n * FFN_NOUT, (n + 1) * FFN_NOUT)
            o_ref[0, :, ns] += g2[:, ns] * _dot(act, wd_ref[cs, ns].astype(BF16))

    @pl.when(j == last_j)
    def _():
        o_ref[0] = _rms(o_ref[0]) * fw_ref[...]


def _conv_ffn_final(h2, w_up, conv_w, conv_b, w_down, x1, mod3, final_w):
    B = h2.shape[0]
    nf = D_FF // FFN_TF
    ni = SEQ // FFN_TM
    assert nf >= FFN_X1_STEPS
    halo_per_tile = FFN_TM // FFN_HALO
    n_halo = SEQ // FFN_HALO
    return pl.pallas_call(
        _ffn_kernel,
        grid=(B, ni, nf),
        in_specs=[
            pl.BlockSpec((1, FFN_TM, D_MODEL), lambda b, i, j: (b, i, 0),
                         pipeline_mode=pl.Buffered(1)),
            pl.BlockSpec((1, FFN_HALO, D_MODEL),
                         lambda b, i, j: (b, jnp.maximum(i * halo_per_tile - 1, 0), 0)),
            pl.BlockSpec((1, FFN_HALO, D_MODEL),
                         lambda b, i, j: (b, jnp.minimum((i + 1) * halo_per_tile, n_halo - 1), 0)),
            pl.BlockSpec((D_MODEL, FFN_TF), lambda b, i, j: (0, j)),
            pl.BlockSpec((D_MODEL, FFN_TF), lambda b, i, j: (0, nf + j)),
            pl.BlockSpec((CONV_W, FFN_TF), lambda b, i, j: (0, j)),
            pl.BlockSpec((1, FFN_TF), lambda b, i, j: (0, j)),
            pl.BlockSpec((FFN_TF, D_MODEL), lambda b, i, j: (j, 0)),
            pl.BlockSpec((1, FFN_X1_ROWS, D_MODEL),
                         lambda b, i, j: (b, i * FFN_X1_STEPS + jnp.minimum(j, FFN_X1_STEPS - 1), 0)),
            pl.BlockSpec((1, 1, D_MODEL), lambda b, i, j: (b, 0, 5)),
            pl.BlockSpec((1, D_MODEL), lambda b, i, j: (0, 0)),
        ],
        out_specs=pl.BlockSpec((1, FFN_TM, D_MODEL), lambda b, i, j: (b, i, 0)),
        out_shape=jax.ShapeDtypeStruct((B, SEQ, D_MODEL), F32),
        scratch_shapes=[pltpu.VMEM((FFN_TM + 2 * FFN_HALO, D_MODEL), BF16)],
        compiler_params=_cparams(3),
        name="conv_ffn",
    )(h2, h2, h2, w_up, w_up, conv_w, conv_b, w_down, x1, mod3, final_w)


def _rope_tables():
    pos = jnp.arange(SEQ)
    row = (pos // GRID_W).astype(F32)
    col = (pos % GRID_W).astype(F32)
    nf = ROPE_DIM // 4
    inv = ROPE_BASE ** (-jnp.arange(nf, dtype=F32) / nf)
    ar = row[:, None] * inv
    ac = col[:, None] * inv
    cos = jnp.concatenate([jnp.cos(ar), jnp.cos(ar), jnp.cos(ac), jnp.cos(ac)], axis=-1)
    sin = jnp.concatenate([-jnp.sin(ar), jnp.sin(ar), -jnp.sin(ac), jnp.sin(ac)], axis=-1)
    return jnp.tile(cos, (1, LANES // ROPE_DIM)), jnp.tile(sin, (1, LANES // ROPE_DIM))


def kernel(x, c, ctx, c_ctx, w_ada, b_ada, norm1_w, w_in, q_norm_w, kv_norm_w, w_uq, w_ukv,
           lambda_q1, lambda_k1, lambda_q2, lambda_k2, subln_w, w_o, norm2_w, w_up,
           conv_w, conv_b, w_down, final_w):
    B = x.shape[0]
    assert B <= CTX_ROW and x.shape == (B, SEQ, D_MODEL) and ctx.shape == (B, CTX_LEN, D_MODEL)
    assert w_in.shape[1:] == (D_MODEL, IN_WIDTH)
    l = 0
    c8 = jnp.concatenate([c, jnp.zeros((CTX_ROW - B, D_MODEL), F32), c_ctx[None, :],
                          jnp.zeros((MOD_ROWS - CTX_ROW - 1, D_MODEL), F32)], axis=0)
    mod = _ada(c8, w_ada[l], b_ada[l][None, :])
    mod3 = mod.reshape(MOD_ROWS, 1, N_MOD * D_MODEL)

    w_in_t = jnp.swapaxes(w_in[l], 0, 1)
    wuq_b = jnp.pad(w_uq[l].reshape(Q_RANK, MLA_HEADS, MLA_QK),
                    ((0, 0), (0, 0), (0, MLA_QK_PAD - MLA_QK))
                    ).reshape(Q_RANK, MLA_HEADS * MLA_QK_PAD).astype(BF16)
    wukv3 = w_ukv[l].reshape(KV_RANK, MLA_HEADS, MLA_NOPE + MLA_V)
    wukv_b = jnp.concatenate([wukv3[:, :, :MLA_NOPE].reshape(KV_RANK, MLA_HEADS * MLA_NOPE),
                              wukv3[:, :, MLA_NOPE:].reshape(KV_RANK, MLA_WIDTH)],
                             axis=1).astype(BF16)
    cos_t, sin_t = _rope_tables()

    q_da, k_da, v_da, q_mla, k_mla, v_mla = _mixer_inputs(
        ctx, x, mod3, norm1_w[l][None, :], cos_t, sin_t, w_in_t,
        q_norm_w[l][None, :], kv_norm_w[l][None, :], wuq_b, wukv_b)
    o_da = _diff_attn(q_da, k_da, v_da, lambda_q1[l][None, :], lambda_k1[l][None, :],
                      lambda_q2[l][None, :], lambda_k2[l][None, :], subln_w[l][None, :])
    o_mla = _mla_attn(q_mla, k_mla, v_mla)
    x1, h2 = _out_proj(o_da, o_mla, w_o[l], x, mod3, norm2_w[l][None, :])
    return _conv_ffn_final(h2, w_up[l], conv_w[l], conv_b[l][None, :],
                           w_down[l], x1, mod3, final_w[None, :])
```

```python
import functools
import math

import jax
import jax.numpy as jnp
from jax import lax
from jax.experimental import pallas as pl
from jax.experimental.pallas import tpu as pltpu

D_MODEL = 2048
SEQ = 2048
CTX_LEN = 256
GRID_W = 64
HEAD_DIM = 128
DA_HEADS = 8
DA_HALF = 64
MLA_HEADS = 8
MLA_NOPE = 128
MLA_ROPE = 64
MLA_V = 128
Q_RANK = 384
KV_RANK = 256
ROPE_DIM = 64
ROPE_BASE = 10000.0
D_FF = 5632
CONV_W = 3
N_MOD = 6
EPS = 1e-6
DA_WIDTH = DA_HEADS * HEAD_DIM
MLA_WIDTH = MLA_HEADS * MLA_V
MLA_QK = MLA_NOPE + MLA_ROPE
MLA_QK_PAD = 256
IN_WIDTH = 3 * DA_WIDTH + Q_RANK + KV_RANK + MLA_ROPE
LOG2E = math.log2(math.e)
DA_SCALE = LOG2E / math.sqrt(DA_HALF)
MLA_SCALE = LOG2E / math.sqrt(MLA_QK)
LAMBDA_INIT = 0.8 - 0.6 * math.exp(-0.3 * 0)
T_ALL = CTX_LEN + SEQ

LANES = 128
MOD_ROWS = 8
CTX_ROW = 4
VMEM_LIMIT = 60 * 1024 * 1024

TOK_TILE = 256
ADA_TN = 1024
ATT_TQ = 256
DA_STEP_HEADS = 1
MLA_STEP_HEADS = 2
OUT_TM = 512
FFN_TM = 1024
FFN_TF = 512
FFN_SUB = 256
FFN_NOUT = 512
FFN_HALO = 16
FFN_X1_STEPS = 8
FFN_X1_ROWS = FFN_TM // FFN_X1_STEPS

F32 = jnp.float32
BF16 = jnp.bfloat16


def _dot(a, b):
    return jnp.dot(a, b, preferred_element_type=F32)


def _dot_nt(a, b):
    return lax.dot_general(a, b, (((1,), (1,)), ((), ())), preferred_element_type=F32)


def _rms(x):
    return x * lax.rsqrt(jnp.mean(x * x, axis=-1, keepdims=True) + EPS)


def _cparams(n_grid):
    return pltpu.CompilerParams(dimension_semantics=("arbitrary",) * n_grid,
                                vmem_limit_bytes=VMEM_LIMIT)


def _ada_kernel(c_ref, w_ref, b_ref, o_ref):
    c = c_ref[...]
    sc = c * (1.0 / (1.0 + jnp.exp(-c)))
    o_ref[...] = _dot(sc.astype(BF16), w_ref[...].astype(BF16)) + b_ref[...]


def _ada(c8, w_ada, b_ada):
    n = w_ada.shape[1]
    return pl.pallas_call(
        _ada_kernel,
        grid=(n // ADA_TN,),
        in_specs=[pl.BlockSpec((MOD_ROWS, D_MODEL), lambda j: (0, 0)),
                  pl.BlockSpec((D_MODEL, ADA_TN), lambda j: (0, j)),
                  pl.BlockSpec((1, ADA_TN), lambda j: (0, j))],
        out_specs=pl.BlockSpec((MOD_ROWS, ADA_TN), lambda j: (0, j)),
        out_shape=jax.ShapeDtypeStruct((MOD_ROWS, n), F32),
        compiler_params=_cparams(1),
        name="ada",
    )(c8, w_ada, b_ada)


def _rope_chunk(xc, cos, sin, lo_mask):
    up = pltpu.roll(xc, LANES - 16, 1)
    dn = pltpu.roll(xc, 16, 1)
    return xc * cos + jnp.where(lo_mask, up, dn) * sin


def _mixer_kernel(ctx_ref, x_ref, sh_ref, sc_ref, n1_ref, cos_ref, sin_ref, wt_ref,
                  qn_ref, kvn_ref, wuq_ref, wukv_ref,
                  qda_ref, kda_ref, vda_ref, qm_ref, km_ref, vm_ref):
    t = pl.program_id(1)
    is_ctx = t == 0
    xt = jnp.where(is_ctx, ctx_ref[0], x_ref[0])
    h = _rms(xt) * n1_ref[...]
    h = h * (1.0 + sc_ref[0]) + sh_ref[0]
    hb = h.astype(BF16)
    cos = jnp.where(is_ctx, 1.0, cos_ref[...])
    sin = jnp.where(is_ctx, 0.0, sin_ref[...])
    lane = lax.broadcasted_iota(jnp.int32, (TOK_TILE, LANES), 1)
    lo_mask = (lane % 32) < 16
    o1, o2, o3 = DA_WIDTH, 2 * DA_WIDTH, 3 * DA_WIDTH
    o4 = o3 + Q_RANK
    o5 = o4 + KV_RANK

    k = _dot_nt(hb, wt_ref[o1:o2, :].astype(BF16))
    for hh in range(DA_HEADS):
        sl = slice(hh * LANES, (hh + 1) * LANES)
        kda_ref[0, :, sl] = _rope_chunk(k[:, sl], cos, sin, lo_mask).astype(BF16)
    vda_ref[0] = _dot_nt(hb, wt_ref[o2:o3, :].astype(BF16)).astype(BF16)

    w_low = jnp.concatenate([wt_ref[o3:IN_WIDTH, :].astype(BF16),
                             jnp.zeros((LANES - MLA_ROPE, D_MODEL), BF16)], axis=0)
    low = _dot_nt(hb, w_low)
    ckv = _rms(low[:, Q_RANK:Q_RANK + KV_RANK]) * kvn_ref[...]
    kv = _dot(ckv.astype(BF16), wukv_ref[...])
    kr = _rope_chunk(low[:, Q_RANK + KV_RANK:], cos, sin, lo_mask).astype(BF16)
    for hh in range(MLA_HEADS):
        base = hh * MLA_QK_PAD
        km_ref[0, :, base:base + LANES] = kv[:, hh * LANES:(hh + 1) * LANES].astype(BF16)
        km_ref[0, :, base + LANES:base + 2 * LANES] = kr
    vm_ref[0] = kv[:, MLA_WIDTH:].astype(BF16)

    q = _dot_nt(hb, wt_ref[0:o1, :].astype(BF16))
    for hh in range(DA_HEADS):
        sl = slice(hh * LANES, (hh + 1) * LANES)
        qda_ref[0, :, sl] = (_rope_chunk(q[:, sl], cos, sin, lo_mask) * DA_SCALE).astype(BF16)
    cq = _rms(low[:, :Q_RANK]) * qn_ref[...]
    qm = _dot(cq.astype(BF16), wuq_ref[...])
    for hh in range(MLA_HEADS):
        base = hh * MLA_QK_PAD
        qm_ref[0, :, base:base + LANES] = (qm[:, base:base + LANES] * MLA_SCALE).astype(BF16)
        qr = _rope_chunk(qm[:, base + LANES:base + 2 * LANES], cos, sin, lo_mask)
        qm_ref[0, :, base + LANES:base + 2 * LANES] = (qr * MLA_SCALE).astype(BF16)


def _mixer_inputs(ctx, x, mod3, n1, cos_t, sin_t, w_in_t, qn, kvn, wuq_b, wukv_b):
    B = x.shape[0]
    nt = T_ALL // TOK_TILE

    def lat(t):
        return jnp.maximum(t - 1, 0)

    def const(shape):
        return pl.BlockSpec(shape, lambda b, t: (0,) * len(shape), pipeline_mode=pl.Buffered(1))

    in_specs = [
        pl.BlockSpec((1, CTX_LEN, D_MODEL), lambda b, t: (b, 0, 0)),
        pl.BlockSpec((1, TOK_TILE, D_MODEL), lambda b, t: (b, lat(t), 0)),
        pl.BlockSpec((1, 1, D_MODEL), lambda b, t: (jnp.where(t == 0, CTX_ROW, b), 0, 0)),
        pl.BlockSpec((1, 1, D_MODEL), lambda b, t: (jnp.where(t == 0, CTX_ROW, b), 0, 1)),
        const((1, D_MODEL)),
        pl.BlockSpec((TOK_TILE, LANES), lambda b, t: (lat(t), 0)),
        pl.BlockSpec((TOK_TILE, LANES), lambda b, t: (lat(t), 0)),
        const(w_in_t.shape),
        const((1, Q_RANK)),
        const((1, KV_RANK)),
        const(wuq_b.shape),
        const(wukv_b.shape),
    ]
    out_specs = [
        pl.BlockSpec((1, TOK_TILE, DA_WIDTH), lambda b, t: (b, lat(t), 0)),
        pl.BlockSpec((1, TOK_TILE, DA_WIDTH), lambda b, t: (b, t, 0)),
        pl.BlockSpec((1, TOK_TILE, DA_WIDTH), lambda b, t: (b, t, 0)),
        pl.BlockSpec((1, TOK_TILE, MLA_HEADS * MLA_QK_PAD), lambda b, t: (b, lat(t), 0)),
        pl.BlockSpec((1, TOK_TILE, MLA_HEADS * MLA_QK_PAD), lambda b, t: (b, t, 0)),
        pl.BlockSpec((1, TOK_TILE, MLA_WIDTH), lambda b, t: (b, t, 0)),
    ]
    out_shape = [
        jax.ShapeDtypeStruct((B, SEQ, DA_WIDTH), BF16),
        jax.ShapeDtypeStruct((B, T_ALL, DA_WIDTH), BF16),
        jax.ShapeDtypeStruct((B, T_ALL, DA_WIDTH), BF16),
        jax.ShapeDtypeStruct((B, SEQ, MLA_HEADS * MLA_QK_PAD), BF16),
        jax.ShapeDtypeStruct((B, T_ALL, MLA_HEADS * MLA_QK_PAD), BF16),
        jax.ShapeDtypeStruct((B, T_ALL, MLA_WIDTH), BF16),
    ]
    return pl.pallas_call(
        _mixer_kernel,
        grid=(B, nt),
        in_specs=in_specs,
        out_specs=out_specs,
        out_shape=out_shape,
        compiler_params=_cparams(2),
        name="mixer_in",
    )(ctx, x, mod3, mod3, n1, cos_t, sin_t, w_in_t, qn, kvn, wuq_b, wukv_b)


def _fill_values_with_ones(v_ref, vx_ref):
    width = vx_ref.shape[-1] // 2
    ones_lane = lax.broadcasted_iota(jnp.int32, (T_ALL, width), 1) == 0
    for hh in range(vx_ref.shape[0]):
        vx_ref[hh, :, 0:width] = v_ref[0, :, hh * width:(hh + 1) * width]
        vx_ref[hh, :, width:] = jnp.where(ones_lane, 1.0, 0.0).astype(BF16)


def _softmax_weighted(s, vx):
    width = vx.shape[-1] // 2
    p = jnp.exp2(s - jnp.max(s, axis=-1, keepdims=True))
    ox = _dot(p.astype(BF16), vx)
    return ox[:, 0:width] * (1.0 / ox[:, width:width + 1])


def _pipelined_tiles(scores, finish, n_heads):
    n_tiles = SEQ // ATT_TQ
    s = [scores(0, hh) for hh in range(n_heads)]
    for n in range(n_tiles):
        s_next = [scores(n + 1, hh) for hh in range(n_heads)] if n + 1 < n_tiles else None
        for hh in range(n_heads):
            finish(n, hh, s[hh])
        s = s_next


def _diff_attn_kernel(q_ref, k_ref, v_ref, lq1_ref, lk1_ref, lq2_ref, lk2_ref, sub_ref, o_ref,
                      vx_ref):
    lam = (jnp.exp(jnp.sum(lq1_ref[...] * lk1_ref[...], axis=-1, keepdims=True))
           - jnp.exp(jnp.sum(lq2_ref[...] * lk2_ref[...], axis=-1, keepdims=True))
           + LAMBDA_INIT)
    lane = lax.broadcasted_iota(jnp.int32, (ATT_TQ, HEAD_DIM), 1)
    _fill_values_with_ones(v_ref, vx_ref)

    def scores(n, hh):
        cols = slice(hh * HEAD_DIM, (hh + 1) * HEAD_DIM)
        q = q_ref[0, n * ATT_TQ:(n + 1) * ATT_TQ, cols]
        zero = jnp.zeros_like(q)
        q1 = jnp.where(lane < DA_HALF, q, zero)
        q2 = jnp.where(lane >= DA_HALF, q, zero)
        k = k_ref[0, :, cols]
        return _dot_nt(q1, k), _dot_nt(q2, k)

    def finish(n, hh, s):
        o = _softmax_weighted(s[0], vx_ref[hh]) - lam * _softmax_weighted(s[1], vx_ref[hh])
        o = _rms(o) * sub_ref[...] * (1.0 - LAMBDA_INIT)
        o_ref[0, n * ATT_TQ:(n + 1) * ATT_TQ, hh * HEAD_DIM:(hh + 1) * HEAD_DIM] = o.astype(BF16)

    _pipelined_tiles(scores, finish, DA_STEP_HEADS)


def _diff_attn(q, k, v, lq1, lk1, lq2, lk2, subln):
    B = q.shape[0]
    small = lambda n: pl.BlockSpec((1, n), lambda b, h: (0, 0))
    width = DA_STEP_HEADS * HEAD_DIM
    return pl.pallas_call(
        _diff_attn_kernel,
        grid=(B, DA_HEADS // DA_STEP_HEADS),
        in_specs=[pl.BlockSpec((1, SEQ, width), lambda b, h: (b, 0, h)),
                  pl.BlockSpec((1, T_ALL, width), lambda b, h: (b, 0, h)),
                  pl.BlockSpec((1, T_ALL, width), lambda b, h: (b, 0, h)),
                  small(DA_HALF), small(DA_HALF), small(DA_HALF), small(DA_HALF),
                  small(HEAD_DIM)],
        out_specs=pl.BlockSpec((1, SEQ, width), lambda b, h: (b, 0, h)),
        out_shape=jax.ShapeDtypeStruct((B, SEQ, DA_WIDTH), BF16),
        scratch_shapes=[pltpu.VMEM((DA_STEP_HEADS, T_ALL, 2 * HEAD_DIM), BF16)],
        compiler_params=_cparams(2),
        name="diff_attn",
    )(q, k, v, lq1, lk1, lq2, lk2, subln)


def _mla_attn_kernel(q_ref, k_ref, v_ref, o_ref, vx_ref):
    _fill_values_with_ones(v_ref, vx_ref)

    def scores(n, hh):
        cols = slice(hh * MLA_QK_PAD, (hh + 1) * MLA_QK_PAD)
        return _dot_nt(q_ref[0, n * ATT_TQ:(n + 1) * ATT_TQ, cols], k_ref[0, :, cols])

    def finish(n, hh, s):
        o = _softmax_weighted(s, vx_ref[hh])
        o_ref[0, n * ATT_TQ:(n + 1) * ATT_TQ, hh * MLA_V:(hh + 1) * MLA_V] = o.astype(BF16)

    _pipelined_tiles(scores, finish, MLA_STEP_HEADS)


def _mla_attn(q, k, v):
    B = q.shape[0]
    return pl.pallas_call(
        _mla_attn_kernel,
        grid=(B, MLA_HEADS // MLA_STEP_HEADS),
        in_specs=[pl.BlockSpec((1, SEQ, MLA_STEP_HEADS * MLA_QK_PAD), lambda b, h: (b, 0, h)),
                  pl.BlockSpec((1, T_ALL, MLA_STEP_HEADS * MLA_QK_PAD), lambda b, h: (b, 0, h)),
                  pl.BlockSpec((1, T_ALL, MLA_STEP_HEADS * MLA_V), lambda b, h: (b, 0, h))],
        out_specs=pl.BlockSpec((1, SEQ, MLA_STEP_HEADS * MLA_V), lambda b, h: (b, 0, h)),
        out_shape=jax.ShapeDtypeStruct((B, SEQ, MLA_WIDTH), BF16),
        scratch_shapes=[pltpu.VMEM((MLA_STEP_HEADS, T_ALL, 2 * MLA_V), BF16)],
        compiler_params=_cparams(2),
        name="mla_attn",
    )(q, k, v)


def _out_proj_kernel(oda_ref, omla_ref, wo_ref, x_ref, g1_ref, sh2_ref, sc2_ref, n2_ref,
                     x1_ref, h2_ref, wob_ref):
    @pl.when((pl.program_id(0) == 0) & (pl.program_id(1) == 0))
    def _():
        wob_ref[...] = wo_ref[...].astype(BF16)

    y = _dot(oda_ref[0], wob_ref[0:DA_WIDTH, :]) + _dot(omla_ref[0], wob_ref[DA_WIDTH:, :])
    x1 = x_ref[0] + g1_ref[0] * y
    x1_ref[0] = x1
    h2 = _rms(x1) * n2_ref[...]
    h2_ref[0] = (h2 * (1.0 + sc2_ref[0]) + sh2_ref[0]).astype(BF16)


def _out_proj(o_da, o_mla, w_o, x, mod3, n2):
    B = x.shape[0]
    tile = lambda w: pl.BlockSpec((1, OUT_TM, w), lambda b, i: (b, i, 0))
    modrow = lambda col: pl.BlockSpec((1, 1, D_MODEL), lambda b, i: (b, 0, col))
    return pl.pallas_call(
        _out_proj_kernel,
        grid=(B, SEQ // OUT_TM),
        in_specs=[tile(DA_WIDTH), tile(MLA_WIDTH),
                  pl.BlockSpec(w_o.shape, lambda b, i: (0, 0), pipeline_mode=pl.Buffered(1)),
                  tile(D_MODEL), modrow(2), modrow(3), modrow(4),
                  pl.BlockSpec((1, D_MODEL), lambda b, i: (0, 0))],
        out_specs=[tile(D_MODEL), tile(D_MODEL)],
        out_shape=[jax.ShapeDtypeStruct((B, SEQ, D_MODEL), F32),
                   jax.ShapeDtypeStruct((B, SEQ, D_MODEL), BF16)],
        scratch_shapes=[pltpu.VMEM((DA_WIDTH + MLA_WIDTH, D_MODEL), BF16)],
        compiler_params=_cparams(2),
        name="out_proj",
    )(o_da, o_mla, w_o, x, mod3, mod3, mod3, n2)


def _ffn_kernel(h_ref, top_ref, bot_ref, wg_ref, wu_ref, cw_ref, cb_ref, wd_ref, x1_ref, g2_ref,
                fw_ref, o_ref, hs_ref):
    i = pl.program_id(1)
    j = pl.program_id(2)
    last_i = pl.num_programs(1) - 1
    last_j = pl.num_programs(2) - 1

    @pl.when(j == 0)
    def _():
        top = top_ref[0]
        bot = bot_ref[0]
        hs_ref[0:FFN_HALO, :] = jnp.where(i == 0, jnp.zeros_like(top), top)
        hs_ref[FFN_HALO:FFN_HALO + FFN_TM, :] = h_ref[0]
        hs_ref[FFN_HALO + FFN_TM:, :] = jnp.where(i == last_i, jnp.zeros_like(bot), bot)
        o_ref[0] = jnp.zeros((FFN_TM, D_MODEL), F32)

    @pl.when(j < FFN_X1_STEPS)
    def _():
        rows = pl.ds(pl.multiple_of(j * FFN_X1_ROWS, FFN_X1_ROWS), FFN_X1_ROWS)
        o_ref[0, rows, :] += x1_ref[0]

    lo = FFN_HALO
    g2 = g2_ref[0]
    for c in range(FFN_TF // FFN_SUB):
        cs = slice(c * FFN_SUB, (c + 1) * FFN_SUB)
        g = _dot(hs_ref[...], wg_ref[:, cs].astype(BF16))
        u = _dot(hs_ref[lo:lo + FFN_TM, :], wu_ref[:, cs].astype(BF16))
        g_prev = pltpu.roll(g, 1, 0)
        g_next = pltpu.roll(g, FFN_TM + 2 * FFN_HALO - 1, 0)
        gc = (g_prev[lo:lo + FFN_TM] * cw_ref[0:1, cs]
              + g[lo:lo + FFN_TM] * cw_ref[1:2, cs]
              + g_next[lo:lo + FFN_TM] * cw_ref[2:3, cs]
              + cb_ref[:, cs])
        act = (gc * (1.0 / (1.0 + jnp.exp(-gc))) * u).astype(BF16)
        for n in range(D_MODEL // FFN_NOUT):
            ns = slice(n * FFN_NOUT, (n + 1) * FFN_NOUT)
            o_ref[0, :, ns] += g2[:, ns] * _dot(act, wd_ref[cs, ns].astype(BF16))

    @pl.when(j == last_j)
    def _():
        o_ref[0] = _rms(o_ref[0]) * fw_ref[...]


def _conv_ffn_final(h2, w_up, conv_w, conv_b, w_down, x1, mod3, final_w):
    B = h2.shape[0]
    nf = D_FF // FFN_TF
    ni = SEQ // FFN_TM
    assert nf >= FFN_X1_STEPS
    halo_per_tile = FFN_TM // FFN_HALO
    n_halo = SEQ // FFN_HALO
    return pl.pallas_call(
        _ffn_kernel,
        grid=(B, ni, nf),
        in_specs=[
            pl.BlockSpec((1, FFN_TM, D_MODEL), lambda b, i, j: (b, i, 0),
                         pipeline_mode=pl.Buffered(1)),
            pl.BlockSpec((1, FFN_HALO, D_MODEL),
                         lambda b, i, j: (b, jnp.maximum(i * halo_per_tile - 1, 0), 0)),
            pl.BlockSpec((1, FFN_HALO, D_MODEL),
                         lambda b, i, j: (b, jnp.minimum((i + 1) * halo_per_tile, n_halo - 1), 0)),
            pl.BlockSpec((D_MODEL, FFN_TF), lambda b, i, j: (0, j)),
            pl.BlockSpec((D_MODEL, FFN_TF), lambda b, i, j: (0, nf + j)),
            pl.BlockSpec((CONV_W, FFN_TF), lambda b, i, j: (0, j)),
            pl.BlockSpec((1, FFN_TF), lambda b, i, j: (0, j)),
            pl.BlockSpec((FFN_TF, D_MODEL), lambda b, i, j: (j, 0)),
            pl.BlockSpec((1, FFN_X1_ROWS, D_MODEL),
                         lambda b, i, j: (b, i * FFN_X1_STEPS + jnp.minimum(j, FFN_X1_STEPS - 1), 0)),
            pl.BlockSpec((1, 1, D_MODEL), lambda b, i, j: (b, 0, 5)),
            pl.BlockSpec((1, D_MODEL), lambda b, i, j: (0, 0)),
        ],
        out_specs=pl.BlockSpec((1, FFN_TM, D_MODEL), lambda b, i, j: (b, i, 0)),
        out_shape=jax.ShapeDtypeStruct((B, SEQ, D_MODEL), F32),
        scratch_shapes=[pltpu.VMEM((FFN_TM + 2 * FFN_HALO, D_MODEL), BF16)],
        compiler_params=_cparams(3),
        name="conv_ffn",
    )(h2, h2, h2, w_up, w_up, conv_w, conv_b, w_down, x1, mod3, final_w)


def _rope_tables():
    pos = jnp.arange(SEQ)
    row = (pos // GRID_W).astype(F32)
    col = (pos % GRID_W).astype(F32)
    nf = ROPE_DIM // 4
    inv = ROPE_BASE ** (-jnp.arange(nf, dtype=F32) / nf)
    ar = row[:, None] * inv
    ac = col[:, None] * inv
    cos = jnp.concatenate([jnp.cos(ar), jnp.cos(ar), jnp.cos(ac), jnp.cos(ac)], axis=-1)
    sin = jnp.concatenate([-jnp.sin(ar), jnp.sin(ar), -jnp.sin(ac), jnp.sin(ac)], axis=-1)
    return jnp.tile(cos, (1, LANES // ROPE_DIM)), jnp.tile(sin, (1, LANES // ROPE_DIM))


def kernel(x, c, ctx, c_ctx, w_ada, b_ada, norm1_w, w_in, q_norm_w, kv_norm_w, w_uq, w_ukv,
           lambda_q1, lambda_k1, lambda_q2, lambda_k2, subln_w, w_o, norm2_w, w_up,
           conv_w, conv_b, w_down, final_w):
    B = x.shape[0]
    assert B <= CTX_ROW and x.shape == (B, SEQ, D_MODEL) and ctx.shape == (B, CTX_LEN, D_MODEL)
    assert w_in.shape[1:] == (D_MODEL, IN_WIDTH)
    l = 0
    c8 = jnp.concatenate([c, jnp.zeros((CTX_ROW - B, D_MODEL), F32), c_ctx[None, :],
                          jnp.zeros((MOD_ROWS - CTX_ROW - 1, D_MODEL), F32)], axis=0)
    mod = _ada(c8, w_ada[l], b_ada[l][None, :])
    mod3 = mod.reshape(MOD_ROWS, 1, N_MOD * D_MODEL)

    w_in_t = jnp.swapaxes(w_in[l], 0, 1)
    wuq_b = jnp.pad(w_uq[l].reshape(Q_RANK, MLA_HEADS, MLA_QK),
                    ((0, 0), (0, 0), (0, MLA_QK_PAD - MLA_QK))
                    ).reshape(Q_RANK, MLA_HEADS * MLA_QK_PAD).astype(BF16)
    wukv3 = w_ukv[l].reshape(KV_RANK, MLA_HEADS, MLA_NOPE + MLA_V)
    wukv_b = jnp.concatenate([wukv3[:, :, :MLA_NOPE].reshape(KV_RANK, MLA_HEADS * MLA_NOPE),
                              wukv3[:, :, MLA_NOPE:].reshape(KV_RANK, MLA_WIDTH)],
                             axis=1).astype(BF16)
    cos_t, sin_t = _rope_tables()

    q_da, k_da, v_da, q_mla, k_mla, v_mla = _mixer_inputs(
        ctx, x, mod3, norm1_w[l][None, :], cos_t, sin_t, w_in_t,
        q_norm_w[l][None, :], kv_norm_w[l][None, :], wuq_b, wukv_b)
    o_da = _diff_attn(q_da, k_da, v_da, lambda_q1[l][None, :], lambda_k1[l][None, :],
                      lambda_q2[l][None, :], lambda_k2[l][None, :], subln_w[l][None, :])
    o_mla = _mla_attn(q_mla, k_mla, v_mla)
    x1, h2 = _out_proj(o_da, o_mla, w_o[l], x, mod3, norm2_w[l][None, :])
    return _conv_ffn_final(h2, w_up[l], conv_w[l], conv_b[l][None, :],
                           w_down[l], x1, mod3, final_w[None, :])
```

```python
import functools
import math

import jax
import jax.numpy as jnp
from jax import lax
from jax.experimental import pallas as pl
from jax.experimental.pallas import tpu as pltpu

D_MODEL = 2048
SEQ = 2048
CTX_LEN = 256
GRID_W = 64
HEAD_DIM = 128
DA_HEADS = 8
DA_HALF = 64
MLA_HEADS = 8
MLA_NOPE = 128
MLA_ROPE = 64
MLA_V = 128
Q_RANK = 384
KV_RANK = 256
ROPE_DIM = 64
ROPE_BASE = 10000.0
D_FF = 5632
CONV_W = 3
N_MOD = 6
EPS = 1e-6
DA_WIDTH = DA_HEADS * HEAD_DIM
MLA_WIDTH = MLA_HEADS * MLA_V
MLA_QK = MLA_NOPE + MLA_ROPE
MLA_QK_PAD = 256
IN_WIDTH = 3 * DA_WIDTH + Q_RANK + KV_RANK + MLA_ROPE
LOG2E = math.log2(math.e)
DA_SCALE = LOG2E / math.sqrt(DA_HALF)
MLA_SCALE = LOG2E / math.sqrt(MLA_QK)
LAMBDA_INIT = 0.8 - 0.6 * math.exp(-0.3 * 0)
T_ALL = CTX_LEN + SEQ

LANES = 128
MOD_ROWS = 8
CTX_ROW = 4
VMEM_LIMIT = 60 * 1024 * 1024

TOK_TILE = 256
ADA_TN = 2048
ATT_TQ = 256
DA_STEP_HEADS = 1
MLA_STEP_HEADS = 2
OUT_TM = 512
FFN_TM = 1024
FFN_TF = 512
FFN_SUB = 512
FFN_NOUT = 512
FFN_HALO = 16
FFN_X1_STEPS = 8
FFN_X1_ROWS = FFN_TM // FFN_X1_STEPS

F32 = jnp.float32
BF16 = jnp.bfloat16


def _dot(a, b):
    return jnp.dot(a, b, preferred_element_type=F32)


def _dot_nt(a, b):
    return lax.dot_general(a, b, (((1,), (1,)), ((), ())), preferred_element_type=F32)


def _rms(x):
    return x * lax.rsqrt(jnp.mean(x * x, axis=-1, keepdims=True) + EPS)


def _cparams(n_grid):
    return pltpu.CompilerParams(dimension_semantics=("arbitrary",) * n_grid,
                                vmem_limit_bytes=VMEM_LIMIT)


def _ada_kernel(c_ref, w_ref, b_ref, o_ref):
    c = c_ref[...]
    sc = c * (1.0 / (1.0 + jnp.exp(-c)))
    o_ref[...] = _dot(sc.astype(BF16), w_ref[...].astype(BF16)) + b_ref[...]


def _ada(c8, w_ada, b_ada):
    n = w_ada.shape[1]
    return pl.pallas_call(
        _ada_kernel,
        grid=(n // ADA_TN,),
        in_specs=[pl.BlockSpec((MOD_ROWS, D_MODEL), lambda j: (0, 0)),
                  pl.BlockSpec((D_MODEL, ADA_TN), lambda j: (0, j)),
                  pl.BlockSpec((1, ADA_TN), lambda j: (0, j))],
        out_specs=pl.BlockSpec((MOD_ROWS, ADA_TN), lambda j: (0, j)),
        out_shape=jax.ShapeDtypeStruct((MOD_ROWS, n), F32),
        compiler_params=_cparams(1),
        name="ada",
    )(c8, w_ada, b_ada)


def _rope_chunk(xc, cos, sin, lo_mask):
    up = pltpu.roll(xc, LANES - 16, 1)
    dn = pltpu.roll(xc, 16, 1)
    return xc * cos + jnp.where(lo_mask, up, dn) * sin


def _mixer_kernel(ctx_ref, x_ref, sh_ref, sc_ref, n1_ref, cos_ref, sin_ref, wt_ref,
                  qn_ref, kvn_ref, wuq_ref, wukv_ref,
                  qda_ref, kda_ref, vda_ref, qm_ref, km_ref, vm_ref):
    t = pl.program_id(1)
    is_ctx = t == 0
    xt = jnp.where(is_ctx, ctx_ref[0], x_ref[0])
    h = _rms(xt) * n1_ref[...]
    h = h * (1.0 + sc_ref[0]) + sh_ref[0]
    hb = h.astype(BF16)
    cos = jnp.where(is_ctx, 1.0, cos_ref[...])
    sin = jnp.where(is_ctx, 0.0, sin_ref[...])
    lane = lax.broadcasted_iota(jnp.int32, (TOK_TILE, LANES), 1)
    lo_mask = (lane % 32) < 16
    o1, o2, o3 = DA_WIDTH, 2 * DA_WIDTH, 3 * DA_WIDTH
    o4 = o3 + Q_RANK
    o5 = o4 + KV_RANK

    k = _dot_nt(hb, wt_ref[o1:o2, :].astype(BF16))
    for hh in range(DA_HEADS):
        sl = slice(hh * LANES, (hh + 1) * LANES)
        kda_ref[0, :, sl] = _rope_chunk(k[:, sl], cos, sin, lo_mask).astype(BF16)
    vda_ref[0] = _dot_nt(hb, wt_ref[o2:o3, :].astype(BF16)).astype(BF16)

    w_low = jnp.concatenate([wt_ref[o3:IN_WIDTH, :].astype(BF16),
                             jnp.zeros((LANES - MLA_ROPE, D_MODEL), BF16)], axis=0)
    low = _dot_nt(hb, w_low)
    ckv = _rms(low[:, Q_RANK:Q_RANK + KV_RANK]) * kvn_ref[...]
    kv = _dot(ckv.astype(BF16), wukv_ref[...])
    kr = _rope_chunk(low[:, Q_RANK + KV_RANK:], cos, sin, lo_mask).astype(BF16)
    for hh in range(MLA_HEADS):
        base = hh * MLA_QK_PAD
        km_ref[0, :, base:base + LANES] = kv[:, hh * LANES:(hh + 1) * LANES].astype(BF16)
        km_ref[0, :, base + LANES:base + 2 * LANES] = kr
    vm_ref[0] = kv[:, MLA_WIDTH:].astype(BF16)

    q = _dot_nt(hb, wt_ref[0:o1, :].astype(BF16))
    for hh in range(DA_HEADS):
        sl = slice(hh * LANES, (hh + 1) * LANES)
        qda_ref[0, :, sl] = (_rope_chunk(q[:, sl], cos, sin, lo_mask) * DA_SCALE).astype(BF16)
    cq = _rms(low[:, :Q_RANK]) * qn_ref[...]
    qm = _dot(cq.astype(BF16), wuq_ref[...])
    for hh in range(MLA_HEADS):
        base = hh * MLA_QK_PAD
        qm_ref[0, :, base:base + LANES] = (qm[:, base:base + LANES] * MLA_SCALE).astype(BF16)
        qr = _rope_chunk(qm[:, base + LANES:base + 2 * LANES], cos, sin, lo_mask)
        qm_ref[0, :, base + LANES:base + 2 * LANES] = (qr * MLA_SCALE).astype(BF16)


def _mixer_inputs(ctx, x, mod3, n1, cos_t, sin_t, w_in_t, qn, kvn, wuq_b, wukv_b):
    B = x.shape[0]
    nt = T_ALL // TOK_TILE

    def lat(t):
        return jnp.maximum(t - 1, 0)

    def const(shape):
        return pl.BlockSpec(shape, lambda b, t: (0,) * len(shape), pipeline_mode=pl.Buffered(1))

    in_specs = [
        pl.BlockSpec((1, CTX_LEN, D_MODEL), lambda b, t: (b, 0, 0)),
        pl.BlockSpec((1, TOK_TILE, D_MODEL), lambda b, t: (b, lat(t), 0)),
        pl.BlockSpec((1, 1, D_MODEL), lambda b, t: (jnp.where(t == 0, CTX_ROW, b), 0, 0)),
        pl.BlockSpec((1, 1, D_MODEL), lambda b, t: (jnp.where(t == 0, CTX_ROW, b), 0, 1)),
        const((1, D_MODEL)),
        pl.BlockSpec((TOK_TILE, LANES), lambda b, t: (lat(t), 0)),
        pl.BlockSpec((TOK_TILE, LANES), lambda b, t: (lat(t), 0)),
        const(w_in_t.shape),
        const((1, Q_RANK)),
        const((1, KV_RANK)),
        const(wuq_b.shape),
        const(wukv_b.shape),
    ]
    out_specs = [
        pl.BlockSpec((1, TOK_TILE, DA_WIDTH), lambda b, t: (b, lat(t), 0)),
        pl.BlockSpec((1, TOK_TILE, DA_WIDTH), lambda b, t: (b, t, 0)),
        pl.BlockSpec((1, TOK_TILE, DA_WIDTH), lambda b, t: (b, t, 0)),
        pl.BlockSpec((1, TOK_TILE, MLA_HEADS * MLA_QK_PAD), lambda b, t: (b, lat(t), 0)),
        pl.BlockSpec((1, TOK_TILE, MLA_HEADS * MLA_QK_PAD), lambda b, t: (b, t, 0)),
        pl.BlockSpec((1, TOK_TILE, MLA_WIDTH), lambda b, t: (b, t, 0)),
    ]
    out_shape = [
        jax.ShapeDtypeStruct((B, SEQ, DA_WIDTH), BF16),
        jax.ShapeDtypeStruct((B, T_ALL, DA_WIDTH), BF16),
        jax.ShapeDtypeStruct((B, T_ALL, DA_WIDTH), BF16),
        jax.ShapeDtypeStruct((B, SEQ, MLA_HEADS * MLA_QK_PAD), BF16),
        jax.ShapeDtypeStruct((B, T_ALL, MLA_HEADS * MLA_QK_PAD), BF16),
        jax.ShapeDtypeStruct((B, T_ALL, MLA_WIDTH), BF16),
    ]
    return pl.pallas_call(
        _mixer_kernel,
        grid=(B, nt),
        in_specs=in_specs,
        out_specs=out_specs,
        out_shape=out_shape,
        compiler_params=_cparams(2),
        name="mixer_in",
    )(ctx, x, mod3, mod3, n1, cos_t, sin_t, w_in_t, qn, kvn, wuq_b, wukv_b)


def _fill_values_with_ones(v_ref, vx_ref):
    width = vx_ref.shape[-1] // 2
    ones_lane = lax.broadcasted_iota(jnp.int32, (T_ALL, width), 1) == 0
    for hh in range(vx_ref.shape[0]):
        vx_ref[hh, :, 0:width] = v_ref[0, :, hh * width:(hh + 1) * width]
        vx_ref[hh, :, width:] = jnp.where(ones_lane, 1.0, 0.0).astype(BF16)


def _softmax_weighted(s, vx):
    width = vx.shape[-1] // 2
    p = jnp.exp2(s - jnp.max(s, axis=-1, keepdims=True))
    ox = _dot(p.astype(BF16), vx)
    return ox[:, 0:width] * (1.0 / ox[:, width:width + 1])


def _pipelined_tiles(scores, finish, n_heads):
    n_tiles = SEQ // ATT_TQ
    s = [scores(0, hh) for hh in range(n_heads)]
    for n in range(n_tiles):
        s_next = [scores(n + 1, hh) for hh in range(n_heads)] if n + 1 < n_tiles else None
        for hh in range(n_heads):
            finish(n, hh, s[hh])
        s = s_next


def _diff_attn_kernel(q_ref, k_ref, v_ref, lq1_ref, lk1_ref, lq2_ref, lk2_ref, sub_ref, o_ref,
                      vx_ref):
    lam = (jnp.exp(jnp.sum(lq1_ref[...] * lk1_ref[...], axis=-1, keepdims=True))
           - jnp.exp(jnp.sum(lq2_ref[...] * lk2_ref[...], axis=-1, keepdims=True))
           + LAMBDA_INIT)
    lane = lax.broadcasted_iota(jnp.int32, (ATT_TQ, HEAD_DIM), 1)
    _fill_values_with_ones(v_ref, vx_ref)

    def scores(n, hh):
        cols = slice(hh * HEAD_DIM, (hh + 1) * HEAD_DIM)
        q = q_ref[0, n * ATT_TQ:(n + 1) * ATT_TQ, cols]
        zero = jnp.zeros_like(q)
        q1 = jnp.where(lane < DA_HALF, q, zero)
        q2 = jnp.where(lane >= DA_HALF, q, zero)
        k = k_ref[0, :, cols]
        return _dot_nt(q1, k), _dot_nt(q2, k)

    def finish(n, hh, s):
        o = _softmax_weighted(s[0], vx_ref[hh]) - lam * _softmax_weighted(s[1], vx_ref[hh])
        o = _rms(o) * sub_ref[...] * (1.0 - LAMBDA_INIT)
        o_ref[0, n * ATT_TQ:(n + 1) * ATT_TQ, hh * HEAD_DIM:(hh + 1) * HEAD_DIM] = o.astype(BF16)

    _pipelined_tiles(scores, finish, DA_STEP_HEADS)


def _diff_attn(q, k, v, lq1, lk1, lq2, lk2, subln):
    B = q.shape[0]
    small = lambda n: pl.BlockSpec((1, n), lambda b, h: (0, 0))
    width = DA_STEP_HEADS * HEAD_DIM
    return pl.pallas_call(
        _diff_attn_kernel,
        grid=(B, DA_HEADS // DA_STEP_HEADS),
        in_specs=[pl.BlockSpec((1, SEQ, width), lambda b, h: (b, 0, h)),
                  pl.BlockSpec((1, T_ALL, width), lambda b, h: (b, 0, h)),
                  pl.BlockSpec((1, T_ALL, width), lambda b, h: (b, 0, h)),
                  small(DA_HALF), small(DA_HALF), small(DA_HALF), small(DA_HALF),
                  small(HEAD_DIM)],
        out_specs=pl.BlockSpec((1, SEQ, width), lambda b, h: (b, 0, h)),
        out_shape=jax.ShapeDtypeStruct((B, SEQ, DA_WIDTH), BF16),
        scratch_shapes=[pltpu.VMEM((DA_STEP_HEADS, T_ALL, 2 * HEAD_DIM), BF16)],
        compiler_params=_cparams(2),
        name="diff_attn",
    )(q, k, v, lq1, lk1, lq2, lk2, subln)


def _mla_attn_kernel(q_ref, k_ref, v_ref, o_ref, vx_ref):
    _fill_values_with_ones(v_ref, vx_ref)

    def scores(n, hh):
        cols = slice(hh * MLA_QK_PAD, (hh + 1) * MLA_QK_PAD)
        return _dot_nt(q_ref[0, n * ATT_TQ:(n + 1) * ATT_TQ, cols], k_ref[0, :, cols])

    def finish(n, hh, s):
        o = _softmax_weighted(s, vx_ref[hh])
        o_ref[0, n * ATT_TQ:(n + 1) * ATT_TQ, hh * MLA_V:(hh + 1) * MLA_V] = o.astype(BF16)

    _pipelined_tiles(scores, finish, MLA_STEP_HEADS)


def _mla_attn(q, k, v):
    B = q.shape[0]
    return pl.pallas_call(
        _mla_attn_kernel,
        grid=(B, MLA_HEADS // MLA_STEP_HEADS),
        in_specs=[pl.BlockSpec((1, SEQ, MLA_STEP_HEADS * MLA_QK_PAD), lambda b, h: (b, 0, h)),
                  pl.BlockSpec((1, T_ALL, MLA_STEP_HEADS * MLA_QK_PAD), lambda b, h: (b, 0, h)),
                  pl.BlockSpec((1, T_ALL, MLA_STEP_HEADS * MLA_V), lambda b, h: (b, 0, h))],
        out_specs=pl.BlockSpec((1, SEQ, MLA_STEP_HEADS * MLA_V), lambda b, h: (b, 0, h)),
        out_shape=jax.ShapeDtypeStruct((B, SEQ, MLA_WIDTH), BF16),
        scratch_shapes=[pltpu.VMEM((MLA_STEP_HEADS, T_ALL, 2 * MLA_V), BF16)],
        compiler_params=_cparams(2),
        name="mla_attn",
    )(q, k, v)


def _out_proj_kernel(oda_ref, omla_ref, wo_ref, x_ref, g1_ref, sh2_ref, sc2_ref, n2_ref,
                     x1_ref, h2_ref):
    y = (_dot(oda_ref[0], wo_ref[0:DA_WIDTH, :].astype(BF16))
         + _dot(omla_ref[0], wo_ref[DA_WIDTH:, :].astype(BF16)))
    x1 = x_ref[0] + g1_ref[0] * y
    x1_ref[0] = x1
    h2 = _rms(x1) * n2_ref[...]
    h2_ref[0] = (h2 * (1.0 + sc2_ref[0]) + sh2_ref[0]).astype(BF16)


def _out_proj(o_da, o_mla, w_o, x, mod3, n2):
    B = x.shape[0]
    tile = lambda w: pl.BlockSpec((1, OUT_TM, w), lambda b, i: (b, i, 0))
    modrow = lambda col: pl.BlockSpec((1, 1, D_MODEL), lambda b, i: (b, 0, col))
    return pl.pallas_call(
        _out_proj_kernel,
        grid=(B, SEQ // OUT_TM),
        in_specs=[tile(DA_WIDTH), tile(MLA_WIDTH),
                  pl.BlockSpec(w_o.shape, lambda b, i: (0, 0), pipeline_mode=pl.Buffered(1)),
                  tile(D_MODEL), modrow(2), modrow(3), modrow(4),
                  pl.BlockSpec((1, D_MODEL), lambda b, i: (0, 0))],
        out_specs=[tile(D_MODEL), tile(D_MODEL)],
        out_shape=[jax.ShapeDtypeStruct((B, SEQ, D_MODEL), F32),
                   jax.ShapeDtypeStruct((B, SEQ, D_MODEL), BF16)],
        compiler_params=_cparams(2),
        name="out_proj",
    )(o_da, o_mla, w_o, x, mod3, mod3, mod3, n2)


def _ffn_kernel(h_ref, top_ref, bot_ref, wg_ref, wu_ref, cw_ref, cb_ref, wd_ref, x1_ref, g2_ref,
                fw_ref, o_ref, hs_ref):
    i = pl.program_id(1)
    j = pl.program_id(2)
    last_i = pl.num_programs(1) - 1
    last_j = pl.num_programs(2) - 1

    @pl.when(j == 0)
    def _():
        top = top_ref[0]
        bot = bot_ref[0]
        hs_ref[0:FFN_HALO, :] = jnp.where(i == 0, jnp.zeros_like(top), top)
        hs_ref[FFN_HALO:FFN_HALO + FFN_TM, :] = h_ref[0]
        hs_ref[FFN_HALO + FFN_TM:, :] = jnp.where(i == last_i, jnp.zeros_like(bot), bot)
        o_ref[0] = jnp.zeros((FFN_TM, D_MODEL), F32)

    @pl.when(j < FFN_X1_STEPS)
    def _():
        rows = pl.ds(pl.multiple_of(j * FFN_X1_ROWS, FFN_X1_ROWS), FFN_X1_ROWS)
        o_ref[0, rows, :] += x1_ref[0]

    lo = FFN_HALO
    g2 = g2_ref[0]
    for c in range(FFN_TF // FFN_SUB):
        cs = slice(c * FFN_SUB, (c + 1) * FFN_SUB)
        g = _dot(hs_ref[...], wg_ref[:, cs].astype(BF16))
        u = _dot(hs_ref[lo:lo + FFN_TM, :], wu_ref[:, cs].astype(BF16))
        g_prev = pltpu.roll(g, 1, 0)
        g_next = pltpu.roll(g, FFN_TM + 2 * FFN_HALO - 1, 0)
        gc = (g_prev[lo:lo + FFN_TM] * cw_ref[0:1, cs]
              + g[lo:lo + FFN_TM] * cw_ref[1:2, cs]
              + g_next[lo:lo + FFN_TM] * cw_ref[2:3, cs]
              + cb_ref[:, cs])
        act = (gc * (1.0 / (1.0 + jnp.exp(-gc))) * u).astype(BF16)
        for n in range(D_MODEL // FFN_NOUT):
            ns = slice(n * FFN_NOUT, (n + 1) * FFN_NOUT)
            o_ref[0, :, ns] += g2[:, ns] * _dot(act, wd_ref[cs, ns].astype(BF16))

    @pl.when(j == last_j)
    def _():
        o_ref[0] = _rms(o_ref[0]) * fw_ref[...]


def _conv_ffn_final(h2, w_up, conv_w, conv_b, w_down, x1, mod3, final_w):
    B = h2.shape[0]
    nf = D_FF // FFN_TF
    ni = SEQ // FFN_TM
    assert nf >= FFN_X1_STEPS
    halo_per_tile = FFN_TM // FFN_HALO
    n_halo = SEQ // FFN_HALO
    return pl.pallas_call(
        _ffn_kernel,
        grid=(B, ni, nf),
        in_specs=[
            pl.BlockSpec((1, FFN_TM, D_MODEL), lambda b, i, j: (b, i, 0),
                         pipeline_mode=pl.Buffered(1)),
            pl.BlockSpec((1, FFN_HALO, D_MODEL),
                         lambda b, i, j: (b, jnp.maximum(i * halo_per_tile - 1, 0), 0)),
            pl.BlockSpec((1, FFN_HALO, D_MODEL),
                         lambda b, i, j: (b, jnp.minimum((i + 1) * halo_per_tile, n_halo - 1), 0)),
            pl.BlockSpec((D_MODEL, FFN_TF), lambda b, i, j: (0, j)),
            pl.BlockSpec((D_MODEL, FFN_TF), lambda b, i, j: (0, nf + j)),
            pl.BlockSpec((CONV_W, FFN_TF), lambda b, i, j: (0, j)),
            pl.BlockSpec((1, FFN_TF), lambda b, i, j: (0, j)),
            pl.BlockSpec((FFN_TF, D_MODEL), lambda b, i, j: (j, 0)),
            pl.BlockSpec((1, FFN_X1_ROWS, D_MODEL),
                         lambda b, i, j: (b, i * FFN_X1_STEPS + jnp.minimum(j, FFN_X1_STEPS - 1), 0)),
            pl.BlockSpec((1, 1, D_MODEL), lambda b, i, j: (b, 0, 5)),
            pl.BlockSpec((1, D_MODEL), lambda b, i, j: (0, 0)),
        ],
        out_specs=pl.BlockSpec((1, FFN_TM, D_MODEL), lambda b, i, j: (b, i, 0)),
        out_shape=jax.ShapeDtypeStruct((B, SEQ, D_MODEL), F32),
        scratch_shapes=[pltpu.VMEM((FFN_TM + 2 * FFN_HALO, D_MODEL), BF16)],
        compiler_params=_cparams(3),
        name="conv_ffn",
    )(h2, h2, h2, w_up, w_up, conv_w, conv_b, w_down, x1, mod3, final_w)


def _rope_tables():
    pos = jnp.arange(SEQ)
    row = (pos // GRID_W).astype(F32)
    col = (pos % GRID_W).astype(F32)
    nf = ROPE_DIM // 4
    inv = ROPE_BASE ** (-jnp.arange(nf, dtype=F32) / nf)
    ar = row[:, None] * inv
    ac = col[:, None] * inv
    cos = jnp.concatenate([jnp.cos(ar), jnp.cos(ar), jnp.cos(ac), jnp.cos(ac)], axis=-1)
    sin = jnp.concatenate([-jnp.sin(ar), jnp.sin(ar), -jnp.sin(ac), jnp.sin(ac)], axis=-1)
    return jnp.tile(cos, (1, LANES // ROPE_DIM)), jnp.tile(sin, (1, LANES // ROPE_DIM))


def kernel(x, c, ctx, c_ctx, w_ada, b_ada, norm1_w, w_in, q_norm_w, kv_norm_w, w_uq, w_ukv,
           lambda_q1, lambda_k1, lambda_q2, lambda_k2, subln_w, w_o, norm2_w, w_up,
           conv_w, conv_b, w_down, final_w):
    B = x.shape[0]
    assert B <= CTX_ROW and x.shape == (B, SEQ, D_MODEL) and ctx.shape == (B, CTX_LEN, D_MODEL)
    assert w_in.shape[1:] == (D_MODEL, IN_WIDTH)
    l = 0
    c8 = jnp.concatenate([c, jnp.zeros((CTX_ROW - B, D_MODEL), F32), c_ctx[None, :],
                          jnp.zeros((MOD_ROWS - CTX_ROW - 1, D_MODEL), F32)], axis=0)
    mod = _ada(c8, w_ada[l], b_ada[l][None, :])
    mod3 = mod.reshape(MOD_ROWS, 1, N_MOD * D_MODEL)

    w_in_t = jnp.swapaxes(w_in[l], 0, 1)
    wuq_b = jnp.pad(w_uq[l].reshape(Q_RANK, MLA_HEADS, MLA_QK),
                    ((0, 0), (0, 0), (0, MLA_QK_PAD - MLA_QK))
                    ).reshape(Q_RANK, MLA_HEADS * MLA_QK_PAD).astype(BF16)
    wukv3 = w_ukv[l].reshape(KV_RANK, MLA_HEADS, MLA_NOPE + MLA_V)
    wukv_b = jnp.concatenate([wukv3[:, :, :MLA_NOPE].reshape(KV_RANK, MLA_HEADS * MLA_NOPE),
                              wukv3[:, :, MLA_NOPE:].reshape(KV_RANK, MLA_WIDTH)],
                             axis=1).astype(BF16)
    cos_t, sin_t = _rope_tables()

    q_da, k_da, v_da, q_mla, k_mla, v_mla = _mixer_inputs(
        ctx, x, mod3, norm1_w[l][None, :], cos_t, sin_t, w_in_t,
        q_norm_w[l][None, :], kv_norm_w[l][None, :], wuq_b, wukv_b)
    o_da = _diff_attn(q_da, k_da, v_da, lambda_q1[l][None, :], lambda_k1[l][None, :],
                      lambda_q2[l][None, :], lambda_k2[l][None, :], subln_w[l][None, :])
    o_mla = _mla_attn(q_mla, k_mla, v_mla)
    x1, h2 = _out_proj(o_da, o_mla, w_o[l], x, mod3, norm2_w[l][None, :])
    return _conv_ffn_final(h2, w_up[l], conv_w[l], conv_b[l][None, :],
                           w_down[l], x1, mod3, final_w[None, :])
```

```python
import functools
import math

import jax
import jax.numpy as jnp
from jax import lax
from jax.experimental import pallas as pl
from jax.experimental.pallas import tpu as pltpu

D_MODEL = 2048
SEQ = 2048
CTX_LEN = 256
GRID_W = 64
HEAD_DIM = 128
DA_HEADS = 8
DA_HALF = 64
MLA_HEADS = 8
MLA_NOPE = 128
MLA_ROPE = 64
MLA_V = 128
Q_RANK = 384
KV_RANK = 256
ROPE_DIM = 64
ROPE_BASE = 10000.0
D_FF = 5632
CONV_W = 3
N_MOD = 6
EPS = 1e-6
DA_WIDTH = DA_HEADS * HEAD_DIM
MLA_WIDTH = MLA_HEADS * MLA_V
MLA_QK = MLA_NOPE + MLA_ROPE
MLA_QK_PAD = 256
IN_WIDTH = 3 * DA_WIDTH + Q_RANK + KV_RANK + MLA_ROPE
LOG2E = math.log2(math.e)
DA_SCALE = LOG2E / math.sqrt(DA_HALF)
MLA_SCALE = LOG2E / math.sqrt(MLA_QK)
LAMBDA_INIT = 0.8 - 0.6 * math.exp(-0.3 * 0)
T_ALL = CTX_LEN + SEQ

LANES = 128
MOD_ROWS = 8
CTX_ROW = 4
VMEM_LIMIT = 60 * 1024 * 1024

TOK_TILE = 256
ADA_TN = 1024
ATT_TQ = 256
DA_STEP_HEADS = 1
MLA_STEP_HEADS = 2
OUT_TM = 512
FFN_TM = 1024
FFN_TF = 512
FFN_SUB = 512
FFN_NOUT = 512
FFN_HALO = 16
FFN_X1_STEPS = 8
FFN_X1_ROWS = FFN_TM // FFN_X1_STEPS

F32 = jnp.float32
BF16 = jnp.bfloat16


def _dot(a, b):
    return jnp.dot(a, b, preferred_element_type=F32)


def _dot_nt(a, b):
    return lax.dot_general(a, b, (((1,), (1,)), ((), ())), preferred_element_type=F32)


def _rms(x):
    return x * lax.rsqrt(jnp.mean(x * x, axis=-1, keepdims=True) + EPS)


def _cparams(n_grid):
    return pltpu.CompilerParams(dimension_semantics=("arbitrary",) * n_grid,
                                vmem_limit_bytes=VMEM_LIMIT)


def _ada_kernel(c_ref, w_ref, b_ref, o_ref):
    c = c_ref[...]
    sc = c * (1.0 / (1.0 + jnp.exp(-c)))
    o_ref[...] = _dot(sc.astype(BF16), w_ref[...].astype(BF16)) + b_ref[...]


def _ada(c8, w_ada, b_ada):
    n = w_ada.shape[1]
    return pl.pallas_call(
        _ada_kernel,
        grid=(n // ADA_TN,),
        in_specs=[pl.BlockSpec((MOD_ROWS, D_MODEL), lambda j: (0, 0)),
                  pl.BlockSpec((D_MODEL, ADA_TN), lambda j: (0, j)),
                  pl.BlockSpec((1, ADA_TN), lambda j: (0, j))],
        out_specs=pl.BlockSpec((MOD_ROWS, ADA_TN), lambda j: (0, j)),
        out_shape=jax.ShapeDtypeStruct((MOD_ROWS, n), F32),
        compiler_params=_cparams(1),
        name="ada",
    )(c8, w_ada, b_ada)


def _rope_chunk(xc, cos, sin, lo_mask):
    up = pltpu.roll(xc, LANES - 16, 1)
    dn = pltpu.roll(xc, 16, 1)
    return xc * cos + jnp.where(lo_mask, up, dn) * sin


def _mixer_kernel(ctx_ref, x_ref, sh_ref, sc_ref, n1_ref, cos_ref, sin_ref, wt_ref,
                  qn_ref, kvn_ref, wuq_ref, wukv_ref,
                  qda_ref, kda_ref, vda_ref, qm_ref, km_ref, vm_ref):
    t = pl.program_id(1)
    is_ctx = t == 0
    xt = jnp.where(is_ctx, ctx_ref[0], x_ref[0])
    h = _rms(xt) * n1_ref[...]
    h = h * (1.0 + sc_ref[0]) + sh_ref[0]
    hb = h.astype(BF16)
    cos = jnp.where(is_ctx, 1.0, cos_ref[...])
    sin = jnp.where(is_ctx, 0.0, sin_ref[...])
    lane = lax.broadcasted_iota(jnp.int32, (TOK_TILE, LANES), 1)
    lo_mask = (lane % 32) < 16
    o1, o2, o3 = DA_WIDTH, 2 * DA_WIDTH, 3 * DA_WIDTH
    o4 = o3 + Q_RANK
    o5 = o4 + KV_RANK

    k = _dot_nt(hb, wt_ref[o1:o2, :].astype(BF16))
    for hh in range(DA_HEADS):
        sl = slice(hh * LANES, (hh + 1) * LANES)
        kda_ref[0, :, sl] = _rope_chunk(k[:, sl], cos, sin, lo_mask).astype(BF16)
    vda_ref[0] = _dot_nt(hb, wt_ref[o2:o3, :].astype(BF16)).astype(BF16)

    w_low = jnp.concatenate([wt_ref[o3:IN_WIDTH, :].astype(BF16),
                             jnp.zeros((LANES - MLA_ROPE, D_MODEL), BF16)], axis=0)
    low = _dot_nt(hb, w_low)
    ckv = _rms(low[:, Q_RANK:Q_RANK + KV_RANK]) * kvn_ref[...]
    kv = _dot(ckv.astype(BF16), wukv_ref[...])
    kr = _rope_chunk(low[:, Q_RANK + KV_RANK:], cos, sin, lo_mask).astype(BF16)
    for hh in range(MLA_HEADS):
        base = hh * MLA_QK_PAD
        km_ref[0, :, base:base + LANES] = kv[:, hh * LANES:(hh + 1) * LANES].astype(BF16)
        km_ref[0, :, base + LANES:base + 2 * LANES] = kr
    vm_ref[0] = kv[:, MLA_WIDTH:].astype(BF16)

    q = _dot_nt(hb, wt_ref[0:o1, :].astype(BF16))
    for hh in range(DA_HEADS):
        sl = slice(hh * LANES, (hh + 1) * LANES)
        qda_ref[0, :, sl] = (_rope_chunk(q[:, sl], cos, sin, lo_mask) * DA_SCALE).astype(BF16)
    cq = _rms(low[:, :Q_RANK]) * qn_ref[...]
    qm = _dot(cq.astype(BF16), wuq_ref[...])
    for hh in range(MLA_HEADS):
        base = hh * MLA_QK_PAD
        qm_ref[0, :, base:base + LANES] = (qm[:, base:base + LANES] * MLA_SCALE).astype(BF16)
        qr = _rope_chunk(qm[:, base + LANES:base + 2 * LANES], cos, sin, lo_mask)
        qm_ref[0, :, base + LANES:base + 2 * LANES] = (qr * MLA_SCALE).astype(BF16)


def _mixer_inputs(ctx, x, mod3, n1, cos_t, sin_t, w_in_t, qn, kvn, wuq_b, wukv_b):
    B = x.shape[0]
    nt = T_ALL // TOK_TILE

    def lat(t):
        return jnp.maximum(t - 1, 0)

    def const(shape):
        return pl.BlockSpec(shape, lambda b, t: (0,) * len(shape), pipeline_mode=pl.Buffered(1))

    in_specs = [
        pl.BlockSpec((1, CTX_LEN, D_MODEL), lambda b, t: (b, 0, 0)),
        pl.BlockSpec((1, TOK_TILE, D_MODEL), lambda b, t: (b, lat(t), 0)),
        pl.BlockSpec((1, 1, D_MODEL), lambda b, t: (jnp.where(t == 0, CTX_ROW, b), 0, 0)),
        pl.BlockSpec((1, 1, D_MODEL), lambda b, t: (jnp.where(t == 0, CTX_ROW, b), 0, 1)),
        const((1, D_MODEL)),
        pl.BlockSpec((TOK_TILE, LANES), lambda b, t: (lat(t), 0)),
        pl.BlockSpec((TOK_TILE, LANES), lambda b, t: (lat(t), 0)),
        const(w_in_t.shape),
        const((1, Q_RANK)),
        const((1, KV_RANK)),
        const(wuq_b.shape),
        const(wukv_b.shape),
    ]
    out_specs = [
        pl.BlockSpec((1, TOK_TILE, DA_WIDTH), lambda b, t: (b, lat(t), 0)),
        pl.BlockSpec((1, TOK_TILE, DA_WIDTH), lambda b, t: (b, t, 0)),
        pl.BlockSpec((1, TOK_TILE, DA_WIDTH), lambda b, t: (b, t, 0)),
        pl.BlockSpec((1, TOK_TILE, MLA_HEADS * MLA_QK_PAD), lambda b, t: (b, lat(t), 0)),
        pl.BlockSpec((1, TOK_TILE, MLA_HEADS * MLA_QK_PAD), lambda b, t: (b, t, 0)),
        pl.BlockSpec((1, TOK_TILE, MLA_WIDTH), lambda b, t: (b, t, 0)),
    ]
    out_shape = [
        jax.ShapeDtypeStruct((B, SEQ, DA_WIDTH), BF16),
        jax.ShapeDtypeStruct((B, T_ALL, DA_WIDTH), BF16),
        jax.ShapeDtypeStruct((B, T_ALL, DA_WIDTH), BF16),
        jax.ShapeDtypeStruct((B, SEQ, MLA_HEADS * MLA_QK_PAD), BF16),
        jax.ShapeDtypeStruct((B, T_ALL, MLA_HEADS * MLA_QK_PAD), BF16),
        jax.ShapeDtypeStruct((B, T_ALL, MLA_WIDTH), BF16),
    ]
    return pl.pallas_call(
        _mixer_kernel,
        grid=(B, nt),
        in_specs=in_specs,
        out_specs=out_specs,
        out_shape=out_shape,
        compiler_params=_cparams(2),
        name="mixer_in",
    )(ctx, x, mod3, mod3, n1, cos_t, sin_t, w_in_t, qn, kvn, wuq_b, wukv_b)


def _fill_values_with_ones(v_ref, vx_ref):
    width = vx_ref.shape[-1] // 2
    ones_lane = lax.broadcasted_iota(jnp.int32, (T_ALL, width), 1) == 0
    for hh in range(vx_ref.shape[0]):
        vx_ref[hh, :, 0:width] = v_ref[0, :, hh * width:(hh + 1) * width]
        vx_ref[hh, :, width:] = jnp.where(ones_lane, 1.0, 0.0).astype(BF16)


def _softmax_weighted(s, vx):
    width = vx.shape[-1] // 2
    p = jnp.exp2(s - jnp.max(s, axis=-1, keepdims=True))
    ox = _dot(p.astype(BF16), vx)
    return ox[:, 0:width] * (1.0 / ox[:, width:width + 1])


def _pipelined_tiles(scores, finish, n_heads):
    n_tiles = SEQ // ATT_TQ
    s = [scores(0, hh) for hh in range(n_heads)]
    for n in range(n_tiles):
        s_next = [scores(n + 1, hh) for hh in range(n_heads)] if n + 1 < n_tiles else None
        for hh in range(n_heads):
            finish(n, hh, s[hh])
        s = s_next


def _diff_attn_kernel(q_ref, k_ref, v_ref, lq1_ref, lk1_ref, lq2_ref, lk2_ref, sub_ref, o_ref,
                      vx_ref):
    lam = (jnp.exp(jnp.sum(lq1_ref[...] * lk1_ref[...], axis=-1, keepdims=True))
           - jnp.exp(jnp.sum(lq2_ref[...] * lk2_ref[...], axis=-1, keepdims=True))
           + LAMBDA_INIT)
    lane = lax.broadcasted_iota(jnp.int32, (ATT_TQ, HEAD_DIM), 1)
    _fill_values_with_ones(v_ref, vx_ref)

    def scores(n, hh):
        cols = slice(hh * HEAD_DIM, (hh + 1) * HEAD_DIM)
        q = q_ref[0, n * ATT_TQ:(n + 1) * ATT_TQ, cols]
        zero = jnp.zeros_like(q)
        q1 = jnp.where(lane < DA_HALF, q, zero)
        q2 = jnp.where(lane >= DA_HALF, q, zero)
        k = k_ref[0, :, cols]
        return _dot_nt(q1, k), _dot_nt(q2, k)

    def finish(n, hh, s):
        o = _softmax_weighted(s[0], vx_ref[hh]) - lam * _softmax_weighted(s[1], vx_ref[hh])
        o = _rms(o) * sub_ref[...] * (1.0 - LAMBDA_INIT)
        o_ref[0, n * ATT_TQ:(n + 1) * ATT_TQ, hh * HEAD_DIM:(hh + 1) * HEAD_DIM] = o.astype(BF16)

    _pipelined_tiles(scores, finish, DA_STEP_HEADS)


def _diff_attn(q, k, v, lq1, lk1, lq2, lk2, subln):
    B = q.shape[0]
    small = lambda n: pl.BlockSpec((1, n), lambda b, h: (0, 0))
    width = DA_STEP_HEADS * HEAD_DIM
    return pl.pallas_call(
        _diff_attn_kernel,
        grid=(B, DA_HEADS // DA_STEP_HEADS),
        in_specs=[pl.BlockSpec((1, SEQ, width), lambda b, h: (b, 0, h)),
                  pl.BlockSpec((1, T_ALL, width), lambda b, h: (b, 0, h)),
                  pl.BlockSpec((1, T_ALL, width), lambda b, h: (b, 0, h)),
                  small(DA_HALF), small(DA_HALF), small(DA_HALF), small(DA_HALF),
                  small(HEAD_DIM)],
        out_specs=pl.BlockSpec((1, SEQ, width), lambda b, h: (b, 0, h)),
        out_shape=jax.ShapeDtypeStruct((B, SEQ, DA_WIDTH), BF16),
        scratch_shapes=[pltpu.VMEM((DA_STEP_HEADS, T_ALL, 2 * HEAD_DIM), BF16)],
        compiler_params=_cparams(2),
        name="diff_attn",
    )(q, k, v, lq1, lk1, lq2, lk2, subln)


def _mla_attn_kernel(q_ref, k_ref, v_ref, o_ref, vx_ref):
    _fill_values_with_ones(v_ref, vx_ref)

    def scores(n, hh):
        cols = slice(hh * MLA_QK_PAD, (hh + 1) * MLA_QK_PAD)
        return _dot_nt(q_ref[0, n * ATT_TQ:(n + 1) * ATT_TQ, cols], k_ref[0, :, cols])

    def finish(n, hh, s):
        o = _softmax_weighted(s, vx_ref[hh])
        o_ref[0, n * ATT_TQ:(n + 1) * ATT_TQ, hh * MLA_V:(hh + 1) * MLA_V] = o.astype(BF16)

    _pipelined_tiles(scores, finish, MLA_STEP_HEADS)


def _mla_attn(q, k, v):
    B = q.shape[0]
    return pl.pallas_call(
        _mla_attn_kernel,
        grid=(B, MLA_HEADS // MLA_STEP_HEADS),
        in_specs=[pl.BlockSpec((1, SEQ, MLA_STEP_HEADS * MLA_QK_PAD), lambda b, h: (b, 0, h)),
                  pl.BlockSpec((1, T_ALL, MLA_STEP_HEADS * MLA_QK_PAD), lambda b, h: (b, 0, h)),
                  pl.BlockSpec((1, T_ALL, MLA_STEP_HEADS * MLA_V), lambda b, h: (b, 0, h))],
        out_specs=pl.BlockSpec((1, SEQ, MLA_STEP_HEADS * MLA_V), lambda b, h: (b, 0, h)),
        out_shape=jax.ShapeDtypeStruct((B, SEQ, MLA_WIDTH), BF16),
        scratch_shapes=[pltpu.VMEM((MLA_STEP_HEADS, T_ALL, 2 * MLA_V), BF16)],
        compiler_params=_cparams(2),
        name="mla_attn",
    )(q, k, v)


def _out_proj_kernel(oda_ref, omla_ref, wo_ref, x_ref, g1_ref, sh2_ref, sc2_ref, n2_ref,
                     x1_ref, h2_ref):
    merged = jnp.concatenate([oda_ref[0], omla_ref[0]], axis=1)
    y = _dot(merged, wo_ref[...].astype(BF16))
    x1 = x_ref[0] + g1_ref[0] * y
    x1_ref[0] = x1
    h2 = _rms(x1) * n2_ref[...]
    h2_ref[0] = (h2 * (1.0 + sc2_ref[0]) + sh2_ref[0]).astype(BF16)


def _out_proj(o_da, o_mla, w_o, x, mod3, n2):
    B = x.shape[0]
    tile = lambda w: pl.BlockSpec((1, OUT_TM, w), lambda b, i: (b, i, 0))
    modrow = lambda col: pl.BlockSpec((1, 1, D_MODEL), lambda b, i: (b, 0, col))
    return pl.pallas_call(
        _out_proj_kernel,
        grid=(B, SEQ // OUT_TM),
        in_specs=[tile(DA_WIDTH), tile(MLA_WIDTH),
                  pl.BlockSpec(w_o.shape, lambda b, i: (0, 0), pipeline_mode=pl.Buffered(1)),
                  tile(D_MODEL), modrow(2), modrow(3), modrow(4),
                  pl.BlockSpec((1, D_MODEL), lambda b, i: (0, 0))],
        out_specs=[tile(D_MODEL), tile(D_MODEL)],
        out_shape=[jax.ShapeDtypeStruct((B, SEQ, D_MODEL), F32),
                   jax.ShapeDtypeStruct((B, SEQ, D_MODEL), BF16)],
        compiler_params=_cparams(2),
        name="out_proj",
    )(o_da, o_mla, w_o, x, mod3, mod3, mod3, n2)


def _ffn_kernel(h_ref, top_ref, bot_ref, wg_ref, wu_ref, cw_ref, cb_ref, wd_ref, x1_ref, g2_ref,
                fw_ref, o_ref, hs_ref):
    i = pl.program_id(1)
    j = pl.program_id(2)
    last_i = pl.num_programs(1) - 1
    last_j = pl.num_programs(2) - 1

    @pl.when(j == 0)
    def _():
        top = top_ref[0]
        bot = bot_ref[0]
        hs_ref[0:FFN_HALO, :] = jnp.where(i == 0, jnp.zeros_like(top), top)
        hs_ref[FFN_HALO:FFN_HALO + FFN_TM, :] = h_ref[0]
        hs_ref[FFN_HALO + FFN_TM:, :] = jnp.where(i == last_i, jnp.zeros_like(bot), bot)
        o_ref[0] = jnp.zeros((FFN_TM, D_MODEL), F32)

    @pl.when(j < FFN_X1_STEPS)
    def _():
        rows = pl.ds(pl.multiple_of(j * FFN_X1_ROWS, FFN_X1_ROWS), FFN_X1_ROWS)
        o_ref[0, rows, :] += x1_ref[0]

    lo = FFN_HALO
    g2 = g2_ref[0]
    for c in range(FFN_TF // FFN_SUB):
        cs = slice(c * FFN_SUB, (c + 1) * FFN_SUB)
        g = _dot(hs_ref[...], wg_ref[:, cs].astype(BF16))
        u = _dot(hs_ref[lo:lo + FFN_TM, :], wu_ref[:, cs].astype(BF16))
        g_prev = pltpu.roll(g, 1, 0)
        g_next = pltpu.roll(g, FFN_TM + 2 * FFN_HALO - 1, 0)
        gc = (g_prev[lo:lo + FFN_TM] * cw_ref[0:1, cs]
              + g[lo:lo + FFN_TM] * cw_ref[1:2, cs]
              + g_next[lo:lo + FFN_TM] * cw_ref[2:3, cs]
              + cb_ref[:, cs])
        act = (gc * (1.0 / (1.0 + jnp.exp(-gc))) * u).astype(BF16)
        for n in range(D_MODEL // FFN_NOUT):
            ns = slice(n * FFN_NOUT, (n + 1) * FFN_NOUT)
            o_ref[0, :, ns] += g2[:, ns] * _dot(act, wd_ref[cs, ns].astype(BF16))

    @pl.when(j == last_j)
    def _():
        o_ref[0] = _rms(o_ref[0]) * fw_ref[...]


def _conv_ffn_final(h2, w_up, conv_w, conv_b, w_down, x1, mod3, final_w):
    B = h2.shape[0]
    nf = D_FF // FFN_TF
    ni = SEQ // FFN_TM
    assert nf >= FFN_X1_STEPS
    halo_per_tile = FFN_TM // FFN_HALO
    n_halo = SEQ // FFN_HALO
    return pl.pallas_call(
        _ffn_kernel,
        grid=(B, ni, nf),
        in_specs=[
            pl.BlockSpec((1, FFN_TM, D_MODEL), lambda b, i, j: (b, i, 0),
                         pipeline_mode=pl.Buffered(1)),
            pl.BlockSpec((1, FFN_HALO, D_MODEL),
                         lambda b, i, j: (b, jnp.maximum(i * halo_per_tile - 1, 0), 0)),
            pl.BlockSpec((1, FFN_HALO, D_MODEL),
                         lambda b, i, j: (b, jnp.minimum((i + 1) * halo_per_tile, n_halo - 1), 0)),
            pl.BlockSpec((D_MODEL, FFN_TF), lambda b, i, j: (0, j)),
            pl.BlockSpec((D_MODEL, FFN_TF), lambda b, i, j: (0, nf + j)),
            pl.BlockSpec((CONV_W, FFN_TF), lambda b, i, j: (0, j)),
            pl.BlockSpec((1, FFN_TF), lambda b, i, j: (0, j)),
            pl.BlockSpec((FFN_TF, D_MODEL), lambda b, i, j: (j, 0)),
            pl.BlockSpec((1, FFN_X1_ROWS, D_MODEL),
                         lambda b, i, j: (b, i * FFN_X1_STEPS + jnp.minimum(j, FFN_X1_STEPS - 1), 0)),
            pl.BlockSpec((1, 1, D_MODEL), lambda b, i, j: (b, 0, 5)),
            pl.BlockSpec((1, D_MODEL), lambda b, i, j: (0, 0)),
        ],
        out_specs=pl.BlockSpec((1, FFN_TM, D_MODEL), lambda b, i, j: (b, i, 0)),
        out_shape=jax.ShapeDtypeStruct((B, SEQ, D_MODEL), F32),
        scratch_shapes=[pltpu.VMEM((FFN_TM + 2 * FFN_HALO, D_MODEL), BF16)],
        compiler_params=_cparams(3),
        name="conv_ffn",
    )(h2, h2, h2, w_up, w_up, conv_w, conv_b, w_down, x1, mod3, final_w)


def _rope_tables():
    pos = jnp.arange(SEQ)
    row = (pos // GRID_W).astype(F32)
    col = (pos % GRID_W).astype(F32)
    nf = ROPE_DIM // 4
    inv = ROPE_BASE ** (-jnp.arange(nf, dtype=F32) / nf)
    ar = row[:, None] * inv
    ac = col[:, None] * inv
    cos = jnp.concatenate([jnp.cos(ar), jnp.cos(ar), jnp.cos(ac), jnp.cos(ac)], axis=-1)
    sin = jnp.concatenate([-jnp.sin(ar), jnp.sin(ar), -jnp.sin(ac), jnp.sin(ac)], axis=-1)
    return jnp.tile(cos, (1, LANES // ROPE_DIM)), jnp.tile(sin, (1, LANES // ROPE_DIM))


def kernel(x, c, ctx, c_ctx, w_ada, b_ada, norm1_w, w_in, q_norm_w, kv_norm_w, w_uq, w_ukv,
           lambda_q1, lambda_k1, lambda_q2, lambda_k2, subln_w, w_o, norm2_w, w_up,
           conv_w, conv_b, w_down, final_w):
    B = x.shape[0]
    assert B <= CTX_ROW and x.shape == (B, SEQ, D_MODEL) and ctx.shape == (B, CTX_LEN, D_MODEL)
    assert w_in.shape[1:] == (D_MODEL, IN_WIDTH)
    l = 0
    c8 = jnp.concatenate([c, jnp.zeros((CTX_ROW - B, D_MODEL), F32), c_ctx[None, :],
                          jnp.zeros((MOD_ROWS - CTX_ROW - 1, D_MODEL), F32)], axis=0)
    mod = _ada(c8, w_ada[l], b_ada[l][None, :])
    mod3 = mod.reshape(MOD_ROWS, 1, N_MOD * D_MODEL)

    w_in_t = jnp.swapaxes(w_in[l], 0, 1)
    wuq_b = jnp.pad(w_uq[l].reshape(Q_RANK, MLA_HEADS, MLA_QK),
                    ((0, 0), (0, 0), (0, MLA_QK_PAD - MLA_QK))
                    ).reshape(Q_RANK, MLA_HEADS * MLA_QK_PAD).astype(BF16)
    wukv3 = w_ukv[l].reshape(KV_RANK, MLA_HEADS, MLA_NOPE + MLA_V)
    wukv_b = jnp.concatenate([wukv3[:, :, :MLA_NOPE].reshape(KV_RANK, MLA_HEADS * MLA_NOPE),
                              wukv3[:, :, MLA_NOPE:].reshape(KV_RANK, MLA_WIDTH)],
                             axis=1).astype(BF16)
    cos_t, sin_t = _rope_tables()

    q_da, k_da, v_da, q_mla, k_mla, v_mla = _mixer_inputs(
        ctx, x, mod3, norm1_w[l][None, :], cos_t, sin_t, w_in_t,
        q_norm_w[l][None, :], kv_norm_w[l][None, :], wuq_b, wukv_b)
    o_da = _diff_attn(q_da, k_da, v_da, lambda_q1[l][None, :], lambda_k1[l][None, :],
                      lambda_q2[l][None, :], lambda_k2[l][None, :], subln_w[l][None, :])
    o_mla = _mla_attn(q_mla, k_mla, v_mla)
    x1, h2 = _out_proj(o_da, o_mla, w_o[l], x, mod3, norm2_w[l][None, :])
    return _conv_ffn_final(h2, w_up[l], conv_w[l], conv_b[l][None, :],
                           w_down[l], x1, mod3, final_w[None, :])
```

```python
import functools
import math

import jax
import jax.numpy as jnp
from jax import lax
from jax.experimental import pallas as pl
from jax.experimental.pallas import tpu as pltpu

D_MODEL = 2048
SEQ = 2048
CTX_LEN = 256
GRID_W = 64
HEAD_DIM = 128
DA_HEADS = 8
DA_HALF = 64
MLA_HEADS = 8
MLA_NOPE = 128
MLA_ROPE = 64
MLA_V = 128
Q_RANK = 384
KV_RANK = 256
ROPE_DIM = 64
ROPE_BASE = 10000.0
D_FF = 5632
CONV_W = 3
N_MOD = 6
EPS = 1e-6
DA_WIDTH = DA_HEADS * HEAD_DIM
MLA_WIDTH = MLA_HEADS * MLA_V
MLA_QK = MLA_NOPE + MLA_ROPE
MLA_QK_PAD = 256
IN_WIDTH = 3 * DA_WIDTH + Q_RANK + KV_RANK + MLA_ROPE
LOG2E = math.log2(math.e)
DA_SCALE = LOG2E / math.sqrt(DA_HALF)
MLA_SCALE = LOG2E / math.sqrt(MLA_QK)
LAMBDA_INIT = 0.8 - 0.6 * math.exp(-0.3 * 0)
T_ALL = CTX_LEN + SEQ

LANES = 128
MOD_ROWS = 8
CTX_ROW = 4
VMEM_LIMIT = 60 * 1024 * 1024

TOK_TILE = 256
ADA_TN = 1024
ATT_TQ = 256
DA_STEP_HEADS = 1
MLA_STEP_HEADS = 2
OUT_TM = 512
FFN_TM = 1024
FFN_TF = 512
FFN_SUB = 512
FFN_NOUT = 512
FFN_HALO = 16
FFN_X1_STEPS = 8
FFN_X1_ROWS = FFN_TM // FFN_X1_STEPS

F32 = jnp.float32
BF16 = jnp.bfloat16


def _dot(a, b):
    return jnp.dot(a, b, preferred_element_type=F32)


def _dot_nt(a, b):
    return lax.dot_general(a, b, (((1,), (1,)), ((), ())), preferred_element_type=F32)


def _rms(x):
    return x * lax.rsqrt(jnp.mean(x * x, axis=-1, keepdims=True) + EPS)


def _cparams(n_grid):
    return pltpu.CompilerParams(dimension_semantics=("arbitrary",) * n_grid,
                                vmem_limit_bytes=VMEM_LIMIT)


def _ada_kernel(c_ref, w_ref, b_ref, o_ref):
    c = c_ref[...]
    sc = c * (1.0 / (1.0 + jnp.exp(-c)))
    o_ref[...] = _dot(sc.astype(BF16), w_ref[...].astype(BF16)) + b_ref[...]


def _ada(c8, w_ada, b_ada):
    n = w_ada.shape[1]
    return pl.pallas_call(
        _ada_kernel,
        grid=(n // ADA_TN,),
        in_specs=[pl.BlockSpec((MOD_ROWS, D_MODEL), lambda j: (0, 0)),
                  pl.BlockSpec((D_MODEL, ADA_TN), lambda j: (0, j)),
                  pl.BlockSpec((1, ADA_TN), lambda j: (0, j))],
        out_specs=pl.BlockSpec((MOD_ROWS, ADA_TN), lambda j: (0, j)),
        out_shape=jax.ShapeDtypeStruct((MOD_ROWS, n), F32),
        compiler_params=_cparams(1),
        name="ada",
    )(c8, w_ada, b_ada)


def _rope_chunk(xc, cos, sin, lo_mask):
    up = pltpu.roll(xc, LANES - 16, 1)
    dn = pltpu.roll(xc, 16, 1)
    return xc * cos + jnp.where(lo_mask, up, dn) * sin


def _mixer_kernel(ctx_ref, x_ref, sh_ref, sc_ref, n1_ref, cos_ref, sin_ref, wt_ref,
                  qn_ref, kvn_ref, wuq_ref, wukv_ref,
                  qda_ref, kda_ref, vda_ref, qm_ref, km_ref, vm_ref):
    t = pl.program_id(1)
    is_ctx = t == 0
    xt = jnp.where(is_ctx, ctx_ref[0], x_ref[0])
    h = _rms(xt) * n1_ref[...]
    h = h * (1.0 + sc_ref[0]) + sh_ref[0]
    hb = h.astype(BF16)
    cos = jnp.where(is_ctx, 1.0, cos_ref[...])
    sin = jnp.where(is_ctx, 0.0, sin_ref[...])
    lane = lax.broadcasted_iota(jnp.int32, (TOK_TILE, LANES), 1)
    lo_mask = (lane % 32) < 16
    o1, o2, o3 = DA_WIDTH, 2 * DA_WIDTH, 3 * DA_WIDTH
    o4 = o3 + Q_RANK
    o5 = o4 + KV_RANK

    w_low = jnp.concatenate([wt_ref[o3:IN_WIDTH, :].astype(BF16),
                             jnp.zeros((LANES - MLA_ROPE, D_MODEL), BF16)], axis=0)
    low = _dot_nt(hb, w_low)
    k = _dot_nt(hb, wt_ref[o1:o2, :].astype(BF16))
    v = _dot_nt(hb, wt_ref[o2:o3, :].astype(BF16))
    q = _dot_nt(hb, wt_ref[0:o1, :].astype(BF16))
    ckv = _rms(low[:, Q_RANK:Q_RANK + KV_RANK]) * kvn_ref[...]
    cq = _rms(low[:, :Q_RANK]) * qn_ref[...]
    kv = _dot(ckv.astype(BF16), wukv_ref[...])
    qm = _dot(cq.astype(BF16), wuq_ref[...])

    for hh in range(DA_HEADS):
        sl = slice(hh * LANES, (hh + 1) * LANES)
        kda_ref[0, :, sl] = _rope_chunk(k[:, sl], cos, sin, lo_mask).astype(BF16)
    vda_ref[0] = v.astype(BF16)
    kr = _rope_chunk(low[:, Q_RANK + KV_RANK:], cos, sin, lo_mask).astype(BF16)
    for hh in range(MLA_HEADS):
        base = hh * MLA_QK_PAD
        km_ref[0, :, base:base + LANES] = kv[:, hh * LANES:(hh + 1) * LANES].astype(BF16)
        km_ref[0, :, base + LANES:base + 2 * LANES] = kr
    vm_ref[0] = kv[:, MLA_WIDTH:].astype(BF16)

    for hh in range(DA_HEADS):
        sl = slice(hh * LANES, (hh + 1) * LANES)
        qda_ref[0, :, sl] = (_rope_chunk(q[:, sl], cos, sin, lo_mask) * DA_SCALE).astype(BF16)
    for hh in range(MLA_HEADS):
        base = hh * MLA_QK_PAD
        qm_ref[0, :, base:base + LANES] = (qm[:, base:base + LANES] * MLA_SCALE).astype(BF16)
        qr = _rope_chunk(qm[:, base + LANES:base + 2 * LANES], cos, sin, lo_mask)
        qm_ref[0, :, base + LANES:base + 2 * LANES] = (qr * MLA_SCALE).astype(BF16)


def _mixer_inputs(ctx, x, mod3, n1, cos_t, sin_t, w_in_t, qn, kvn, wuq_b, wukv_b):
    B = x.shape[0]
    nt = T_ALL // TOK_TILE

    def lat(t):
        return jnp.maximum(t - 1, 0)

    def const(shape):
        return pl.BlockSpec(shape, lambda b, t: (0,) * len(shape), pipeline_mode=pl.Buffered(1))

    in_specs = [
        pl.BlockSpec((1, CTX_LEN, D_MODEL), lambda b, t: (b, 0, 0)),
        pl.BlockSpec((1, TOK_TILE, D_MODEL), lambda b, t: (b, lat(t), 0)),
        pl.BlockSpec((1, 1, D_MODEL), lambda b, t: (jnp.where(t == 0, CTX_ROW, b), 0, 0)),
        pl.BlockSpec((1, 1, D_MODEL), lambda b, t: (jnp.where(t == 0, CTX_ROW, b), 0, 1)),
        const((1, D_MODEL)),
        pl.BlockSpec((TOK_TILE, LANES), lambda b, t: (lat(t), 0)),
        pl.BlockSpec((TOK_TILE, LANES), lambda b, t: (lat(t), 0)),
        const(w_in_t.shape),
        const((1, Q_RANK)),
        const((1, KV_RANK)),
        const(wuq_b.shape),
        const(wukv_b.shape),
    ]
    out_specs = [
        pl.BlockSpec((1, TOK_TILE, DA_WIDTH), lambda b, t: (b, lat(t), 0)),
        pl.BlockSpec((1, TOK_TILE, DA_WIDTH), lambda b, t: (b, t, 0)),
        pl.BlockSpec((1, TOK_TILE, DA_WIDTH), lambda b, t: (b, t, 0)),
        pl.BlockSpec((1, TOK_TILE, MLA_HEADS * MLA_QK_PAD), lambda b, t: (b, lat(t), 0)),
        pl.BlockSpec((1, TOK_TILE, MLA_HEADS * MLA_QK_PAD), lambda b, t: (b, t, 0)),
        pl.BlockSpec((1, TOK_TILE, MLA_WIDTH), lambda b, t: (b, t, 0)),
    ]
    out_shape = [
        jax.ShapeDtypeStruct((B, SEQ, DA_WIDTH), BF16),
        jax.ShapeDtypeStruct((B, T_ALL, DA_WIDTH), BF16),
        jax.ShapeDtypeStruct((B, T_ALL, DA_WIDTH), BF16),
        jax.ShapeDtypeStruct((B, SEQ, MLA_HEADS * MLA_QK_PAD), BF16),
        jax.ShapeDtypeStruct((B, T_ALL, MLA_HEADS * MLA_QK_PAD), BF16),
        jax.ShapeDtypeStruct((B, T_ALL, MLA_WIDTH), BF16),
    ]
    return pl.pallas_call(
        _mixer_kernel,
        grid=(B, nt),
        in_specs=in_specs,
        out_specs=out_specs,
        out_shape=out_shape,
        compiler_params=_cparams(2),
        name="mixer_in",
    )(ctx, x, mod3, mod3, n1, cos_t, sin_t, w_in_t, qn, kvn, wuq_b, wukv_b)


def _fill_values_with_ones(v_ref, vx_ref):
    width = vx_ref.shape[-1] // 2
    ones_lane = lax.broadcasted_iota(jnp.int32, (T_ALL, width), 1) == 0
    for hh in range(vx_ref.shape[0]):
        vx_ref[hh, :, 0:width] = v_ref[0, :, hh * width:(hh + 1) * width]
        vx_ref[hh, :, width:] = jnp.where(ones_lane, 1.0, 0.0).astype(BF16)


def _softmax_weighted(s, vx):
    width = vx.shape[-1] // 2
    p = jnp.exp2(s - jnp.max(s, axis=-1, keepdims=True))
    ox = _dot(p.astype(BF16), vx)
    return ox[:, 0:width] * (1.0 / ox[:, width:width + 1])


def _pipelined_tiles(scores, finish, n_heads):
    n_tiles = SEQ // ATT_TQ
    s = [scores(0, hh) for hh in range(n_heads)]
    for n in range(n_tiles):
        s_next = [scores(n + 1, hh) for hh in range(n_heads)] if n + 1 < n_tiles else None
        for hh in range(n_heads):
            finish(n, hh, s[hh])
        s = s_next


def _diff_attn_kernel(q_ref, k_ref, v_ref, lq1_ref, lk1_ref, lq2_ref, lk2_ref, sub_ref, o_ref,
                      vx_ref):
    lam = (jnp.exp(jnp.sum(lq1_ref[...] * lk1_ref[...], axis=-1, keepdims=True))
           - jnp.exp(jnp.sum(lq2_ref[...] * lk2_ref[...], axis=-1, keepdims=True))
           + LAMBDA_INIT)
    lane = lax.broadcasted_iota(jnp.int32, (ATT_TQ, HEAD_DIM), 1)
    _fill_values_with_ones(v_ref, vx_ref)

    def scores(n, hh):
        cols = slice(hh * HEAD_DIM, (hh + 1) * HEAD_DIM)
        q = q_ref[0, n * ATT_TQ:(n + 1) * ATT_TQ, cols]
        zero = jnp.zeros_like(q)
        q1 = jnp.where(lane < DA_HALF, q, zero)
        q2 = jnp.where(lane >= DA_HALF, q, zero)
        k = k_ref[0, :, cols]
        return _dot_nt(q1, k), _dot_nt(q2, k)

    def finish(n, hh, s):
        o = _softmax_weighted(s[0], vx_ref[hh]) - lam * _softmax_weighted(s[1], vx_ref[hh])
        o = _rms(o) * sub_ref[...] * (1.0 - LAMBDA_INIT)
        o_ref[0, n * ATT_TQ:(n + 1) * ATT_TQ, hh * HEAD_DIM:(hh + 1) * HEAD_DIM] = o.astype(BF16)

    _pipelined_tiles(scores, finish, DA_STEP_HEADS)


def _diff_attn(q, k, v, lq1, lk1, lq2, lk2, subln):
    B = q.shape[0]
    small = lambda n: pl.BlockSpec((1, n), lambda b, h: (0, 0))
    width = DA_STEP_HEADS * HEAD_DIM
    return pl.pallas_call(
        _diff_attn_kernel,
        grid=(B, DA_HEADS // DA_STEP_HEADS),
        in_specs=[pl.BlockSpec((1, SEQ, width), lambda b, h: (b, 0, h)),
                  pl.BlockSpec((1, T_ALL, width), lambda b, h: (b, 0, h)),
                  pl.BlockSpec((1, T_ALL, width), lambda b, h: (b, 0, h)),
                  small(DA_HALF), small(DA_HALF), small(DA_HALF), small(DA_HALF),
                  small(HEAD_DIM)],
        out_specs=pl.BlockSpec((1, SEQ, width), lambda b, h: (b, 0, h)),
        out_shape=jax.ShapeDtypeStruct((B, SEQ, DA_WIDTH), BF16),
        scratch_shapes=[pltpu.VMEM((DA_STEP_HEADS, T_ALL, 2 * HEAD_DIM), BF16)],
        compiler_params=_cparams(2),
        name="diff_attn",
    )(q, k, v, lq1, lk1, lq2, lk2, subln)


def _mla_attn_kernel(q_ref, k_ref, v_ref, o_ref, vx_ref):
    _fill_values_with_ones(v_ref, vx_ref)

    def scores(n, hh):
        cols = slice(hh * MLA_QK_PAD, (hh + 1) * MLA_QK_PAD)
        return _dot_nt(q_ref[0, n * ATT_TQ:(n + 1) * ATT_TQ, cols], k_ref[0, :, cols])

    def finish(n, hh, s):
        o = _softmax_weighted(s, vx_ref[hh])
        o_ref[0, n * ATT_TQ:(n + 1) * ATT_TQ, hh * MLA_V:(hh + 1) * MLA_V] = o.astype(BF16)

    _pipelined_tiles(scores, finish, MLA_STEP_HEADS)


def _mla_attn(q, k, v):
    B = q.shape[0]
    return pl.pallas_call(
        _mla_attn_kernel,
        grid=(B, MLA_HEADS // MLA_STEP_HEADS),
        in_specs=[pl.BlockSpec((1, SEQ, MLA_STEP_HEADS * MLA_QK_PAD), lambda b, h: (b, 0, h)),
                  pl.BlockSpec((1, T_ALL, MLA_STEP_HEADS * MLA_QK_PAD), lambda b, h: (b, 0, h)),
                  pl.BlockSpec((1, T_ALL, MLA_STEP_HEADS * MLA_V), lambda b, h: (b, 0, h))],
        out_specs=pl.BlockSpec((1, SEQ, MLA_STEP_HEADS * MLA_V), lambda b, h: (b, 0, h)),
        out_shape=jax.ShapeDtypeStruct((B, SEQ, MLA_WIDTH), BF16),
        scratch_shapes=[pltpu.VMEM((MLA_STEP_HEADS, T_ALL, 2 * MLA_V), BF16)],
        compiler_params=_cparams(2),
        name="mla_attn",
    )(q, k, v)


def _out_proj_kernel(oda_ref, omla_ref, wo_ref, x_ref, g1_ref, sh2_ref, sc2_ref, n2_ref,
                     x1_ref, h2_ref):
    merged = jnp.concatenate([oda_ref[0], omla_ref[0]], axis=1)
    y = _dot(merged, wo_ref[...].astype(BF16))
    x1 = x_ref[0] + g1_ref[0] * y
    x1_ref[0] = x1
    h2 = _rms(x1) * n2_ref[...]
    h2_ref[0] = (h2 * (1.0 + sc2_ref[0]) + sh2_ref[0]).astype(BF16)


def _out_proj(o_da, o_mla, w_o, x, mod3, n2):
    B = x.shape[0]
    tile = lambda w: pl.BlockSpec((1, OUT_TM, w), lambda b, i: (b, i, 0))
    modrow = lambda col: pl.BlockSpec((1, 1, D_MODEL), lambda b, i: (b, 0, col))
    return pl.pallas_call(
        _out_proj_kernel,
        grid=(B, SEQ // OUT_TM),
        in_specs=[tile(DA_WIDTH), tile(MLA_WIDTH),
                  pl.BlockSpec(w_o.shape, lambda b, i: (0, 0), pipeline_mode=pl.Buffered(1)),
                  tile(D_MODEL), modrow(2), modrow(3), modrow(4),
                  pl.BlockSpec((1, D_MODEL), lambda b, i: (0, 0))],
        out_specs=[tile(D_MODEL), tile(D_MODEL)],
        out_shape=[jax.ShapeDtypeStruct((B, SEQ, D_MODEL), F32),
                   jax.ShapeDtypeStruct((B, SEQ, D_MODEL), BF16)],
        compiler_params=_cparams(2),
        name="out_proj",
    )(o_da, o_mla, w_o, x, mod3, mod3, mod3, n2)


def _ffn_kernel(h_ref, top_ref, bot_ref, wg_ref, wu_ref, cw_ref, cb_ref, wd_ref, x1_ref, g2_ref,
                fw_ref, o_ref, hs_ref):
    i = pl.program_id(1)
    j = pl.program_id(2)
    last_i = pl.num_programs(1) - 1
    last_j = pl.num_programs(2) - 1

    @pl.when(j == 0)
    def _():
        top = top_ref[0]
        bot = bot_ref[0]
        hs_ref[0:FFN_HALO, :] = jnp.where(i == 0, jnp.zeros_like(top), top)
        hs_ref[FFN_HALO:FFN_HALO + FFN_TM, :] = h_ref[0]
        hs_ref[FFN_HALO + FFN_TM:, :] = jnp.where(i == last_i, jnp.zeros_like(bot), bot)
        o_ref[0] = jnp.zeros((FFN_TM, D_MODEL), F32)

    @pl.when(j < FFN_X1_STEPS)
    def _():
        rows = pl.ds(pl.multiple_of(j * FFN_X1_ROWS, FFN_X1_ROWS), FFN_X1_ROWS)
        o_ref[0, rows, :] += x1_ref[0]

    lo = FFN_HALO
    g2 = g2_ref[0]
    for c in range(FFN_TF // FFN_SUB):
        cs = slice(c * FFN_SUB, (c + 1) * FFN_SUB)
        g = _dot(hs_ref[...], wg_ref[:, cs].astype(BF16))
        u = _dot(hs_ref[lo:lo + FFN_TM, :], wu_ref[:, cs].astype(BF16))
        g_prev = pltpu.roll(g, 1, 0)
        g_next = pltpu.roll(g, FFN_TM + 2 * FFN_HALO - 1, 0)
        gc = (g_prev[lo:lo + FFN_TM] * cw_ref[0:1, cs]
              + g[lo:lo + FFN_TM] * cw_ref[1:2, cs]
              + g_next[lo:lo + FFN_TM] * cw_ref[2:3, cs]
              + cb_ref[:, cs])
        act = (gc * (1.0 / (1.0 + jnp.exp(-gc))) * u).astype(BF16)
        for n in range(D_MODEL // FFN_NOUT):
            ns = slice(n * FFN_NOUT, (n + 1) * FFN_NOUT)
            o_ref[0, :, ns] += g2[:, ns] * _dot(act, wd_ref[cs, ns].astype(BF16))

    @pl.when(j == last_j)
    def _():
        o_ref[0] = _rms(o_ref[0]) * fw_ref[...]


def _conv_ffn_final(h2, w_up, conv_w, conv_b, w_down, x1, mod3, final_w):
    B = h2.shape[0]
    nf = D_FF // FFN_TF
    ni = SEQ // FFN_TM
    assert nf >= FFN_X1_STEPS
    halo_per_tile = FFN_TM // FFN_HALO
    n_halo = SEQ // FFN_HALO
    return pl.pallas_call(
        _ffn_kernel,
        grid=(B, ni, nf),
        in_specs=[
            pl.BlockSpec((1, FFN_TM, D_MODEL), lambda b, i, j: (b, i, 0),
                         pipeline_mode=pl.Buffered(1)),
            pl.BlockSpec((1, FFN_HALO, D_MODEL),
                         lambda b, i, j: (b, jnp.maximum(i * halo_per_tile - 1, 0), 0)),
            pl.BlockSpec((1, FFN_HALO, D_MODEL),
                         lambda b, i, j: (b, jnp.minimum((i + 1) * halo_per_tile, n_halo - 1), 0)),
            pl.BlockSpec((D_MODEL, FFN_TF), lambda b, i, j: (0, j)),
            pl.BlockSpec((D_MODEL, FFN_TF), lambda b, i, j: (0, nf + j)),
            pl.BlockSpec((CONV_W, FFN_TF), lambda b, i, j: (0, j)),
            pl.BlockSpec((1, FFN_TF), lambda b, i, j: (0, j)),
            pl.BlockSpec((FFN_TF, D_MODEL), lambda b, i, j: (j, 0)),
            pl.BlockSpec((1, FFN_X1_ROWS, D_MODEL),
                         lambda b, i, j: (b, i * FFN_X1_STEPS + jnp.minimum(j, FFN_X1_STEPS - 1), 0)),
            pl.BlockSpec((1, 1, D_MODEL), lambda b, i, j: (b, 0, 5)),
            pl.BlockSpec((1, D_MODEL), lambda b, i, j: (0, 0)),
        ],
        out_specs=pl.BlockSpec((1, FFN_TM, D_MODEL), lambda b, i, j: (b, i, 0)),
        out_shape=jax.ShapeDtypeStruct((B, SEQ, D_MODEL), F32),
        scratch_shapes=[pltpu.VMEM((FFN_TM + 2 * FFN_HALO, D_MODEL), BF16)],
        compiler_params=_cparams(3),
        name="conv_ffn",
    )(h2, h2, h2, w_up, w_up, conv_w, conv_b, w_down, x1, mod3, final_w)


def _rope_tables():
    pos = jnp.arange(SEQ)
    row = (pos // GRID_W).astype(F32)
    col = (pos % GRID_W).astype(F32)
    nf = ROPE_DIM // 4
    inv = ROPE_BASE ** (-jnp.arange(nf, dtype=F32) / nf)
    ar = row[:, None] * inv
    ac = col[:, None] * inv
    cos = jnp.concatenate([jnp.cos(ar), jnp.cos(ar), jnp.cos(ac), jnp.cos(ac)], axis=-1)
    sin = jnp.concatenate([-jnp.sin(ar), jnp.sin(ar), -jnp.sin(ac), jnp.sin(ac)], axis=-1)
    return jnp.tile(cos, (1, LANES // ROPE_DIM)), jnp.tile(sin, (1, LANES // ROPE_DIM))


def kernel(x, c, ctx, c_ctx, w_ada, b_ada, norm1_w, w_in, q_norm_w, kv_norm_w, w_uq, w_ukv,
           lambda_q1, lambda_k1, lambda_q2, lambda_k2, subln_w, w_o, norm2_w, w_up,
           conv_w, conv_b, w_down, final_w):
    B = x.shape[0]
    assert B <= CTX_ROW and x.shape == (B, SEQ, D_MODEL) and ctx.shape == (B, CTX_LEN, D_MODEL)
    assert w_in.shape[1:] == (D_MODEL, IN_WIDTH)
    l = 0
    c8 = jnp.concatenate([c, jnp.zeros((CTX_ROW - B, D_MODEL), F32), c_ctx[None, :],
                          jnp.zeros((MOD_ROWS - CTX_ROW - 1, D_MODEL), F32)], axis=0)
    mod = _ada(c8, w_ada[l], b_ada[l][None, :])
    mod3 = mod.reshape(MOD_ROWS, 1, N_MOD * D_MODEL)

    w_in_t = jnp.swapaxes(w_in[l], 0, 1)
    wuq_b = jnp.pad(w_uq[l].reshape(Q_RANK, MLA_HEADS, MLA_QK),
                    ((0, 0), (0, 0), (0, MLA_QK_PAD - MLA_QK))
                    ).reshape(Q_RANK, MLA_HEADS * MLA_QK_PAD).astype(BF16)
    wukv3 = w_ukv[l].reshape(KV_RANK, MLA_HEADS, MLA_NOPE + MLA_V)
    wukv_b = jnp.concatenate([wukv3[:, :, :MLA_NOPE].reshape(KV_RANK, MLA_HEADS * MLA_NOPE),
                              wukv3[:, :, MLA_NOPE:].reshape(KV_RANK, MLA_WIDTH)],
                             axis=1).astype(BF16)
    cos_t, sin_t = _rope_tables()

    q_da, k_da, v_da, q_mla, k_mla, v_mla = _mixer_inputs(
        ctx, x, mod3, norm1_w[l][None, :], cos_t, sin_t, w_in_t,
        q_norm_w[l][None, :], kv_norm_w[l][None, :], wuq_b, wukv_b)
    o_da = _diff_attn(q_da, k_da, v_da, lambda_q1[l][None, :], lambda_k1[l][None, :],
                      lambda_q2[l][None, :], lambda_k2[l][None, :], subln_w[l][None, :])
    o_mla = _mla_attn(q_mla, k_mla, v_mla)
    x1, h2 = _out_proj(o_da, o_mla, w_o[l], x, mod3, norm2_w[l][None, :])
    return _conv_ffn_final(h2, w_up[l], conv_w[l], conv_b[l][None, :],
                           w_down[l], x1, mod3, final_w[None, :])
```

```python
import math

import jax
import jax.numpy as jnp
from jax import lax
from jax.experimental import pallas as pl
from jax.experimental.pallas import tpu as pltpu

D_MODEL = 2048
SEQ = 2048
CTX_LEN = 256
GRID_W = 64
HEAD_DIM = 128
DA_HEADS = 8
DA_HALF = 64
MLA_HEADS = 8
MLA_NOPE = 128
MLA_ROPE = 64
MLA_V = 128
Q_RANK = 384
KV_RANK = 256
ROPE_DIM = 64
ROPE_BASE = 10000.0
D_FF = 5632
CONV_W = 3
N_MOD = 6
EPS = 1e-6
DA_WIDTH = DA_HEADS * HEAD_DIM
MLA_WIDTH = MLA_HEADS * MLA_V
MLA_QK = MLA_NOPE + MLA_ROPE
MLA_QK_PAD = 256
IN_WIDTH = 3 * DA_WIDTH + Q_RANK + KV_RANK + MLA_ROPE
LOG2E = math.log2(math.e)
DA_SCALE = LOG2E / math.sqrt(DA_HALF)
MLA_SCALE = LOG2E / math.sqrt(MLA_QK)
LAMBDA_INIT = 0.8 - 0.6 * math.exp(-0.3 * 0)
T_ALL = CTX_LEN + SEQ

LANES = 128
MOD_ROWS = 8
CTX_ROW = 4
VMEM_LIMIT = 60 * 1024 * 1024

TOK_TILE = 256
ADA_TN = 1024
ATT_TQ = 256
DA_STEP_HEADS = 1
MLA_STEP_HEADS = 2
OUT_TM = 512
FFN_TM = 1024
FFN_TF = 512
FFN_SUB = 512
FFN_NOUT = 512
FFN_HALO = 16
FFN_X1_STEPS = 8
FFN_X1_ROWS = FFN_TM // FFN_X1_STEPS

F32 = jnp.float32
BF16 = jnp.bfloat16


def _dot(a, b):
    return jnp.dot(a, b, preferred_element_type=F32)


def _dot_nt(a, b):
    return lax.dot_general(a, b, (((1,), (1,)), ((), ())), preferred_element_type=F32)


def _rms(x):
    return x * lax.rsqrt(jnp.mean(x * x, axis=-1, keepdims=True) + EPS)


def _cparams(n_grid):
    return pltpu.CompilerParams(dimension_semantics=("arbitrary",) * n_grid,
                                vmem_limit_bytes=VMEM_LIMIT)


def _ada_kernel(c_ref, w_ref, b_ref, o_ref):
    c = c_ref[...]
    sc = c * (1.0 / (1.0 + jnp.exp(-c)))
    o_ref[...] = _dot(sc.astype(BF16), w_ref[...].astype(BF16)) + b_ref[...]


def _ada(c8, w_ada, b_ada):
    n = w_ada.shape[1]
    return pl.pallas_call(
        _ada_kernel,
        grid=(n // ADA_TN,),
        in_specs=[pl.BlockSpec((MOD_ROWS, D_MODEL), lambda j: (0, 0)),
                  pl.BlockSpec((D_MODEL, ADA_TN), lambda j: (0, j)),
                  pl.BlockSpec((1, ADA_TN), lambda j: (0, j))],
        out_specs=pl.BlockSpec((MOD_ROWS, ADA_TN), lambda j: (0, j)),
        out_shape=jax.ShapeDtypeStruct((MOD_ROWS, n), F32),
        compiler_params=_cparams(1),
        name="ada",
    )(c8, w_ada, b_ada)


def _rope_chunk(xc, cos, sin, lo_mask):
    up = pltpu.roll(xc, LANES - 16, 1)
    dn = pltpu.roll(xc, 16, 1)
    return xc * cos + jnp.where(lo_mask, up, dn) * sin


def _mixer_kernel(ctx_ref, x_ref, sh_ref, sc_ref, n1_ref, cos_ref, sin_ref, wt_ref,
                  qn_ref, kvn_ref, wuq_ref, wukv_ref,
                  qda_ref, kda_ref, vda_ref, qm_ref, km_ref, vm_ref):
    t = pl.program_id(1)
    is_ctx = t == 0
    xt = jnp.where(is_ctx, ctx_ref[0], x_ref[0])
    h = _rms(xt) * n1_ref[...]
    h = h * (1.0 + sc_ref[0]) + sh_ref[0]
    hb = h.astype(BF16)
    cos = jnp.where(is_ctx, 1.0, cos_ref[...])
    sin = jnp.where(is_ctx, 0.0, sin_ref[...])
    lane = lax.broadcasted_iota(jnp.int32, (TOK_TILE, LANES), 1)
    lo_mask = (lane % 32) < 16
    o1, o2, o3 = DA_WIDTH, 2 * DA_WIDTH, 3 * DA_WIDTH
    o4 = o3 + Q_RANK
    o5 = o4 + KV_RANK

    w_low = jnp.concatenate([wt_ref[o3:IN_WIDTH, :].astype(BF16),
                             jnp.zeros((LANES - MLA_ROPE, D_MODEL), BF16)], axis=0)
    low = _dot_nt(hb, w_low)
    k = _dot_nt(hb, wt_ref[o1:o2, :].astype(BF16))
    v = _dot_nt(hb, wt_ref[o2:o3, :].astype(BF16))
    q = _dot_nt(hb, wt_ref[0:o1, :].astype(BF16))
    ckv = _rms(low[:, Q_RANK:Q_RANK + KV_RANK]) * kvn_ref[...]
    cq = _rms(low[:, :Q_RANK]) * qn_ref[...]
    kv = _dot(ckv.astype(BF16), wukv_ref[...])
    qm = _dot(cq.astype(BF16), wuq_ref[...])

    for hh in range(DA_HEADS):
        sl = slice(hh * LANES, (hh + 1) * LANES)
        kda_ref[0, :, sl] = _rope_chunk(k[:, sl], cos, sin, lo_mask).astype(BF16)
    vda_ref[0] = v.astype(BF16)
    kr = _rope_chunk(low[:, Q_RANK + KV_RANK:], cos, sin, lo_mask).astype(BF16)
    for hh in range(MLA_HEADS):
        base = hh * MLA_QK_PAD
        km_ref[0, :, base:base + LANES] = kv[:, hh * LANES:(hh + 1) * LANES].astype(BF16)
        km_ref[0, :, base + LANES:base + 2 * LANES] = kr
    vm_ref[0] = kv[:, MLA_WIDTH:].astype(BF16)

    for hh in range(DA_HEADS):
        sl = slice(hh * LANES, (hh + 1) * LANES)
        qda_ref[0, :, sl] = (_rope_chunk(q[:, sl], cos, sin, lo_mask) * DA_SCALE).astype(BF16)
    for hh in range(MLA_HEADS):
        base = hh * MLA_QK_PAD
        qm_ref[0, :, base:base + LANES] = (qm[:, base:base + LANES] * MLA_SCALE).astype(BF16)
        qr = _rope_chunk(qm[:, base + LANES:base + 2 * LANES], cos, sin, lo_mask)
        qm_ref[0, :, base + LANES:base + 2 * LANES] = (qr * MLA_SCALE).astype(BF16)


def _mixer_inputs(ctx, x, mod3, n1, cos_t, sin_t, w_in_t, qn, kvn, wuq_b, wukv_b):
    B = x.shape[0]
    nt = T_ALL // TOK_TILE

    def lat(t):
        return jnp.maximum(t - 1, 0)

    def const(shape):
        return pl.BlockSpec(shape, lambda b, t: (0,) * len(shape), pipeline_mode=pl.Buffered(1))

    in_specs = [
        pl.BlockSpec((1, CTX_LEN, D_MODEL), lambda b, t: (b, 0, 0)),
        pl.BlockSpec((1, TOK_TILE, D_MODEL), lambda b, t: (b, lat(t), 0)),
        pl.BlockSpec((1, 1, D_MODEL), lambda b, t: (jnp.where(t == 0, CTX_ROW, b), 0, 0)),
        pl.BlockSpec((1, 1, D_MODEL), lambda b, t: (jnp.where(t == 0, CTX_ROW, b), 0, 1)),
        const((1, D_MODEL)),
        pl.BlockSpec((TOK_TILE, LANES), lambda b, t: (lat(t), 0)),
        pl.BlockSpec((TOK_TILE, LANES), lambda b, t: (lat(t), 0)),
        const(w_in_t.shape),
        const((1, Q_RANK)),
        const((1, KV_RANK)),
        const(wuq_b.shape),
        const(wukv_b.shape),
    ]
    out_specs = [
        pl.BlockSpec((1, TOK_TILE, DA_WIDTH), lambda b, t: (b, lat(t), 0)),
        pl.BlockSpec((1, TOK_TILE, DA_WIDTH), lambda b, t: (b, t, 0)),
        pl.BlockSpec((1, TOK_TILE, DA_WIDTH), lambda b, t: (b, t, 0)),
        pl.BlockSpec((1, TOK_TILE, MLA_HEADS * MLA_QK_PAD), lambda b, t: (b, lat(t), 0)),
        pl.BlockSpec((1, TOK_TILE, MLA_HEADS * MLA_QK_PAD), lambda b, t: (b, t, 0)),
        pl.BlockSpec((1, TOK_TILE, MLA_WIDTH), lambda b, t: (b, t, 0)),
    ]
    out_shape = [
        jax.ShapeDtypeStruct((B, SEQ, DA_WIDTH), BF16),
        jax.ShapeDtypeStruct((B, T_ALL, DA_WIDTH), BF16),
        jax.ShapeDtypeStruct((B, T_ALL, DA_WIDTH), BF16),
        jax.ShapeDtypeStruct((B, SEQ, MLA_HEADS * MLA_QK_PAD), BF16),
        jax.ShapeDtypeStruct((B, T_ALL, MLA_HEADS * MLA_QK_PAD), BF16),
        jax.ShapeDtypeStruct((B, T_ALL, MLA_WIDTH), BF16),
    ]
    return pl.pallas_call(
        _mixer_kernel,
        grid=(B, nt),
        in_specs=in_specs,
        out_specs=out_specs,
        out_shape=out_shape,
        compiler_params=_cparams(2),
        name="mixer_in",
    )(ctx, x, mod3, mod3, n1, cos_t, sin_t, w_in_t, qn, kvn, wuq_b, wukv_b)


def _fill_values_with_ones(v_ref, vx_ref):
    width = vx_ref.shape[-1] // 2
    ones_lane = lax.broadcasted_iota(jnp.int32, (T_ALL, width), 1) == 0
    for hh in range(vx_ref.shape[0]):
        vx_ref[hh, :, 0:width] = v_ref[0, :, hh * width:(hh + 1) * width]
        vx_ref[hh, :, width:] = jnp.where(ones_lane, 1.0, 0.0).astype(BF16)


def _softmax_weighted(s, vx):
    width = vx.shape[-1] // 2
    p = jnp.exp2(s - jnp.max(s, axis=-1, keepdims=True))
    ox = _dot(p.astype(BF16), vx)
    return ox[:, 0:width] * (1.0 / ox[:, width:width + 1])


def _pipelined_tiles(scores, finish, n_heads):
    n_tiles = SEQ // ATT_TQ
    s = [scores(0, hh) for hh in range(n_heads)]
    for n in range(n_tiles):
        s_next = [scores(n + 1, hh) for hh in range(n_heads)] if n + 1 < n_tiles else None
        for hh in range(n_heads):
            finish(n, hh, s[hh])
        s = s_next


def _diff_attn_kernel(q_ref, k_ref, v_ref, lq1_ref, lk1_ref, lq2_ref, lk2_ref, sub_ref, o_ref,
                      vx_ref):
    lam = (jnp.exp(jnp.sum(lq1_ref[...] * lk1_ref[...], axis=-1, keepdims=True))
           - jnp.exp(jnp.sum(lq2_ref[...] * lk2_ref[...], axis=-1, keepdims=True))
           + LAMBDA_INIT)
    lane = lax.broadcasted_iota(jnp.int32, (ATT_TQ, HEAD_DIM), 1)
    _fill_values_with_ones(v_ref, vx_ref)

    def scores(n, hh):
        cols = slice(hh * HEAD_DIM, (hh + 1) * HEAD_DIM)
        q = q_ref[0, n * ATT_TQ:(n + 1) * ATT_TQ, cols]
        zero = jnp.zeros_like(q)
        q1 = jnp.where(lane < DA_HALF, q, zero)
        q2 = jnp.where(lane >= DA_HALF, q, zero)
        k = k_ref[0, :, cols]
        return _dot_nt(q1, k), _dot_nt(q2, k)

    def finish(n, hh, s):
        o = _softmax_weighted(s[0], vx_ref[hh]) - lam * _softmax_weighted(s[1], vx_ref[hh])
        o = _rms(o) * sub_ref[...] * (1.0 - LAMBDA_INIT)
        o_ref[0, n * ATT_TQ:(n + 1) * ATT_TQ, hh * HEAD_DIM:(hh + 1) * HEAD_DIM] = o.astype(BF16)

    _pipelined_tiles(scores, finish, DA_STEP_HEADS)


def _diff_attn(q, k, v, lq1, lk1, lq2, lk2, subln):
    B = q.shape[0]
    small = lambda n: pl.BlockSpec((1, n), lambda b, h: (0, 0))
    width = DA_STEP_HEADS * HEAD_DIM
    return pl.pallas_call(
        _diff_attn_kernel,
        grid=(B, DA_HEADS // DA_STEP_HEADS),
        in_specs=[pl.BlockSpec((1, SEQ, width), lambda b, h: (b, 0, h)),
                  pl.BlockSpec((1, T_ALL, width), lambda b, h: (b, 0, h)),
                  pl.BlockSpec((1, T_ALL, width), lambda b, h: (b, 0, h)),
                  small(DA_HALF), small(DA_HALF), small(DA_HALF), small(DA_HALF),
                  small(HEAD_DIM)],
        out_specs=pl.BlockSpec((1, SEQ, width), lambda b, h: (b, 0, h)),
        out_shape=jax.ShapeDtypeStruct((B, SEQ, DA_WIDTH), BF16),
        scratch_shapes=[pltpu.VMEM((DA_STEP_HEADS, T_ALL, 2 * HEAD_DIM), BF16)],
        compiler_params=_cparams(2),
        name="diff_attn",
    )(q, k, v, lq1, lk1, lq2, lk2, subln)


def _mla_attn_kernel(q_ref, k_ref, v_ref, o_ref, vx_ref):
    _fill_values_with_ones(v_ref, vx_ref)

    def scores(n, hh):
        cols = slice(hh * MLA_QK_PAD, (hh + 1) * MLA_QK_PAD)
        return _dot_nt(q_ref[0, n * ATT_TQ:(n + 1) * ATT_TQ, cols], k_ref[0, :, cols])

    def finish(n, hh, s):
        o = _softmax_weighted(s, vx_ref[hh])
        o_ref[0, n * ATT_TQ:(n + 1) * ATT_TQ, hh * MLA_V:(hh + 1) * MLA_V] = o.astype(BF16)

    _pipelined_tiles(scores, finish, MLA_STEP_HEADS)


def _mla_attn(q, k, v):
    B = q.shape[0]
    return pl.pallas_call(
        _mla_attn_kernel,
        grid=(B, MLA_HEADS // MLA_STEP_HEADS),
        in_specs=[pl.BlockSpec((1, SEQ, MLA_STEP_HEADS * MLA_QK_PAD), lambda b, h: (b, 0, h)),
                  pl.BlockSpec((1, T_ALL, MLA_STEP_HEADS * MLA_QK_PAD), lambda b, h: (b, 0, h)),
                  pl.BlockSpec((1, T_ALL, MLA_STEP_HEADS * MLA_V), lambda b, h: (b, 0, h))],
        out_specs=pl.BlockSpec((1, SEQ, MLA_STEP_HEADS * MLA_V), lambda b, h: (b, 0, h)),
        out_shape=jax.ShapeDtypeStruct((B, SEQ, MLA_WIDTH), BF16),
        scratch_shapes=[pltpu.VMEM((MLA_STEP_HEADS, T_ALL, 2 * MLA_V), BF16)],
        compiler_params=_cparams(2),
        name="mla_attn",
    )(q, k, v)


def _out_proj_kernel(oda_ref, omla_ref, wo_ref, x_ref, g1_ref, sh2_ref, sc2_ref, n2_ref,
                     x1_ref, h2_ref):
    merged = jnp.concatenate([oda_ref[0], omla_ref[0]], axis=1)
    y = _dot(merged, wo_ref[...].astype(BF16))
    x1 = x_ref[0] + g1_ref[0] * y
    x1_ref[0] = x1
    h2 = _rms(x1) * n2_ref[...]
    h2_ref[0] = (h2 * (1.0 + sc2_ref[0]) + sh2_ref[0]).astype(BF16)


def _out_proj(o_da, o_mla, w_o, x, mod3, n2):
    B = x.shape[0]
    tile = lambda w: pl.BlockSpec((1, OUT_TM, w), lambda b, i: (b, i, 0))
    modrow = lambda col: pl.BlockSpec((1, 1, D_MODEL), lambda b, i: (b, 0, col))
    return pl.pallas_call(
        _out_proj_kernel,
        grid=(B, SEQ // OUT_TM),
        in_specs=[tile(DA_WIDTH), tile(MLA_WIDTH),
                  pl.BlockSpec(w_o.shape, lambda b, i: (0, 0), pipeline_mode=pl.Buffered(1)),
                  tile(D_MODEL), modrow(2), modrow(3), modrow(4),
                  pl.BlockSpec((1, D_MODEL), lambda b, i: (0, 0))],
        out_specs=[tile(D_MODEL), tile(D_MODEL)],
        out_shape=[jax.ShapeDtypeStruct((B, SEQ, D_MODEL), F32),
                   jax.ShapeDtypeStruct((B, SEQ, D_MODEL), BF16)],
        compiler_params=_cparams(2),
        name="out_proj",
    )(o_da, o_mla, w_o, x, mod3, mod3, mod3, n2)


def _ffn_kernel(h_ref, top_ref, bot_ref, wg_ref, wu_ref, cw_ref, cb_ref, wd_ref, x1_ref, g2_ref,
                fw_ref, o_ref, hs_ref):
    i = pl.program_id(1)
    j = pl.program_id(2)
    last_i = pl.num_programs(1) - 1
    last_j = pl.num_programs(2) - 1

    @pl.when(j == 0)
    def _():
        top = top_ref[0]
        bot = bot_ref[0]
        hs_ref[0:FFN_HALO, :] = jnp.where(i == 0, jnp.zeros_like(top), top)
        hs_ref[FFN_HALO:FFN_HALO + FFN_TM, :] = h_ref[0]
        hs_ref[FFN_HALO + FFN_TM:, :] = jnp.where(i == last_i, jnp.zeros_like(bot), bot)
        o_ref[0] = jnp.zeros((FFN_TM, D_MODEL), F32)

    @pl.when(j < FFN_X1_STEPS)
    def _():
        rows = pl.ds(pl.multiple_of(j * FFN_X1_ROWS, FFN_X1_ROWS), FFN_X1_ROWS)
        o_ref[0, rows, :] += x1_ref[0]

    lo = FFN_HALO
    g2 = g2_ref[0]
    for c in range(FFN_TF // FFN_SUB):
        cs = slice(c * FFN_SUB, (c + 1) * FFN_SUB)
        g = _dot(hs_ref[...], wg_ref[:, cs].astype(BF16))
        u = _dot(hs_ref[lo:lo + FFN_TM, :], wu_ref[:, cs].astype(BF16))
        g_prev = pltpu.roll(g, 1, 0)
        g_next = pltpu.roll(g, FFN_TM + 2 * FFN_HALO - 1, 0)
        gc = (g_prev[lo:lo + FFN_TM] * cw_ref[0:1, cs]
              + g[lo:lo + FFN_TM] * cw_ref[1:2, cs]
              + g_next[lo:lo + FFN_TM] * cw_ref[2:3, cs]
              + cb_ref[:, cs])
        act = (gc * (1.0 / (1.0 + jnp.exp(-gc))) * u).astype(BF16)
        for n in range(D_MODEL // FFN_NOUT):
            ns = slice(n * FFN_NOUT, (n + 1) * FFN_NOUT)
            o_ref[0, :, ns] += g2[:, ns] * _dot(act, wd_ref[cs, ns].astype(BF16))

    @pl.when(j == last_j)
    def _():
        o_ref[0] = _rms(o_ref[0]) * fw_ref[...]


def _conv_ffn_final(h2, w_up, conv_w, conv_b, w_down, x1, mod3, final_w):
    B = h2.shape[0]
    nf = D_FF // FFN_TF
    ni = SEQ // FFN_TM
    assert nf >= FFN_X1_STEPS
    halo_per_tile = FFN_TM // FFN_HALO
    n_halo = SEQ // FFN_HALO
    return pl.pallas_call(
        _ffn_kernel,
        grid=(B, ni, nf),
        in_specs=[
            pl.BlockSpec((1, FFN_TM, D_MODEL), lambda b, i, j: (b, i, 0),
                         pipeline_mode=pl.Buffered(1)),
            pl.BlockSpec((1, FFN_HALO, D_MODEL),
                         lambda b, i, j: (b, jnp.maximum(i * halo_per_tile - 1, 0), 0)),
            pl.BlockSpec((1, FFN_HALO, D_MODEL),
                         lambda b, i, j: (b, jnp.minimum((i + 1) * halo_per_tile, n_halo - 1), 0)),
            pl.BlockSpec((D_MODEL, FFN_TF), lambda b, i, j: (0, j)),
            pl.BlockSpec((D_MODEL, FFN_TF), lambda b, i, j: (0, nf + j)),
            pl.BlockSpec((CONV_W, FFN_TF), lambda b, i, j: (0, j)),
            pl.BlockSpec((1, FFN_TF), lambda b, i, j: (0, j)),
            pl.BlockSpec((FFN_TF, D_MODEL), lambda b, i, j: (j, 0)),
            pl.BlockSpec((1, FFN_X1_ROWS, D_MODEL),
                         lambda b, i, j: (b, i * FFN_X1_STEPS + jnp.minimum(j, FFN_X1_STEPS - 1), 0)),
            pl.BlockSpec((1, 1, D_MODEL), lambda b, i, j: (b, 0, 5)),
            pl.BlockSpec((1, D_MODEL), lambda b, i, j: (0, 0)),
        ],
        out_specs=pl.BlockSpec((1, FFN_TM, D_MODEL), lambda b, i, j: (b, i, 0)),
        out_shape=jax.ShapeDtypeStruct((B, SEQ, D_MODEL), F32),
        scratch_shapes=[pltpu.VMEM((FFN_TM + 2 * FFN_HALO, D_MODEL), BF16)],
        compiler_params=_cparams(3),
        name="conv_ffn",
    )(h2, h2, h2, w_up, w_up, conv_w, conv_b, w_down, x1, mod3, final_w)


def _rope_tables():
    pos = jnp.arange(SEQ)
    row = (pos // GRID_W).astype(F32)
    col = (pos % GRID_W).astype(F32)
    nf = ROPE_DIM // 4
    inv = ROPE_BASE ** (-jnp.arange(nf, dtype=F32) / nf)
    ar = row[:, None] * inv
    ac = col[:, None] * inv
    cos = jnp.concatenate([jnp.cos(ar), jnp.cos(ar), jnp.cos(ac), jnp.cos(ac)], axis=-1)
    sin = jnp.concatenate([-jnp.sin(ar), jnp.sin(ar), -jnp.sin(ac), jnp.sin(ac)], axis=-1)
    return jnp.tile(cos, (1, LANES // ROPE_DIM)), jnp.tile(sin, (1, LANES // ROPE_DIM))


def kernel(x, c, ctx, c_ctx, w_ada, b_ada, norm1_w, w_in, q_norm_w, kv_norm_w, w_uq, w_ukv,
           lambda_q1, lambda_k1, lambda_q2, lambda_k2, subln_w, w_o, norm2_w, w_up,
           conv_w, conv_b, w_down, final_w):
    B = x.shape[0]
    assert B <= CTX_ROW and x.shape == (B, SEQ, D_MODEL) and ctx.shape == (B, CTX_LEN, D_MODEL)
    assert w_in.shape[1:] == (D_MODEL, IN_WIDTH)
    l = 0
    c8 = jnp.concatenate([c, jnp.zeros((CTX_ROW - B, D_MODEL), F32), c_ctx[None, :],
                          jnp.zeros((MOD_ROWS - CTX_ROW - 1, D_MODEL), F32)], axis=0)
    mod = _ada(c8, w_ada[l], b_ada[l][None, :])
    mod3 = mod.reshape(MOD_ROWS, 1, N_MOD * D_MODEL)

    w_in_t = jnp.swapaxes(w_in[l], 0, 1)
    wuq_b = jnp.pad(w_uq[l].reshape(Q_RANK, MLA_HEADS, MLA_QK),
                    ((0, 0), (0, 0), (0, MLA_QK_PAD - MLA_QK))
                    ).reshape(Q_RANK, MLA_HEADS * MLA_QK_PAD).astype(BF16)
    wukv3 = w_ukv[l].reshape(KV_RANK, MLA_HEADS, MLA_NOPE + MLA_V)
    wukv_b = jnp.concatenate([wukv3[:, :, :MLA_NOPE].reshape(KV_RANK, MLA_HEADS * MLA_NOPE),
                              wukv3[:, :, MLA_NOPE:].reshape(KV_RANK, MLA_WIDTH)],
                             axis=1).astype(BF16)
    cos_t, sin_t = _rope_tables()

    q_da, k_da, v_da, q_mla, k_mla, v_mla = _mixer_inputs(
        ctx, x, mod3, norm1_w[l][None, :], cos_t, sin_t, w_in_t,
        q_norm_w[l][None, :], kv_norm_w[l][None, :], wuq_b, wukv_b)
    o_da = _diff_attn(q_da, k_da, v_da, lambda_q1[l][None, :], lambda_k1[l][None, :],
                      lambda_q2[l][None, :], lambda_k2[l][None, :], subln_w[l][None, :])
    o_mla = _mla_attn(q_mla, k_mla, v_mla)
    x1, h2 = _out_proj(o_da, o_mla, w_o[l], x, mod3, norm2_w[l][None, :])
    return _conv_ffn_final(h2, w_up[l], conv_w[l], conv_b[l][None, :],
                           w_down[l], x1, mod3, final_w[None, :])
```

```python
import math

import jax
import jax.numpy as jnp
from jax import lax
from jax.experimental import pallas as pl
from jax.experimental.pallas import tpu as pltpu

D_MODEL = 2048
SEQ = 2048
CTX_LEN = 256
GRID_W = 64
HEAD_DIM = 128
DA_HEADS = 8
DA_HALF = 64
MLA_HEADS = 8
MLA_NOPE = 128
MLA_ROPE = 64
MLA_V = 128
Q_RANK = 384
KV_RANK = 256
ROPE_DIM = 64
ROPE_BASE = 10000.0
D_FF = 5632
CONV_W = 3
N_MOD = 6
EPS = 1e-6
DA_WIDTH = DA_HEADS * HEAD_DIM
MLA_WIDTH = MLA_HEADS * MLA_V
MLA_QK = MLA_NOPE + MLA_ROPE
MLA_QK_PAD = 256
IN_WIDTH = 3 * DA_WIDTH + Q_RANK + KV_RANK + MLA_ROPE
LOG2E = math.log2(math.e)
DA_SCALE = LOG2E / math.sqrt(DA_HALF)
MLA_SCALE = LOG2E / math.sqrt(MLA_QK)
LAMBDA_INIT = 0.8 - 0.6 * math.exp(-0.3 * 0)
T_ALL = CTX_LEN + SEQ

LANES = 128
MOD_ROWS = 8
CTX_ROW = 4
VMEM_LIMIT = 60 * 1024 * 1024

TOK_TILE = 256
MOD_EARLY = 2 * D_MODEL
MOD_LATE_TN = 256
ADA_TN = 1024
ATT_TQ = 256
DA_STEP_HEADS = 1
MLA_STEP_HEADS = 2
OUT_TM = 512
FFN_TM = 1024
FFN_TF = 512
FFN_SUB = 512
FFN_NOUT = 512
FFN_HALO = 16
FFN_X1_STEPS = 8
FFN_X1_ROWS = FFN_TM // FFN_X1_STEPS

F32 = jnp.float32
BF16 = jnp.bfloat16


def _dot(a, b):
    return jnp.dot(a, b, preferred_element_type=F32)


def _dot_nt(a, b):
    return lax.dot_general(a, b, (((1,), (1,)), ((), ())), preferred_element_type=F32)


def _rms(x):
    return x * lax.rsqrt(jnp.mean(x * x, axis=-1, keepdims=True) + EPS)


def _cparams(n_grid):
    return pltpu.CompilerParams(dimension_semantics=("arbitrary",) * n_grid,
                                vmem_limit_bytes=VMEM_LIMIT)


def _ada_kernel(c_ref, w_ref, b_ref, o_ref):
    c = c_ref[...]
    sc = c * (1.0 / (1.0 + jnp.exp(-c)))
    o_ref[...] = _dot(sc.astype(BF16), w_ref[...].astype(BF16)) + b_ref[...]


def _ada(c8, w_ada, b_ada, n):
    return pl.pallas_call(
        _ada_kernel,
        grid=(n // ADA_TN,),
        in_specs=[pl.BlockSpec((MOD_ROWS, D_MODEL), lambda j: (0, 0)),
                  pl.BlockSpec((D_MODEL, ADA_TN), lambda j: (0, j)),
                  pl.BlockSpec((1, ADA_TN), lambda j: (0, j))],
        out_specs=pl.BlockSpec((MOD_ROWS, ADA_TN), lambda j: (0, j)),
        out_shape=jax.ShapeDtypeStruct((MOD_ROWS, n), F32),
        compiler_params=_cparams(1),
        name="ada",
    )(c8, w_ada, b_ada)


def _rope_chunk(xc, cos, sin, lo_mask):
    up = pltpu.roll(xc, LANES - 16, 1)
    dn = pltpu.roll(xc, 16, 1)
    return xc * cos + jnp.where(lo_mask, up, dn) * sin


def _mixer_kernel(ctx_ref, x_ref, sh_ref, sc_ref, n1_ref, cos_ref, sin_ref, wt_ref,
                  qn_ref, kvn_ref, wuq_ref, wukv_ref, c8_ref, wa_ref, ba_ref,
                  qda_ref, kda_ref, vda_ref, qm_ref, km_ref, vm_ref, modl_ref):
    t = pl.program_id(1)
    is_ctx = t == 0
    c8 = c8_ref[...]
    sc8 = (c8 * (1.0 / (1.0 + jnp.exp(-c8)))).astype(BF16)
    mod_late = _dot(sc8, wa_ref[...].astype(BF16)) + ba_ref[...]
    xt = jnp.where(is_ctx, ctx_ref[0], x_ref[0])
    h = _rms(xt) * n1_ref[...]
    h = h * (1.0 + sc_ref[0]) + sh_ref[0]
    hb = h.astype(BF16)
    cos = jnp.where(is_ctx, 1.0, cos_ref[...])
    sin = jnp.where(is_ctx, 0.0, sin_ref[...])
    lane = lax.broadcasted_iota(jnp.int32, (TOK_TILE, LANES), 1)
    lo_mask = (lane % 32) < 16
    o1, o2, o3 = DA_WIDTH, 2 * DA_WIDTH, 3 * DA_WIDTH
    o4 = o3 + Q_RANK
    o5 = o4 + KV_RANK

    w_low = jnp.concatenate([wt_ref[o3:IN_WIDTH, :].astype(BF16),
                             jnp.zeros((LANES - MLA_ROPE, D_MODEL), BF16)], axis=0)
    low = _dot_nt(hb, w_low)
    k = _dot_nt(hb, wt_ref[o1:o2, :].astype(BF16))
    v = _dot_nt(hb, wt_ref[o2:o3, :].astype(BF16))
    q = _dot_nt(hb, wt_ref[0:o1, :].astype(BF16))
    ckv = _rms(low[:, Q_RANK:Q_RANK + KV_RANK]) * kvn_ref[...]
    cq = _rms(low[:, :Q_RANK]) * qn_ref[...]
    kv = _dot(ckv.astype(BF16), wukv_ref[...])
    qm = _dot(cq.astype(BF16), wuq_ref[...])

    for hh in range(DA_HEADS):
        sl = slice(hh * LANES, (hh + 1) * LANES)
        kda_ref[0, :, sl] = _rope_chunk(k[:, sl], cos, sin, lo_mask).astype(BF16)
    vda_ref[0] = v.astype(BF16)
    kr = _rope_chunk(low[:, Q_RANK + KV_RANK:], cos, sin, lo_mask).astype(BF16)
    for hh in range(MLA_HEADS):
        base = hh * MLA_QK_PAD
        km_ref[0, :, base:base + LANES] = kv[:, hh * LANES:(hh + 1) * LANES].astype(BF16)
        km_ref[0, :, base + LANES:base + 2 * LANES] = kr
    vm_ref[0] = kv[:, MLA_WIDTH:].astype(BF16)

    for hh in range(DA_HEADS):
        sl = slice(hh * LANES, (hh + 1) * LANES)
        qda_ref[0, :, sl] = (_rope_chunk(q[:, sl], cos, sin, lo_mask) * DA_SCALE).astype(BF16)
    for hh in range(MLA_HEADS):
        base = hh * MLA_QK_PAD
        qm_ref[0, :, base:base + LANES] = (qm[:, base:base + LANES] * MLA_SCALE).astype(BF16)
        qr = _rope_chunk(qm[:, base + LANES:base + 2 * LANES], cos, sin, lo_mask)
        qm_ref[0, :, base + LANES:base + 2 * LANES] = (qr * MLA_SCALE).astype(BF16)
    modl_ref[...] = mod_late


def _mixer_inputs(ctx, x, mod3, n1, cos_t, sin_t, w_in_t, qn, kvn, wuq_b, wukv_b, c8, w_ada, b_ada):
    B = x.shape[0]
    nt = T_ALL // TOK_TILE
    n_late = w_ada.shape[1] - MOD_EARLY
    assert n_late == B * (nt - 1) * MOD_LATE_TN

    def lat(t):
        return jnp.maximum(t - 1, 0)

    def late(b, t):
        return b * (nt - 1) + lat(t)

    def const(shape):
        return pl.BlockSpec(shape, lambda b, t: (0,) * len(shape), pipeline_mode=pl.Buffered(1))

    in_specs = [
        pl.BlockSpec((1, CTX_LEN, D_MODEL), lambda b, t: (b, 0, 0)),
        pl.BlockSpec((1, TOK_TILE, D_MODEL), lambda b, t: (b, lat(t), 0)),
        pl.BlockSpec((1, 1, D_MODEL), lambda b, t: (jnp.where(t == 0, CTX_ROW, b), 0, 0)),
        pl.BlockSpec((1, 1, D_MODEL), lambda b, t: (jnp.where(t == 0, CTX_ROW, b), 0, 1)),
        const((1, D_MODEL)),
        pl.BlockSpec((TOK_TILE, LANES), lambda b, t: (lat(t), 0)),
        pl.BlockSpec((TOK_TILE, LANES), lambda b, t: (lat(t), 0)),
        const(w_in_t.shape),
        const((1, Q_RANK)),
        const((1, KV_RANK)),
        const(wuq_b.shape),
        const(wukv_b.shape),
        const((MOD_ROWS, D_MODEL)),
        pl.BlockSpec((D_MODEL, MOD_LATE_TN), lambda b, t: (0, MOD_EARLY // MOD_LATE_TN + late(b, t))),
        pl.BlockSpec((1, MOD_LATE_TN), lambda b, t: (0, MOD_EARLY // MOD_LATE_TN + late(b, t))),
    ]
    out_specs = [
        pl.BlockSpec((1, TOK_TILE, DA_WIDTH), lambda b, t: (b, lat(t), 0)),
        pl.BlockSpec((1, TOK_TILE, DA_WIDTH), lambda b, t: (b, t, 0)),
        pl.BlockSpec((1, TOK_TILE, DA_WIDTH), lambda b, t: (b, t, 0)),
        pl.BlockSpec((1, TOK_TILE, MLA_HEADS * MLA_QK_PAD), lambda b, t: (b, lat(t), 0)),
        pl.BlockSpec((1, TOK_TILE, MLA_HEADS * MLA_QK_PAD), lambda b, t: (b, t, 0)),
        pl.BlockSpec((1, TOK_TILE, MLA_WIDTH), lambda b, t: (b, t, 0)),
        pl.BlockSpec((MOD_ROWS, MOD_LATE_TN), lambda b, t: (0, late(b, t))),
    ]
    out_shape = [
        jax.ShapeDtypeStruct((B, SEQ, DA_WIDTH), BF16),
        jax.ShapeDtypeStruct((B, T_ALL, DA_WIDTH), BF16),
        jax.ShapeDtypeStruct((B, T_ALL, DA_WIDTH), BF16),
        jax.ShapeDtypeStruct((B, SEQ, MLA_HEADS * MLA_QK_PAD), BF16),
        jax.ShapeDtypeStruct((B, T_ALL, MLA_HEADS * MLA_QK_PAD), BF16),
        jax.ShapeDtypeStruct((B, T_ALL, MLA_WIDTH), BF16),
        jax.ShapeDtypeStruct((MOD_ROWS, n_late), F32),
    ]
    return pl.pallas_call(
        _mixer_kernel,
        grid=(B, nt),
        in_specs=in_specs,
        out_specs=out_specs,
        out_shape=out_shape,
        compiler_params=_cparams(2),
        name="mixer_in",
    )(ctx, x, mod3, mod3, n1, cos_t, sin_t, w_in_t, qn, kvn, wuq_b, wukv_b, c8, w_ada, b_ada)


def _fill_values_with_ones(v_ref, vx_ref):
    width = vx_ref.shape[-1] // 2
    ones_lane = lax.broadcasted_iota(jnp.int32, (T_ALL, width), 1) == 0
    for hh in range(vx_ref.shape[0]):
        vx_ref[hh, :, 0:width] = v_ref[0, :, hh * width:(hh + 1) * width]
        vx_ref[hh, :, width:] = jnp.where(ones_lane, 1.0, 0.0).astype(BF16)


def _softmax_weighted(s, vx):
    width = vx.shape[-1] // 2
    p = jnp.exp2(s - jnp.max(s, axis=-1, keepdims=True))
    ox = _dot(p.astype(BF16), vx)
    return ox[:, 0:width] * (1.0 / ox[:, width:width + 1])


def _pipelined_tiles(scores, finish, n_heads):
    n_tiles = SEQ // ATT_TQ
    s = [scores(0, hh) for hh in range(n_heads)]
    for n in range(n_tiles):
        s_next = [scores(n + 1, hh) for hh in range(n_heads)] if n + 1 < n_tiles else None
        for hh in range(n_heads):
            finish(n, hh, s[hh])
        s = s_next


def _diff_attn_kernel(q_ref, k_ref, v_ref, lq1_ref, lk1_ref, lq2_ref, lk2_ref, sub_ref, o_ref,
                      vx_ref):
    lam = (jnp.exp(jnp.sum(lq1_ref[...] * lk1_ref[...], axis=-1, keepdims=True))
           - jnp.exp(jnp.sum(lq2_ref[...] * lk2_ref[...], axis=-1, keepdims=True))
           + LAMBDA_INIT)
    lane = lax.broadcasted_iota(jnp.int32, (ATT_TQ, HEAD_DIM), 1)
    _fill_values_with_ones(v_ref, vx_ref)

    def scores(n, hh):
        cols = slice(hh * HEAD_DIM, (hh + 1) * HEAD_DIM)
        q = q_ref[0, n * ATT_TQ:(n + 1) * ATT_TQ, cols]
        zero = jnp.zeros_like(q)
        q1 = jnp.where(lane < DA_HALF, q, zero)
        q2 = jnp.where(lane >= DA_HALF, q, zero)
        k = k_ref[0, :, cols]
        return _dot_nt(q1, k), _dot_nt(q2, k)

    def finish(n, hh, s):
        o = _softmax_weighted(s[0], vx_ref[hh]) - lam * _softmax_weighted(s[1], vx_ref[hh])
        o = _rms(o) * sub_ref[...] * (1.0 - LAMBDA_INIT)
        o_ref[0, n * ATT_TQ:(n + 1) * ATT_TQ, hh * HEAD_DIM:(hh + 1) * HEAD_DIM] = o.astype(BF16)

    _pipelined_tiles(scores, finish, DA_STEP_HEADS)


def _diff_attn(q, k, v, lq1, lk1, lq2, lk2, subln):
    B = q.shape[0]
    small = lambda n: pl.BlockSpec((1, n), lambda b, h: (0, 0))
    width = DA_STEP_HEADS * HEAD_DIM
    return pl.pallas_call(
        _diff_attn_kernel,
        grid=(B, DA_HEADS // DA_STEP_HEADS),
        in_specs=[pl.BlockSpec((1, SEQ, width), lambda b, h: (b, 0, h)),
                  pl.BlockSpec((1, T_ALL, width), lambda b, h: (b, 0, h)),
                  pl.BlockSpec((1, T_ALL, width), lambda b, h: (b, 0, h)),
                  small(DA_HALF), small(DA_HALF), small(DA_HALF), small(DA_HALF),
                  small(HEAD_DIM)],
        out_specs=pl.BlockSpec((1, SEQ, width), lambda b, h: (b, 0, h)),
        out_shape=jax.ShapeDtypeStruct((B, SEQ, DA_WIDTH), BF16),
        scratch_shapes=[pltpu.VMEM((DA_STEP_HEADS, T_ALL, 2 * HEAD_DIM), BF16)],
        compiler_params=_cparams(2),
        name="diff_attn",
    )(q, k, v, lq1, lk1, lq2, lk2, subln)


def _mla_attn_kernel(q_ref, k_ref, v_ref, o_ref, vx_ref):
    _fill_values_with_ones(v_ref, vx_ref)

    def scores(n, hh):
        cols = slice(hh * MLA_QK_PAD, (hh + 1) * MLA_QK_PAD)
        return _dot_nt(q_ref[0, n * ATT_TQ:(n + 1) * ATT_TQ, cols], k_ref[0, :, cols])

    def finish(n, hh, s):
        o = _softmax_weighted(s, vx_ref[hh])
        o_ref[0, n * ATT_TQ:(n + 1) * ATT_TQ, hh * MLA_V:(hh + 1) * MLA_V] = o.astype(BF16)

    _pipelined_tiles(scores, finish, MLA_STEP_HEADS)


def _mla_attn(q, k, v):
    B = q.shape[0]
    return pl.pallas_call(
        _mla_attn_kernel,
        grid=(B, MLA_HEADS // MLA_STEP_HEADS),
        in_specs=[pl.BlockSpec((1, SEQ, MLA_STEP_HEADS * MLA_QK_PAD), lambda b, h: (b, 0, h)),
                  pl.BlockSpec((1, T_ALL, MLA_STEP_HEADS * MLA_QK_PAD), lambda b, h: (b, 0, h)),
                  pl.BlockSpec((1, T_ALL, MLA_STEP_HEADS * MLA_V), lambda b, h: (b, 0, h))],
        out_specs=pl.BlockSpec((1, SEQ, MLA_STEP_HEADS * MLA_V), lambda b, h: (b, 0, h)),
        out_shape=jax.ShapeDtypeStruct((B, SEQ, MLA_WIDTH), BF16),
        scratch_shapes=[pltpu.VMEM((MLA_STEP_HEADS, T_ALL, 2 * MLA_V), BF16)],
        compiler_params=_cparams(2),
        name="mla_attn",
    )(q, k, v)


def _out_proj_kernel(oda_ref, omla_ref, wo_ref, x_ref, g1_ref, sh2_ref, sc2_ref, n2_ref,
                     x1_ref, h2_ref):
    merged = jnp.concatenate([oda_ref[0], omla_ref[0]], axis=1)
    y = _dot(merged, wo_ref[...].astype(BF16))
    x1 = x_ref[0] + g1_ref[0] * y
    x1_ref[0] = x1
    h2 = _rms(x1) * n2_ref[...]
    h2_ref[0] = (h2 * (1.0 + sc2_ref[0]) + sh2_ref[0]).astype(BF16)


def _out_proj(o_da, o_mla, w_o, x, mod3, n2):
    B = x.shape[0]
    tile = lambda w: pl.BlockSpec((1, OUT_TM, w), lambda b, i: (b, i, 0))
    modrow = lambda col: pl.BlockSpec((1, 1, D_MODEL), lambda b, i: (b, 0, col))
    return pl.pallas_call(
        _out_proj_kernel,
        grid=(B, SEQ // OUT_TM),
        in_specs=[tile(DA_WIDTH), tile(MLA_WIDTH),
                  pl.BlockSpec(w_o.shape, lambda b, i: (0, 0), pipeline_mode=pl.Buffered(1)),
                  tile(D_MODEL), modrow(0), modrow(1), modrow(2),
                  pl.BlockSpec((1, D_MODEL), lambda b, i: (0, 0))],
        out_specs=[tile(D_MODEL), tile(D_MODEL)],
        out_shape=[jax.ShapeDtypeStruct((B, SEQ, D_MODEL), F32),
                   jax.ShapeDtypeStruct((B, SEQ, D_MODEL), BF16)],
        compiler_params=_cparams(2),
        name="out_proj",
    )(o_da, o_mla, w_o, x, mod3, mod3, mod3, n2)


def _ffn_kernel(h_ref, top_ref, bot_ref, wg_ref, wu_ref, cw_ref, cb_ref, wd_ref, x1_ref, g2_ref,
                fw_ref, o_ref, hs_ref):
    i = pl.program_id(1)
    j = pl.program_id(2)
    last_i = pl.num_programs(1) - 1
    last_j = pl.num_programs(2) - 1

    @pl.when(j == 0)
    def _():
        top = top_ref[0]
        bot = bot_ref[0]
        hs_ref[0:FFN_HALO, :] = jnp.where(i == 0, jnp.zeros_like(top), top)
        hs_ref[FFN_HALO:FFN_HALO + FFN_TM, :] = h_ref[0]
        hs_ref[FFN_HALO + FFN_TM:, :] = jnp.where(i == last_i, jnp.zeros_like(bot), bot)
        o_ref[0] = jnp.zeros((FFN_TM, D_MODEL), F32)

    @pl.when(j < FFN_X1_STEPS)
    def _():
        rows = pl.ds(pl.multiple_of(j * FFN_X1_ROWS, FFN_X1_ROWS), FFN_X1_ROWS)
        o_ref[0, rows, :] += x1_ref[0]

    lo = FFN_HALO
    g2 = g2_ref[0]
    for c in range(FFN_TF // FFN_SUB):
        cs = slice(c * FFN_SUB, (c + 1) * FFN_SUB)
        g = _dot(hs_ref[...], wg_ref[:, cs].astype(BF16))
        u = _dot(hs_ref[lo:lo + FFN_TM, :], wu_ref[:, cs].astype(BF16))
        g_prev = pltpu.roll(g, 1, 0)
        g_next = pltpu.roll(g, FFN_TM + 2 * FFN_HALO - 1, 0)
        gc = (g_prev[lo:lo + FFN_TM] * cw_ref[0:1, cs]
              + g[lo:lo + FFN_TM] * cw_ref[1:2, cs]
              + g_next[lo:lo + FFN_TM] * cw_ref[2:3, cs]
              + cb_ref[:, cs])
        act = (gc * (1.0 / (1.0 + jnp.exp(-gc))) * u).astype(BF16)
        for n in range(D_MODEL // FFN_NOUT):
            ns = slice(n * FFN_NOUT, (n + 1) * FFN_NOUT)
            o_ref[0, :, ns] += g2[:, ns] * _dot(act, wd_ref[cs, ns].astype(BF16))

    @pl.when(j == last_j)
    def _():
        o_ref[0] = _rms(o_ref[0]) * fw_ref[...]


def _conv_ffn_final(h2, w_up, conv_w, conv_b, w_down, x1, mod3, final_w):
    B = h2.shape[0]
    nf = D_FF // FFN_TF
    ni = SEQ // FFN_TM
    assert nf >= FFN_X1_STEPS
    halo_per_tile = FFN_TM // FFN_HALO
    n_halo = SEQ // FFN_HALO
    return pl.pallas_call(
        _ffn_kernel,
        grid=(B, ni, nf),
        in_specs=[
            pl.BlockSpec((1, FFN_TM, D_MODEL), lambda b, i, j: (b, i, 0),
                         pipeline_mode=pl.Buffered(1)),
            pl.BlockSpec((1, FFN_HALO, D_MODEL),
                         lambda b, i, j: (b, jnp.maximum(i * halo_per_tile - 1, 0), 0)),
            pl.BlockSpec((1, FFN_HALO, D_MODEL),
                         lambda b, i, j: (b, jnp.minimum((i + 1) * halo_per_tile, n_halo - 1), 0)),
            pl.BlockSpec((D_MODEL, FFN_TF), lambda b, i, j: (0, j)),
            pl.BlockSpec((D_MODEL, FFN_TF), lambda b, i, j: (0, nf + j)),
            pl.BlockSpec((CONV_W, FFN_TF), lambda b, i, j: (0, j)),
            pl.BlockSpec((1, FFN_TF), lambda b, i, j: (0, j)),
            pl.BlockSpec((FFN_TF, D_MODEL), lambda b, i, j: (j, 0)),
            pl.BlockSpec((1, FFN_X1_ROWS, D_MODEL),
                         lambda b, i, j: (b, i * FFN_X1_STEPS + jnp.minimum(j, FFN_X1_STEPS - 1), 0)),
            pl.BlockSpec((1, 1, D_MODEL), lambda b, i, j: (b, 0, 3)),
            pl.BlockSpec((1, D_MODEL), lambda b, i, j: (0, 0)),
        ],
        out_specs=pl.BlockSpec((1, FFN_TM, D_MODEL), lambda b, i, j: (b, i, 0)),
        out_shape=jax.ShapeDtypeStruct((B, SEQ, D_MODEL), F32),
        scratch_shapes=[pltpu.VMEM((FFN_TM + 2 * FFN_HALO, D_MODEL), BF16)],
        compiler_params=_cparams(3),
        name="conv_ffn",
    )(h2, h2, h2, w_up, w_up, conv_w, conv_b, w_down, x1, mod3, final_w)


def _rope_tables():
    pos = jnp.arange(SEQ)
    row = (pos // GRID_W).astype(F32)
    col = (pos % GRID_W).astype(F32)
    nf = ROPE_DIM // 4
    inv = ROPE_BASE ** (-jnp.arange(nf, dtype=F32) / nf)
    ar = row[:, None] * inv
    ac = col[:, None] * inv
    cos = jnp.concatenate([jnp.cos(ar), jnp.cos(ar), jnp.cos(ac), jnp.cos(ac)], axis=-1)
    sin = jnp.concatenate([-jnp.sin(ar), jnp.sin(ar), -jnp.sin(ac), jnp.sin(ac)], axis=-1)
    return jnp.tile(cos, (1, LANES // ROPE_DIM)), jnp.tile(sin, (1, LANES // ROPE_DIM))


def kernel(x, c, ctx, c_ctx, w_ada, b_ada, norm1_w, w_in, q_norm_w, kv_norm_w, w_uq, w_ukv,
           lambda_q1, lambda_k1, lambda_q2, lambda_k2, subln_w, w_o, norm2_w, w_up,
           conv_w, conv_b, w_down, final_w):
    B = x.shape[0]
    assert B <= CTX_ROW and x.shape == (B, SEQ, D_MODEL) and ctx.shape == (B, CTX_LEN, D_MODEL)
    assert w_in.shape[1:] == (D_MODEL, IN_WIDTH)
    l = 0
    c8 = jnp.concatenate([c, jnp.zeros((CTX_ROW - B, D_MODEL), F32), c_ctx[None, :],
                          jnp.zeros((MOD_ROWS - CTX_ROW - 1, D_MODEL), F32)], axis=0)
    mod_early = _ada(c8, w_ada[l], b_ada[l][None, :], MOD_EARLY)
    mod3 = mod_early.reshape(MOD_ROWS, 1, MOD_EARLY)

    w_in_t = jnp.swapaxes(w_in[l], 0, 1)
    wuq_b = jnp.pad(w_uq[l].reshape(Q_RANK, MLA_HEADS, MLA_QK),
                    ((0, 0), (0, 0), (0, MLA_QK_PAD - MLA_QK))
                    ).reshape(Q_RANK, MLA_HEADS * MLA_QK_PAD).astype(BF16)
    wukv3 = w_ukv[l].reshape(KV_RANK, MLA_HEADS, MLA_NOPE + MLA_V)
    wukv_b = jnp.concatenate([wukv3[:, :, :MLA_NOPE].reshape(KV_RANK, MLA_HEADS * MLA_NOPE),
                              wukv3[:, :, MLA_NOPE:].reshape(KV_RANK, MLA_WIDTH)],
                             axis=1).astype(BF16)
    cos_t, sin_t = _rope_tables()

    q_da, k_da, v_da, q_mla, k_mla, v_mla, mod_late = _mixer_inputs(
        ctx, x, mod3, norm1_w[l][None, :], cos_t, sin_t, w_in_t,
        q_norm_w[l][None, :], kv_norm_w[l][None, :], wuq_b, wukv_b, c8, w_ada[l], b_ada[l][None, :])
    mod3_late = mod_late.reshape(MOD_ROWS, 1, N_MOD * D_MODEL - MOD_EARLY)
    o_da = _diff_attn(q_da, k_da, v_da, lambda_q1[l][None, :], lambda_k1[l][None, :],
                      lambda_q2[l][None, :], lambda_k2[l][None, :], subln_w[l][None, :])
    o_mla = _mla_attn(q_mla, k_mla, v_mla)
    x1, h2 = _out_proj(o_da, o_mla, w_o[l], x, mod3_late, norm2_w[l][None, :])
    return _conv_ffn_final(h2, w_up[l], conv_w[l], conv_b[l][None, :],
                           w_down[l], x1, mod3_late, final_w[None, :])
```

```python
import math

import jax
import jax.numpy as jnp
from jax import lax
from jax.experimental import pallas as pl
from jax.experimental.pallas import tpu as pltpu

D_MODEL = 2048
SEQ = 2048
CTX_LEN = 256
GRID_W = 64
HEAD_DIM = 128
DA_HEADS = 8
DA_HALF = 64
MLA_HEADS = 8
MLA_NOPE = 128
MLA_ROPE = 64
MLA_V = 128
Q_RANK = 384
KV_RANK = 256
ROPE_DIM = 64
ROPE_BASE = 10000.0
D_FF = 5632
CONV_W = 3
N_MOD = 6
EPS = 1e-6
DA_WIDTH = DA_HEADS * HEAD_DIM
MLA_WIDTH = MLA_HEADS * MLA_V
MLA_QK = MLA_NOPE + MLA_ROPE
MLA_QK_PAD = 256
IN_WIDTH = 3 * DA_WIDTH + Q_RANK + KV_RANK + MLA_ROPE
LOG2E = math.log2(math.e)
DA_SCALE = LOG2E / math.sqrt(DA_HALF)
MLA_SCALE = LOG2E / math.sqrt(MLA_QK)
LAMBDA_INIT = 0.8 - 0.6 * math.exp(-0.3 * 0)
T_ALL = CTX_LEN + SEQ

LANES = 128
MOD_ROWS = 8
CTX_ROW = 4
VMEM_LIMIT = 60 * 1024 * 1024

TOK_TILE = 256
MOD_EARLY = 2 * D_MODEL
MOD_LATE_TN = 256
ADA_TN = 1024
ATT_TQ = 256
DA_LOOKAHEAD = 2
MLA_LOOKAHEAD = 1
DA_STEP_HEADS = 1
MLA_STEP_HEADS = 2
OUT_TM = 512
FFN_TM = 1024
FFN_TF = 512
FFN_SUB = 512
FFN_NOUT = 512
FFN_HALO = 16
FFN_X1_STEPS = 8
FFN_X1_ROWS = FFN_TM // FFN_X1_STEPS

F32 = jnp.float32
BF16 = jnp.bfloat16


def _dot(a, b):
    return jnp.dot(a, b, preferred_element_type=F32)


def _dot_nt(a, b):
    return lax.dot_general(a, b, (((1,), (1,)), ((), ())), preferred_element_type=F32)


def _rms(x):
    return x * lax.rsqrt(jnp.mean(x * x, axis=-1, keepdims=True) + EPS)


def _cparams(n_grid):
    return pltpu.CompilerParams(dimension_semantics=("arbitrary",) * n_grid,
                                vmem_limit_bytes=VMEM_LIMIT)


def _ada_kernel(c_ref, w_ref, b_ref, o_ref):
    c = c_ref[...]
    sc = c * (1.0 / (1.0 + jnp.exp(-c)))
    o_ref[...] = _dot(sc.astype(BF16), w_ref[...].astype(BF16)) + b_ref[...]


def _ada(c8, w_ada, b_ada, n):
    return pl.pallas_call(
        _ada_kernel,
        grid=(n // ADA_TN,),
        in_specs=[pl.BlockSpec((MOD_ROWS, D_MODEL), lambda j: (0, 0)),
                  pl.BlockSpec((D_MODEL, ADA_TN), lambda j: (0, j)),
                  pl.BlockSpec((1, ADA_TN), lambda j: (0, j))],
        out_specs=pl.BlockSpec((MOD_ROWS, ADA_TN), lambda j: (0, j)),
        out_shape=jax.ShapeDtypeStruct((MOD_ROWS, n), F32),
        compiler_params=_cparams(1),
        name="ada",
    )(c8, w_ada, b_ada)


def _rope_chunk(xc, cos, sin, lo_mask):
    up = pltpu.roll(xc, LANES - 16, 1)
    dn = pltpu.roll(xc, 16, 1)
    return xc * cos + jnp.where(lo_mask, up, dn) * sin


def _mixer_kernel(ctx_ref, x_ref, sh_ref, sc_ref, n1_ref, cos_ref, sin_ref, wt_ref,
                  qn_ref, kvn_ref, wuq_ref, wukv_ref, c8_ref, wa_ref, ba_ref,
                  qda_ref, kda_ref, vda_ref, qm_ref, km_ref, vm_ref, modl_ref):
    t = pl.program_id(1)
    is_ctx = t == 0
    c8 = c8_ref[...]
    sc8 = (c8 * (1.0 / (1.0 + jnp.exp(-c8)))).astype(BF16)
    mod_late = _dot(sc8, wa_ref[...].astype(BF16)) + ba_ref[...]
    xt = jnp.where(is_ctx, ctx_ref[0], x_ref[0])
    h = _rms(xt) * n1_ref[...]
    h = h * (1.0 + sc_ref[0]) + sh_ref[0]
    hb = h.astype(BF16)
    cos = jnp.where(is_ctx, 1.0, cos_ref[...])
    sin = jnp.where(is_ctx, 0.0, sin_ref[...])
    lane = lax.broadcasted_iota(jnp.int32, (TOK_TILE, LANES), 1)
    lo_mask = (lane % 32) < 16
    o1, o2, o3 = DA_WIDTH, 2 * DA_WIDTH, 3 * DA_WIDTH
    o4 = o3 + Q_RANK
    o5 = o4 + KV_RANK

    w_low = jnp.concatenate([wt_ref[o3:IN_WIDTH, :].astype(BF16),
                             jnp.zeros((LANES - MLA_ROPE, D_MODEL), BF16)], axis=0)
    low = _dot_nt(hb, w_low)
    k = _dot_nt(hb, wt_ref[o1:o2, :].astype(BF16))
    v = _dot_nt(hb, wt_ref[o2:o3, :].astype(BF16))
    q = _dot_nt(hb, wt_ref[0:o1, :].astype(BF16))
    ckv = _rms(low[:, Q_RANK:Q_RANK + KV_RANK]) * kvn_ref[...]
    cq = _rms(low[:, :Q_RANK]) * qn_ref[...]
    kv = _dot(ckv.astype(BF16), wukv_ref[...])
    qm = _dot(cq.astype(BF16), wuq_ref[...])

    for hh in range(DA_HEADS):
        sl = slice(hh * LANES, (hh + 1) * LANES)
        kda_ref[0, :, sl] = _rope_chunk(k[:, sl], cos, sin, lo_mask).astype(BF16)
    vda_ref[0] = v.astype(BF16)
    kr = _rope_chunk(low[:, Q_RANK + KV_RANK:], cos, sin, lo_mask).astype(BF16)
    for hh in range(MLA_HEADS):
        base = hh * MLA_QK_PAD
        km_ref[0, :, base:base + LANES] = kv[:, hh * LANES:(hh + 1) * LANES].astype(BF16)
        km_ref[0, :, base + LANES:base + 2 * LANES] = kr
    vm_ref[0] = kv[:, MLA_WIDTH:].astype(BF16)

    for hh in range(DA_HEADS):
        sl = slice(hh * LANES, (hh + 1) * LANES)
        qda_ref[0, :, sl] = (_rope_chunk(q[:, sl], cos, sin, lo_mask) * DA_SCALE).astype(BF16)
    for hh in range(MLA_HEADS):
        base = hh * MLA_QK_PAD
        qm_ref[0, :, base:base + LANES] = (qm[:, base:base + LANES] * MLA_SCALE).astype(BF16)
        qr = _rope_chunk(qm[:, base + LANES:base + 2 * LANES], cos, sin, lo_mask)
        qm_ref[0, :, base + LANES:base + 2 * LANES] = (qr * MLA_SCALE).astype(BF16)
    modl_ref[...] = mod_late


def _mixer_inputs(ctx, x, mod3, n1, cos_t, sin_t, w_in_t, qn, kvn, wuq_b, wukv_b, c8, w_ada, b_ada):
    B = x.shape[0]
    nt = T_ALL // TOK_TILE
    n_late = w_ada.shape[1] - MOD_EARLY
    assert n_late == B * (nt - 1) * MOD_LATE_TN

    def lat(t):
        return jnp.maximum(t - 1, 0)

    def late(b, t):
        return b * (nt - 1) + lat(t)

    def const(shape):
        return pl.BlockSpec(shape, lambda b, t: (0,) * len(shape), pipeline_mode=pl.Buffered(1))

    in_specs = [
        pl.BlockSpec((1, CTX_LEN, D_MODEL), lambda b, t: (b, 0, 0)),
        pl.BlockSpec((1, TOK_TILE, D_MODEL), lambda b, t: (b, lat(t), 0)),
        pl.BlockSpec((1, 1, D_MODEL), lambda b, t: (jnp.where(t == 0, CTX_ROW, b), 0, 0)),
        pl.BlockSpec((1, 1, D_MODEL), lambda b, t: (jnp.where(t == 0, CTX_ROW, b), 0, 1)),
        const((1, D_MODEL)),
        pl.BlockSpec((TOK_TILE, LANES), lambda b, t: (lat(t), 0)),
        pl.BlockSpec((TOK_TILE, LANES), lambda b, t: (lat(t), 0)),
        const(w_in_t.shape),
        const((1, Q_RANK)),
        const((1, KV_RANK)),
        const(wuq_b.shape),
        const(wukv_b.shape),
        const((MOD_ROWS, D_MODEL)),
        pl.BlockSpec((D_MODEL, MOD_LATE_TN), lambda b, t: (0, MOD_EARLY // MOD_LATE_TN + late(b, t))),
        pl.BlockSpec((1, MOD_LATE_TN), lambda b, t: (0, MOD_EARLY // MOD_LATE_TN + late(b, t))),
    ]
    out_specs = [
        pl.BlockSpec((1, TOK_TILE, DA_WIDTH), lambda b, t: (b, lat(t), 0)),
        pl.BlockSpec((1, TOK_TILE, DA_WIDTH), lambda b, t: (b, t, 0)),
        pl.BlockSpec((1, TOK_TILE, DA_WIDTH), lambda b, t: (b, t, 0)),
        pl.BlockSpec((1, TOK_TILE, MLA_HEADS * MLA_QK_PAD), lambda b, t: (b, lat(t), 0)),
        pl.BlockSpec((1, TOK_TILE, MLA_HEADS * MLA_QK_PAD), lambda b, t: (b, t, 0)),
        pl.BlockSpec((1, TOK_TILE, MLA_WIDTH), lambda b, t: (b, t, 0)),
        pl.BlockSpec((MOD_ROWS, MOD_LATE_TN), lambda b, t: (0, late(b, t))),
    ]
    out_shape = [
        jax.ShapeDtypeStruct((B, SEQ, DA_WIDTH), BF16),
        jax.ShapeDtypeStruct((B, T_ALL, DA_WIDTH), BF16),
        jax.ShapeDtypeStruct((B, T_ALL, DA_WIDTH), BF16),
        jax.ShapeDtypeStruct((B, SEQ, MLA_HEADS * MLA_QK_PAD), BF16),
        jax.ShapeDtypeStruct((B, T_ALL, MLA_HEADS * MLA_QK_PAD), BF16),
        jax.ShapeDtypeStruct((B, T_ALL, MLA_WIDTH), BF16),
        jax.ShapeDtypeStruct((MOD_ROWS, n_late), F32),
    ]
    return pl.pallas_call(
        _mixer_kernel,
        grid=(B, nt),
        in_specs=in_specs,
        out_specs=out_specs,
        out_shape=out_shape,
        compiler_params=_cparams(2),
        name="mixer_in",
    )(ctx, x, mod3, mod3, n1, cos_t, sin_t, w_in_t, qn, kvn, wuq_b, wukv_b, c8, w_ada, b_ada)


def _fill_values_with_ones(v_ref, vx_ref):
    width = vx_ref.shape[-1] // 2
    ones_lane = lax.broadcasted_iota(jnp.int32, (T_ALL, width), 1) == 0
    for hh in range(vx_ref.shape[0]):
        vx_ref[hh, :, 0:width] = v_ref[0, :, hh * width:(hh + 1) * width]
        vx_ref[hh, :, width:] = jnp.where(ones_lane, 1.0, 0.0).astype(BF16)


def _softmax_weighted(s, vx):
    width = vx.shape[-1] // 2
    p = jnp.exp2(s - jnp.max(s, axis=-1, keepdims=True))
    ox = _dot(p.astype(BF16), vx)
    return ox[:, 0:width] * (1.0 / ox[:, width:width + 1])


def _pipelined_tiles(scores, finish, n_heads, lookahead):
    n_tiles = SEQ // ATT_TQ
    ahead = [[scores(n, hh) for hh in range(n_heads)] for n in range(lookahead)]
    for n in range(n_tiles):
        if n + lookahead < n_tiles:
            ahead.append([scores(n + lookahead, hh) for hh in range(n_heads)])
        s = ahead.pop(0)
        for hh in range(n_heads):
            finish(n, hh, s[hh])


def _diff_attn_kernel(q_ref, k_ref, v_ref, lq1_ref, lk1_ref, lq2_ref, lk2_ref, sub_ref, o_ref,
                      vx_ref):
    lam = (jnp.exp(jnp.sum(lq1_ref[...] * lk1_ref[...], axis=-1, keepdims=True))
           - jnp.exp(jnp.sum(lq2_ref[...] * lk2_ref[...], axis=-1, keepdims=True))
           + LAMBDA_INIT)
    lane = lax.broadcasted_iota(jnp.int32, (ATT_TQ, HEAD_DIM), 1)
    _fill_values_with_ones(v_ref, vx_ref)

    def scores(n, hh):
        cols = slice(hh * HEAD_DIM, (hh + 1) * HEAD_DIM)
        q = q_ref[0, n * ATT_TQ:(n + 1) * ATT_TQ, cols]
        zero = jnp.zeros_like(q)
        q1 = jnp.where(lane < DA_HALF, q, zero)
        q2 = jnp.where(lane >= DA_HALF, q, zero)
        k = k_ref[0, :, cols]
        return _dot_nt(q1, k), _dot_nt(q2, k)

    def finish(n, hh, s):
        o = _softmax_weighted(s[0], vx_ref[hh]) - lam * _softmax_weighted(s[1], vx_ref[hh])
        o = _rms(o) * sub_ref[...] * (1.0 - LAMBDA_INIT)
        o_ref[0, n * ATT_TQ:(n + 1) * ATT_TQ, hh * HEAD_DIM:(hh + 1) * HEAD_DIM] = o.astype(BF16)

    _pipelined_tiles(scores, finish, DA_STEP_HEADS, DA_LOOKAHEAD)


def _diff_attn(q, k, v, lq1, lk1, lq2, lk2, subln):
    B = q.shape[0]
    small = lambda n: pl.BlockSpec((1, n), lambda b, h: (0, 0))
    width = DA_STEP_HEADS * HEAD_DIM
    return pl.pallas_call(
        _diff_attn_kernel,
        grid=(B, DA_HEADS // DA_STEP_HEADS),
        in_specs=[pl.BlockSpec((1, SEQ, width), lambda b, h: (b, 0, h)),
                  pl.BlockSpec((1, T_ALL, width), lambda b, h: (b, 0, h)),
                  pl.BlockSpec((1, T_ALL, width), lambda b, h: (b, 0, h)),
                  small(DA_HALF), small(DA_HALF), small(DA_HALF), small(DA_HALF),
                  small(HEAD_DIM)],
        out_specs=pl.BlockSpec((1, SEQ, width), lambda b, h: (b, 0, h)),
        out_shape=jax.ShapeDtypeStruct((B, SEQ, DA_WIDTH), BF16),
        scratch_shapes=[pltpu.VMEM((DA_STEP_HEADS, T_ALL, 2 * HEAD_DIM), BF16)],
        compiler_params=_cparams(2),
        name="diff_attn",
    )(q, k, v, lq1, lk1, lq2, lk2, subln)


def _mla_attn_kernel(q_ref, k_ref, v_ref, o_ref, vx_ref):
    _fill_values_with_ones(v_ref, vx_ref)

    def scores(n, hh):
        cols = slice(hh * MLA_QK_PAD, (hh + 1) * MLA_QK_PAD)
        return _dot_nt(q_ref[0, n * ATT_TQ:(n + 1) * ATT_TQ, cols], k_ref[0, :, cols])

    def finish(n, hh, s):
        o = _softmax_weighted(s, vx_ref[hh])
        o_ref[0, n * ATT_TQ:(n + 1) * ATT_TQ, hh * MLA_V:(hh + 1) * MLA_V] = o.astype(BF16)

    _pipelined_tiles(scores, finish, MLA_STEP_HEADS, MLA_LOOKAHEAD)


def _mla_attn(q, k, v):
    B = q.shape[0]
    return pl.pallas_call(
        _mla_attn_kernel,
        grid=(B, MLA_HEADS // MLA_STEP_HEADS),
        in_specs=[pl.BlockSpec((1, SEQ, MLA_STEP_HEADS * MLA_QK_PAD), lambda b, h: (b, 0, h)),
                  pl.BlockSpec((1, T_ALL, MLA_STEP_HEADS * MLA_QK_PAD), lambda b, h: (b, 0, h)),
                  pl.BlockSpec((1, T_ALL, MLA_STEP_HEADS * MLA_V), lambda b, h: (b, 0, h))],
        out_specs=pl.BlockSpec((1, SEQ, MLA_STEP_HEADS * MLA_V), lambda b, h: (b, 0, h)),
        out_shape=jax.ShapeDtypeStruct((B, SEQ, MLA_WIDTH), BF16),
        scratch_shapes=[pltpu.VMEM((MLA_STEP_HEADS, T_ALL, 2 * MLA_V), BF16)],
        compiler_params=_cparams(2),
        name="mla_attn",
    )(q, k, v)


def _out_proj_kernel(oda_ref, omla_ref, wo_ref, x_ref, g1_ref, sh2_ref, sc2_ref, n2_ref,
                     x1_ref, h2_ref):
    merged = jnp.concatenate([oda_ref[0], omla_ref[0]], axis=1)
    y = _dot(merged, wo_ref[...].astype(BF16))
    x1 = x_ref[0] + g1_ref[0] * y
    x1_ref[0] = x1
    h2 = _rms(x1) * n2_ref[...]
    h2_ref[0] = (h2 * (1.0 + sc2_ref[0]) + sh2_ref[0]).astype(BF16)


def _out_proj(o_da, o_mla, w_o, x, mod3, n2):
    B = x.shape[0]
    tile = lambda w: pl.BlockSpec((1, OUT_TM, w), lambda b, i: (b, i, 0))
    modrow = lambda col: pl.BlockSpec((1, 1, D_MODEL), lambda b, i: (b, 0, col))
    return pl.pallas_call(
        _out_proj_kernel,
        grid=(B, SEQ // OUT_TM),
        in_specs=[tile(DA_WIDTH), tile(MLA_WIDTH),
                  pl.BlockSpec(w_o.shape, lambda b, i: (0, 0), pipeline_mode=pl.Buffered(1)),
                  tile(D_MODEL), modrow(0), modrow(1), modrow(2),
                  pl.BlockSpec((1, D_MODEL), lambda b, i: (0, 0))],
        out_specs=[tile(D_MODEL), tile(D_MODEL)],
        out_shape=[jax.ShapeDtypeStruct((B, SEQ, D_MODEL), F32),
                   jax.ShapeDtypeStruct((B, SEQ, D_MODEL), BF16)],
        compiler_params=_cparams(2),
        name="out_proj",
    )(o_da, o_mla, w_o, x, mod3, mod3, mod3, n2)


def _ffn_kernel(h_ref, top_ref, bot_ref, wg_ref, wu_ref, cw_ref, cb_ref, wd_ref, x1_ref, g2_ref,
                fw_ref, o_ref, hs_ref):
    i = pl.program_id(1)
    j = pl.program_id(2)
    last_i = pl.num_programs(1) - 1
    last_j = pl.num_programs(2) - 1

    @pl.when(j == 0)
    def _():
        top = top_ref[0]
        bot = bot_ref[0]
        hs_ref[0:FFN_HALO, :] = jnp.where(i == 0, jnp.zeros_like(top), top)
        hs_ref[FFN_HALO:FFN_HALO + FFN_TM, :] = h_ref[0]
        hs_ref[FFN_HALO + FFN_TM:, :] = jnp.where(i == last_i, jnp.zeros_like(bot), bot)
        o_ref[0] = jnp.zeros((FFN_TM, D_MODEL), F32)

    @pl.when(j < FFN_X1_STEPS)
    def _():
        rows = pl.ds(pl.multiple_of(j * FFN_X1_ROWS, FFN_X1_ROWS), FFN_X1_ROWS)
        o_ref[0, rows, :] += x1_ref[0]

    lo = FFN_HALO
    g2 = g2_ref[0]
    for c in range(FFN_TF // FFN_SUB):
        cs = slice(c * FFN_SUB, (c + 1) * FFN_SUB)
        g = _dot(hs_ref[...], wg_ref[:, cs].astype(BF16))
        u = _dot(hs_ref[lo:lo + FFN_TM, :], wu_ref[:, cs].astype(BF16))
        g_prev = pltpu.roll(g, 1, 0)
        g_next = pltpu.roll(g, FFN_TM + 2 * FFN_HALO - 1, 0)
        gc = (g_prev[lo:lo + FFN_TM] * cw_ref[0:1, cs]
              + g[lo:lo + FFN_TM] * cw_ref[1:2, cs]
              + g_next[lo:lo + FFN_TM] * cw_ref[2:3, cs]
              + cb_ref[:, cs])
        act = (gc * (1.0 / (1.0 + jnp.exp(-gc))) * u).astype(BF16)
        for n in range(D_MODEL // FFN_NOUT):
            ns = slice(n * FFN_NOUT, (n + 1) * FFN_NOUT)
            o_ref[0, :, ns] += g2[:, ns] * _dot(act, wd_ref[cs, ns].astype(BF16))

    @pl.when(j == last_j)
    def _():
        o_ref[0] = _rms(o_ref[0]) * fw_ref[...]


def _conv_ffn_final(h2, w_up, conv_w, conv_b, w_down, x1, mod3, final_w):
    B = h2.shape[0]
    nf = D_FF // FFN_TF
    ni = SEQ // FFN_TM
    assert nf >= FFN_X1_STEPS
    halo_per_tile = FFN_TM // FFN_HALO
    n_halo = SEQ // FFN_HALO
    return pl.pallas_call(
        _ffn_kernel,
        grid=(B, ni, nf),
        in_specs=[
            pl.BlockSpec((1, FFN_TM, D_MODEL), lambda b, i, j: (b, i, 0),
                         pipeline_mode=pl.Buffered(1)),
            pl.BlockSpec((1, FFN_HALO, D_MODEL),
                         lambda b, i, j: (b, jnp.maximum(i * halo_per_tile - 1, 0), 0)),
            pl.BlockSpec((1, FFN_HALO, D_MODEL),
                         lambda b, i, j: (b, jnp.minimum((i + 1) * halo_per_tile, n_halo - 1), 0)),
            pl.BlockSpec((D_MODEL, FFN_TF), lambda b, i, j: (0, j)),
            pl.BlockSpec((D_MODEL, FFN_TF), lambda b, i, j: (0, nf + j)),
            pl.BlockSpec((CONV_W, FFN_TF), lambda b, i, j: (0, j)),
            pl.BlockSpec((1, FFN_TF), lambda b, i, j: (0, j)),
            pl.BlockSpec((FFN_TF, D_MODEL), lambda b, i, j: (j, 0)),
            pl.BlockSpec((1, FFN_X1_ROWS, D_MODEL),
                         lambda b, i, j: (b, i * FFN_X1_STEPS + jnp.minimum(j, FFN_X1_STEPS - 1), 0)),
            pl.BlockSpec((1, 1, D_MODEL), lambda b, i, j: (b, 0, 3)),
            pl.BlockSpec((1, D_MODEL), lambda b, i, j: (0, 0)),
        ],
        out_specs=pl.BlockSpec((1, FFN_TM, D_MODEL), lambda b, i, j: (b, i, 0)),
        out_shape=jax.ShapeDtypeStruct((B, SEQ, D_MODEL), F32),
        scratch_shapes=[pltpu.VMEM((FFN_TM + 2 * FFN_HALO, D_MODEL), BF16)],
        compiler_params=_cparams(3),
        name="conv_ffn",
    )(h2, h2, h2, w_up, w_up, conv_w, conv_b, w_down, x1, mod3, final_w)


def _rope_tables():
    pos = jnp.arange(SEQ)
    row = (pos // GRID_W).astype(F32)
    col = (pos % GRID_W).astype(F32)
    nf = ROPE_DIM // 4
    inv = ROPE_BASE ** (-jnp.arange(nf, dtype=F32) / nf)
    ar = row[:, None] * inv
    ac = col[:, None] * inv
    cos = jnp.concatenate([jnp.cos(ar), jnp.cos(ar), jnp.cos(ac), jnp.cos(ac)], axis=-1)
    sin = jnp.concatenate([-jnp.sin(ar), jnp.sin(ar), -jnp.sin(ac), jnp.sin(ac)], axis=-1)
    return jnp.tile(cos, (1, LANES // ROPE_DIM)), jnp.tile(sin, (1, LANES // ROPE_DIM))


def kernel(x, c, ctx, c_ctx, w_ada, b_ada, norm1_w, w_in, q_norm_w, kv_norm_w, w_uq, w_ukv,
           lambda_q1, lambda_k1, lambda_q2, lambda_k2, subln_w, w_o, norm2_w, w_up,
           conv_w, conv_b, w_down, final_w):
    B = x.shape[0]
    assert B <= CTX_ROW and x.shape == (B, SEQ, D_MODEL) and ctx.shape == (B, CTX_LEN, D_MODEL)
    assert w_in.shape[1:] == (D_MODEL, IN_WIDTH)
    l = 0
    c8 = jnp.concatenate([c, jnp.zeros((CTX_ROW - B, D_MODEL), F32), c_ctx[None, :],
                          jnp.zeros((MOD_ROWS - CTX_ROW - 1, D_MODEL), F32)], axis=0)
    mod_early = _ada(c8, w_ada[l], b_ada[l][None, :], MOD_EARLY)
    mod3 = mod_early.reshape(MOD_ROWS, 1, MOD_EARLY)

    w_in_t = jnp.swapaxes(w_in[l], 0, 1)
    wuq_b = jnp.pad(w_uq[l].reshape(Q_RANK, MLA_HEADS, MLA_QK),
                    ((0, 0), (0, 0), (0, MLA_QK_PAD - MLA_QK))
                    ).reshape(Q_RANK, MLA_HEADS * MLA_QK_PAD).astype(BF16)
    wukv3 = w_ukv[l].reshape(KV_RANK, MLA_HEADS, MLA_NOPE + MLA_V)
    wukv_b = jnp.concatenate([wukv3[:, :, :MLA_NOPE].reshape(KV_RANK, MLA_HEADS * MLA_NOPE),
                              wukv3[:, :, MLA_NOPE:].reshape(KV_RANK, MLA_WIDTH)],
                             axis=1).astype(BF16)
    cos_t, sin_t = _rope_tables()

    q_da, k_da, v_da, q_mla, k_mla, v_mla, mod_late = _mixer_inputs(
        ctx, x, mod3, norm1_w[l][None, :], cos_t, sin_t, w_in_t,
        q_norm_w[l][None, :], kv_norm_w[l][None, :], wuq_b, wukv_b, c8, w_ada[l], b_ada[l][None, :])
    mod3_late = mod_late.reshape(MOD_ROWS, 1, N_MOD * D_MODEL - MOD_EARLY)
    o_da = _diff_attn(q_da, k_da, v_da, lambda_q1[l][None, :], lambda_k1[l][None, :],
                      lambda_q2[l][None, :], lambda_k2[l][None, :], subln_w[l][None, :])
    o_mla = _mla_attn(q_mla, k_mla, v_mla)
    x1, h2 = _out_proj(o_da, o_mla, w_o[l], x, mod3_late, norm2_w[l][None, :])
    return _conv_ffn_final(h2, w_up[l], conv_w[l], conv_b[l][None, :],
                           w_down[l], x1, mod3_late, final_w[None, :])
```

```python
import math

import jax
import jax.numpy as jnp
from jax import lax
from jax.experimental import pallas as pl
from jax.experimental.pallas import tpu as pltpu

D_MODEL = 2048
SEQ = 2048
CTX_LEN = 256
GRID_W = 64
HEAD_DIM = 128
DA_HEADS = 8
DA_HALF = 64
MLA_HEADS = 8
MLA_NOPE = 128
MLA_ROPE = 64
MLA_V = 128
Q_RANK = 384
KV_RANK = 256
ROPE_DIM = 64
ROPE_BASE = 10000.0
D_FF = 5632
CONV_W = 3
N_MOD = 6
EPS = 1e-6
DA_WIDTH = DA_HEADS * HEAD_DIM
MLA_WIDTH = MLA_HEADS * MLA_V
MLA_QK = MLA_NOPE + MLA_ROPE
MLA_QK_PAD = 256
IN_WIDTH = 3 * DA_WIDTH + Q_RANK + KV_RANK + MLA_ROPE
LOG2E = math.log2(math.e)
DA_SCALE = LOG2E / math.sqrt(DA_HALF)
MLA_SCALE = LOG2E / math.sqrt(MLA_QK)
LAMBDA_INIT = 0.8 - 0.6 * math.exp(-0.3 * 0)
T_ALL = CTX_LEN + SEQ

LANES = 128
MOD_ROWS = 8
CTX_ROW = 4
VMEM_LIMIT = 60 * 1024 * 1024

TOK_TILE = 256
MOD_EARLY = 2 * D_MODEL
MOD_LATE_TN = 256
ADA_TN = 1024
ATT_TQ = 256
DA_LOOKAHEAD = 3
MLA_LOOKAHEAD = 1
DA_STEP_HEADS = 1
MLA_STEP_HEADS = 2
OUT_TM = 512
FFN_TM = 1024
FFN_TF = 512
FFN_SUB = 512
FFN_NOUT = 512
FFN_HALO = 16
FFN_X1_STEPS = 8
FFN_X1_ROWS = FFN_TM // FFN_X1_STEPS

F32 = jnp.float32
BF16 = jnp.bfloat16


def _dot(a, b):
    return jnp.dot(a, b, preferred_element_type=F32)


def _dot_nt(a, b):
    return lax.dot_general(a, b, (((1,), (1,)), ((), ())), preferred_element_type=F32)


def _rms(x):
    return x * lax.rsqrt(jnp.mean(x * x, axis=-1, keepdims=True) + EPS)


def _cparams(n_grid):
    return pltpu.CompilerParams(dimension_semantics=("arbitrary",) * n_grid,
                                vmem_limit_bytes=VMEM_LIMIT)


def _ada_kernel(c_ref, w_ref, b_ref, o_ref):
    c = c_ref[...]
    sc = c * (1.0 / (1.0 + jnp.exp(-c)))
    o_ref[...] = _dot(sc.astype(BF16), w_ref[...].astype(BF16)) + b_ref[...]


def _ada(c8, w_ada, b_ada, n):
    return pl.pallas_call(
        _ada_kernel,
        grid=(n // ADA_TN,),
        in_specs=[pl.BlockSpec((MOD_ROWS, D_MODEL), lambda j: (0, 0)),
                  pl.BlockSpec((D_MODEL, ADA_TN), lambda j: (0, j)),
                  pl.BlockSpec((1, ADA_TN), lambda j: (0, j))],
        out_specs=pl.BlockSpec((MOD_ROWS, ADA_TN), lambda j: (0, j)),
        out_shape=jax.ShapeDtypeStruct((MOD_ROWS, n), F32),
        compiler_params=_cparams(1),
        name="ada",
    )(c8, w_ada, b_ada)


def _rope_chunk(xc, cos, sin, lo_mask):
    up = pltpu.roll(xc, LANES - 16, 1)
    dn = pltpu.roll(xc, 16, 1)
    return xc * cos + jnp.where(lo_mask, up, dn) * sin


def _mixer_kernel(ctx_ref, x_ref, sh_ref, sc_ref, n1_ref, cos_ref, sin_ref, wt_ref,
                  qn_ref, kvn_ref, wuq_ref, wukv_ref, c8_ref, wa_ref, ba_ref,
                  qda_ref, kda_ref, vda_ref, qm_ref, km_ref, vm_ref, modl_ref):
    t = pl.program_id(1)
    is_ctx = t == 0
    c8 = c8_ref[...]
    sc8 = (c8 * (1.0 / (1.0 + jnp.exp(-c8)))).astype(BF16)
    mod_late = _dot(sc8, wa_ref[...].astype(BF16)) + ba_ref[...]
    xt = jnp.where(is_ctx, ctx_ref[0], x_ref[0])
    h = _rms(xt) * n1_ref[...]
    h = h * (1.0 + sc_ref[0]) + sh_ref[0]
    hb = h.astype(BF16)
    cos = jnp.where(is_ctx, 1.0, cos_ref[...])
    sin = jnp.where(is_ctx, 0.0, sin_ref[...])
    lane = lax.broadcasted_iota(jnp.int32, (TOK_TILE, LANES), 1)
    lo_mask = (lane % 32) < 16
    o1, o2, o3 = DA_WIDTH, 2 * DA_WIDTH, 3 * DA_WIDTH
    o4 = o3 + Q_RANK
    o5 = o4 + KV_RANK

    w_low = jnp.concatenate([wt_ref[o3:IN_WIDTH, :].astype(BF16),
                             jnp.zeros((LANES - MLA_ROPE, D_MODEL), BF16)], axis=0)
    low = _dot_nt(hb, w_low)
    k = _dot_nt(hb, wt_ref[o1:o2, :].astype(BF16))
    v = _dot_nt(hb, wt_ref[o2:o3, :].astype(BF16))
    q = _dot_nt(hb, wt_ref[0:o1, :].astype(BF16))
    ckv = _rms(low[:, Q_RANK:Q_RANK + KV_RANK]) * kvn_ref[...]
    cq = _rms(low[:, :Q_RANK]) * qn_ref[...]
    kv = _dot(ckv.astype(BF16), wukv_ref[...])
    qm = _dot(cq.astype(BF16), wuq_ref[...])

    for hh in range(DA_HEADS):
        sl = slice(hh * LANES, (hh + 1) * LANES)
        kda_ref[0, :, sl] = _rope_chunk(k[:, sl], cos, sin, lo_mask).astype(BF16)
    vda_ref[0] = v.astype(BF16)
    kr = _rope_chunk(low[:, Q_RANK + KV_RANK:], cos, sin, lo_mask).astype(BF16)
    for hh in range(MLA_HEADS):
        base = hh * MLA_QK_PAD
        km_ref[0, :, base:base + LANES] = kv[:, hh * LANES:(hh + 1) * LANES].astype(BF16)
        km_ref[0, :, base + LANES:base + 2 * LANES] = kr
    vm_ref[0] = kv[:, MLA_WIDTH:].astype(BF16)

    for hh in range(DA_HEADS):
        sl = slice(hh * LANES, (hh + 1) * LANES)
        qda_ref[0, :, sl] = (_rope_chunk(q[:, sl], cos, sin, lo_mask) * DA_SCALE).astype(BF16)
    for hh in range(MLA_HEADS):
        base = hh * MLA_QK_PAD
        qm_ref[0, :, base:base + LANES] = (qm[:, base:base + LANES] * MLA_SCALE).astype(BF16)
        qr = _rope_chunk(qm[:, base + LANES:base + 2 * LANES], cos, sin, lo_mask)
        qm_ref[0, :, base + LANES:base + 2 * LANES] = (qr * MLA_SCALE).astype(BF16)
    modl_ref[...] = mod_late


def _mixer_inputs(ctx, x, mod3, n1, cos_t, sin_t, w_in_t, qn, kvn, wuq_b, wukv_b, c8, w_ada, b_ada):
    B = x.shape[0]
    nt = T_ALL // TOK_TILE
    n_late = w_ada.shape[1] - MOD_EARLY
    assert n_late == B * (nt - 1) * MOD_LATE_TN

    def lat(t):
        return jnp.maximum(t - 1, 0)

    def late(b, t):
        return b * (nt - 1) + lat(t)

    def const(shape):
        return pl.BlockSpec(shape, lambda b, t: (0,) * len(shape), pipeline_mode=pl.Buffered(1))

    in_specs = [
        pl.BlockSpec((1, CTX_LEN, D_MODEL), lambda b, t: (b, 0, 0)),
        pl.BlockSpec((1, TOK_TILE, D_MODEL), lambda b, t: (b, lat(t), 0)),
        pl.BlockSpec((1, 1, D_MODEL), lambda b, t: (jnp.where(t == 0, CTX_ROW, b), 0, 0)),
        pl.BlockSpec((1, 1, D_MODEL), lambda b, t: (jnp.where(t == 0, CTX_ROW, b), 0, 1)),
        const((1, D_MODEL)),
        pl.BlockSpec((TOK_TILE, LANES), lambda b, t: (lat(t), 0)),
        pl.BlockSpec((TOK_TILE, LANES), lambda b, t: (lat(t), 0)),
        const(w_in_t.shape),
        const((1, Q_RANK)),
        const((1, KV_RANK)),
        const(wuq_b.shape),
        const(wukv_b.shape),
        const((MOD_ROWS, D_MODEL)),
        pl.BlockSpec((D_MODEL, MOD_LATE_TN), lambda b, t: (0, MOD_EARLY // MOD_LATE_TN + late(b, t))),
        pl.BlockSpec((1, MOD_LATE_TN), lambda b, t: (0, MOD_EARLY // MOD_LATE_TN + late(b, t))),
    ]
    out_specs = [
        pl.BlockSpec((1, TOK_TILE, DA_WIDTH), lambda b, t: (b, lat(t), 0)),
        pl.BlockSpec((1, TOK_TILE, DA_WIDTH), lambda b, t: (b, t, 0)),
        pl.BlockSpec((1, TOK_TILE, DA_WIDTH), lambda b, t: (b, t, 0)),
        pl.BlockSpec((1, TOK_TILE, MLA_HEADS * MLA_QK_PAD), lambda b, t: (b, lat(t), 0)),
        pl.BlockSpec((1, TOK_TILE, MLA_HEADS * MLA_QK_PAD), lambda b, t: (b, t, 0)),
        pl.BlockSpec((1, TOK_TILE, MLA_WIDTH), lambda b, t: (b, t, 0)),
        pl.BlockSpec((MOD_ROWS, MOD_LATE_TN), lambda b, t: (0, late(b, t))),
    ]
    out_shape = [
        jax.ShapeDtypeStruct((B, SEQ, DA_WIDTH), BF16),
        jax.ShapeDtypeStruct((B, T_ALL, DA_WIDTH), BF16),
        jax.ShapeDtypeStruct((B, T_ALL, DA_WIDTH), BF16),
        jax.ShapeDtypeStruct((B, SEQ, MLA_HEADS * MLA_QK_PAD), BF16),
        jax.ShapeDtypeStruct((B, T_ALL, MLA_HEADS * MLA_QK_PAD), BF16),
        jax.ShapeDtypeStruct((B, T_ALL, MLA_WIDTH), BF16),
        jax.ShapeDtypeStruct((MOD_ROWS, n_late), F32),
    ]
    return pl.pallas_call(
        _mixer_kernel,
        grid=(B, nt),
        in_specs=in_specs,
        out_specs=out_specs,
        out_shape=out_shape,
        compiler_params=_cparams(2),
        name="mixer_in",
    )(ctx, x, mod3, mod3, n1, cos_t, sin_t, w_in_t, qn, kvn, wuq_b, wukv_b, c8, w_ada, b_ada)


def _fill_values_with_ones(v_ref, vx_ref):
    width = vx_ref.shape[-1] // 2
    ones_lane = lax.broadcasted_iota(jnp.int32, (T_ALL, width), 1) == 0
    for hh in range(vx_ref.shape[0]):
        vx_ref[hh, :, 0:width] = v_ref[0, :, hh * width:(hh + 1) * width]
        vx_ref[hh, :, width:] = jnp.where(ones_lane, 1.0, 0.0).astype(BF16)


def _softmax_weighted(s, vx):
    width = vx.shape[-1] // 2
    p = jnp.exp2(s - jnp.max(s, axis=-1, keepdims=True))
    ox = _dot(p.astype(BF16), vx)
    return ox[:, 0:width] * (1.0 / ox[:, width:width + 1])


def _pipelined_tiles(scores, finish, n_heads, lookahead):
    n_tiles = SEQ // ATT_TQ
    ahead = [[scores(n, hh) for hh in range(n_heads)] for n in range(lookahead)]
    for n in range(n_tiles):
        if n + lookahead < n_tiles:
            ahead.append([scores(n + lookahead, hh) for hh in range(n_heads)])
        s = ahead.pop(0)
        for hh in range(n_heads):
            finish(n, hh, s[hh])


def _diff_attn_kernel(q_ref, k_ref, v_ref, lq1_ref, lk1_ref, lq2_ref, lk2_ref, sub_ref, o_ref,
                      vx_ref):
    lam = (jnp.exp(jnp.sum(lq1_ref[...] * lk1_ref[...], axis=-1, keepdims=True))
           - jnp.exp(jnp.sum(lq2_ref[...] * lk2_ref[...], axis=-1, keepdims=True))
           + LAMBDA_INIT)
    lane = lax.broadcasted_iota(jnp.int32, (ATT_TQ, HEAD_DIM), 1)
    _fill_values_with_ones(v_ref, vx_ref)

    def scores(n, hh):
        cols = slice(hh * HEAD_DIM, (hh + 1) * HEAD_DIM)
        q = q_ref[0, n * ATT_TQ:(n + 1) * ATT_TQ, cols]
        zero = jnp.zeros_like(q)
        q1 = jnp.where(lane < DA_HALF, q, zero)
        q2 = jnp.where(lane >= DA_HALF, q, zero)
        k = k_ref[0, :, cols]
        return _dot_nt(q1, k), _dot_nt(q2, k)

    def finish(n, hh, s):
        o = _softmax_weighted(s[0], vx_ref[hh]) - lam * _softmax_weighted(s[1], vx_ref[hh])
        o = _rms(o) * sub_ref[...] * (1.0 - LAMBDA_INIT)
        o_ref[0, n * ATT_TQ:(n + 1) * ATT_TQ, hh * HEAD_DIM:(hh + 1) * HEAD_DIM] = o.astype(BF16)

    _pipelined_tiles(scores, finish, DA_STEP_HEADS, DA_LOOKAHEAD)


def _diff_attn(q, k, v, lq1, lk1, lq2, lk2, subln):
    B = q.shape[0]
    small = lambda n: pl.BlockSpec((1, n), lambda b, h: (0, 0))
    width = DA_STEP_HEADS * HEAD_DIM
    return pl.pallas_call(
        _diff_attn_kernel,
        grid=(B, DA_HEADS // DA_STEP_HEADS),
        in_specs=[pl.BlockSpec((1, SEQ, width), lambda b, h: (b, 0, h)),
                  pl.BlockSpec((1, T_ALL, width), lambda b, h: (b, 0, h)),
                  pl.BlockSpec((1, T_ALL, width), lambda b, h: (b, 0, h)),
                  small(DA_HALF), small(DA_HALF), small(DA_HALF), small(DA_HALF),
                  small(HEAD_DIM)],
        out_specs=pl.BlockSpec((1, SEQ, width), lambda b, h: (b, 0, h)),
        out_shape=jax.ShapeDtypeStruct((B, SEQ, DA_WIDTH), BF16),
        scratch_shapes=[pltpu.VMEM((DA_STEP_HEADS, T_ALL, 2 * HEAD_DIM), BF16)],
        compiler_params=_cparams(2),
        name="diff_attn",
    )(q, k, v, lq1, lk1, lq2, lk2, subln)


def _mla_attn_kernel(q_ref, k_ref, v_ref, o_ref, vx_ref):
    _fill_values_with_ones(v_ref, vx_ref)

    def scores(n, hh):
        cols = slice(hh * MLA_QK_PAD, (hh + 1) * MLA_QK_PAD)
        return _dot_nt(q_ref[0, n * ATT_TQ:(n + 1) * ATT_TQ, cols], k_ref[0, :, cols])

    def finish(n, hh, s):
        o = _softmax_weighted(s, vx_ref[hh])
        o_ref[0, n * ATT_TQ:(n + 1) * ATT_TQ, hh * MLA_V:(hh + 1) * MLA_V] = o.astype(BF16)

    _pipelined_tiles(scores, finish, MLA_STEP_HEADS, MLA_LOOKAHEAD)


def _mla_attn(q, k, v):
    B = q.shape[0]
    return pl.pallas_call(
        _mla_attn_kernel,
        grid=(B, MLA_HEADS // MLA_STEP_HEADS),
        in_specs=[pl.BlockSpec((1, SEQ, MLA_STEP_HEADS * MLA_QK_PAD), lambda b, h: (b, 0, h)),
                  pl.BlockSpec((1, T_ALL, MLA_STEP_HEADS * MLA_QK_PAD), lambda b, h: (b, 0, h)),
                  pl.BlockSpec((1, T_ALL, MLA_STEP_HEADS * MLA_V), lambda b, h: (b, 0, h))],
        out_specs=pl.BlockSpec((1, SEQ, MLA_STEP_HEADS * MLA_V), lambda b, h: (b, 0, h)),
        out_shape=jax.ShapeDtypeStruct((B, SEQ, MLA_WIDTH), BF16),
        scratch_shapes=[pltpu.VMEM((MLA_STEP_HEADS, T_ALL, 2 * MLA_V), BF16)],
        compiler_params=_cparams(2),
        name="mla_attn",
    )(q, k, v)


def _out_proj_kernel(oda_ref, omla_ref, wo_ref, x_ref, g1_ref, sh2_ref, sc2_ref, n2_ref,
                     x1_ref, h2_ref):
    merged = jnp.concatenate([oda_ref[0], omla_ref[0]], axis=1)
    y = _dot(merged, wo_ref[...].astype(BF16))
    x1 = x_ref[0] + g1_ref[0] * y
    x1_ref[0] = x1
    h2 = _rms(x1) * n2_ref[...]
    h2_ref[0] = (h2 * (1.0 + sc2_ref[0]) + sh2_ref[0]).astype(BF16)


def _out_proj(o_da, o_mla, w_o, x, mod3, n2):
    B = x.shape[0]
    tile = lambda w: pl.BlockSpec((1, OUT_TM, w), lambda b, i: (b, i, 0))
    modrow = lambda col: pl.BlockSpec((1, 1, D_MODEL), lambda b, i: (b, 0, col))
    return pl.pallas_call(
        _out_proj_kernel,
        grid=(B, SEQ // OUT_TM),
        in_specs=[tile(DA_WIDTH), tile(MLA_WIDTH),
                  pl.BlockSpec(w_o.shape, lambda b, i: (0, 0), pipeline_mode=pl.Buffered(1)),
                  tile(D_MODEL), modrow(0), modrow(1), modrow(2),
                  pl.BlockSpec((1, D_MODEL), lambda b, i: (0, 0))],
        out_specs=[tile(D_MODEL), tile(D_MODEL)],
        out_shape=[jax.ShapeDtypeStruct((B, SEQ, D_MODEL), F32),
                   jax.ShapeDtypeStruct((B, SEQ, D_MODEL), BF16)],
        compiler_params=_cparams(2),
        name="out_proj",
    )(o_da, o_mla, w_o, x, mod3, mod3, mod3, n2)


def _ffn_kernel(h_ref, top_ref, bot_ref, wg_ref, wu_ref, cw_ref, cb_ref, wd_ref, x1_ref, g2_ref,
                fw_ref, o_ref, hs_ref):
    i = pl.program_id(1)
    j = pl.program_id(2)
    last_i = pl.num_programs(1) - 1
    last_j = pl.num_programs(2) - 1

    @pl.when(j == 0)
    def _():
        top = top_ref[0]
        bot = bot_ref[0]
        hs_ref[0:FFN_HALO, :] = jnp.where(i == 0, jnp.zeros_like(top), top)
        hs_ref[FFN_HALO:FFN_HALO + FFN_TM, :] = h_ref[0]
        hs_ref[FFN_HALO + FFN_TM:, :] = jnp.where(i == last_i, jnp.zeros_like(bot), bot)
        o_ref[0] = jnp.zeros((FFN_TM, D_MODEL), F32)

    @pl.when(j < FFN_X1_STEPS)
    def _():
        rows = pl.ds(pl.multiple_of(j * FFN_X1_ROWS, FFN_X1_ROWS), FFN_X1_ROWS)
        o_ref[0, rows, :] += x1_ref[0]

    lo = FFN_HALO
    g2 = g2_ref[0]
    for c in range(FFN_TF // FFN_SUB):
        cs = slice(c * FFN_SUB, (c + 1) * FFN_SUB)
        g = _dot(hs_ref[...], wg_ref[:, cs].astype(BF16))
        u = _dot(hs_ref[lo:lo + FFN_TM, :], wu_ref[:, cs].astype(BF16))
        g_prev = pltpu.roll(g, 1, 0)
        g_next = pltpu.roll(g, FFN_TM + 2 * FFN_HALO - 1, 0)
        gc = (g_prev[lo:lo + FFN_TM] * cw_ref[0:1, cs]
              + g[lo:lo + FFN_TM] * cw_ref[1:2, cs]
              + g_next[lo:lo + FFN_TM] * cw_ref[2:3, cs]
              + cb_ref[:, cs])
        act = (gc * (1.0 / (1.0 + jnp.exp(-gc))) * u).astype(BF16)
        for n in range(D_MODEL // FFN_NOUT):
            ns = slice(n * FFN_NOUT, (n + 1) * FFN_NOUT)
            o_ref[0, :, ns] += g2[:, ns] * _dot(act, wd_ref[cs, ns].astype(BF16))

    @pl.when(j == last_j)
    def _():
        o_ref[0] = _rms(o_ref[0]) * fw_ref[...]


def _conv_ffn_final(h2, w_up, conv_w, conv_b, w_down, x1, mod3, final_w):
    B = h2.shape[0]
    nf = D_FF // FFN_TF
    ni = SEQ // FFN_TM
    assert nf >= FFN_X1_STEPS
    halo_per_tile = FFN_TM // FFN_HALO
    n_halo = SEQ // FFN_HALO
    return pl.pallas_call(
        _ffn_kernel,
        grid=(B, ni, nf),
        in_specs=[
            pl.BlockSpec((1, FFN_TM, D_MODEL), lambda b, i, j: (b, i, 0),
                         pipeline_mode=pl.Buffered(1)),
            pl.BlockSpec((1, FFN_HALO, D_MODEL),
                         lambda b, i, j: (b, jnp.maximum(i * halo_per_tile - 1, 0), 0)),
            pl.BlockSpec((1, FFN_HALO, D_MODEL),
                         lambda b, i, j: (b, jnp.minimum((i + 1) * halo_per_tile, n_halo - 1), 0)),
            pl.BlockSpec((D_MODEL, FFN_TF), lambda b, i, j: (0, j)),
            pl.BlockSpec((D_MODEL, FFN_TF), lambda b, i, j: (0, nf + j)),
            pl.BlockSpec((CONV_W, FFN_TF), lambda b, i, j: (0, j)),
            pl.BlockSpec((1, FFN_TF), lambda b, i, j: (0, j)),
            pl.BlockSpec((FFN_TF, D_MODEL), lambda b, i, j: (j, 0)),
            pl.BlockSpec((1, FFN_X1_ROWS, D_MODEL),
                         lambda b, i, j: (b, i * FFN_X1_STEPS + jnp.minimum(j, FFN_X1_STEPS - 1), 0)),
            pl.BlockSpec((1, 1, D_MODEL), lambda b, i, j: (b, 0, 3)),
            pl.BlockSpec((1, D_MODEL), lambda b, i, j: (0, 0)),
        ],
        out_specs=pl.BlockSpec((1, FFN_TM, D_MODEL), lambda b, i, j: (b, i, 0)),
        out_shape=jax.ShapeDtypeStruct((B, SEQ, D_MODEL), F32),
        scratch_shapes=[pltpu.VMEM((FFN_TM + 2 * FFN_HALO, D_MODEL), BF16)],
        compiler_params=_cparams(3),
        name="conv_ffn",
    )(h2, h2, h2, w_up, w_up, conv_w, conv_b, w_down, x1, mod3, final_w)


def _rope_tables():
    pos = jnp.arange(SEQ)
    row = (pos // GRID_W).astype(F32)
    col = (pos % GRID_W).astype(F32)
    nf = ROPE_DIM // 4
    inv = ROPE_BASE ** (-jnp.arange(nf, dtype=F32) / nf)
    ar = row[:, None] * inv
    ac = col[:, None] * inv
    cos = jnp.concatenate([jnp.cos(ar), jnp.cos(ar), jnp.cos(ac), jnp.cos(ac)], axis=-1)
    sin = jnp.concatenate([-jnp.sin(ar), jnp.sin(ar), -jnp.sin(ac), jnp.sin(ac)], axis=-1)
    return jnp.tile(cos, (1, LANES // ROPE_DIM)), jnp.tile(sin, (1, LANES // ROPE_DIM))


def kernel(x, c, ctx, c_ctx, w_ada, b_ada, norm1_w, w_in, q_norm_w, kv_norm_w, w_uq, w_ukv,
           lambda_q1, lambda_k1, lambda_q2, lambda_k2, subln_w, w_o, norm2_w, w_up,
           conv_w, conv_b, w_down, final_w):
    B = x.shape[0]
    assert B <= CTX_ROW and x.shape == (B, SEQ, D_MODEL) and ctx.shape == (B, CTX_LEN, D_MODEL)
    assert w_in.shape[1:] == (D_MODEL, IN_WIDTH)
    l = 0
    c8 = jnp.concatenate([c, jnp.zeros((CTX_ROW - B, D_MODEL), F32), c_ctx[None, :],
                          jnp.zeros((MOD_ROWS - CTX_ROW - 1, D_MODEL), F32)], axis=0)
    mod_early = _ada(c8, w_ada[l], b_ada[l][None, :], MOD_EARLY)
    mod3 = mod_early.reshape(MOD_ROWS, 1, MOD_EARLY)

    w_in_t = jnp.swapaxes(w_in[l], 0, 1)
    wuq_b = jnp.pad(w_uq[l].reshape(Q_RANK, MLA_HEADS, MLA_QK),
                    ((0, 0), (0, 0), (0, MLA_QK_PAD - MLA_QK))
                    ).reshape(Q_RANK, MLA_HEADS * MLA_QK_PAD).astype(BF16)
    wukv3 = w_ukv[l].reshape(KV_RANK, MLA_HEADS, MLA_NOPE + MLA_V)
    wukv_b = jnp.concatenate([wukv3[:, :, :MLA_NOPE].reshape(KV_RANK, MLA_HEADS * MLA_NOPE),
                              wukv3[:, :, MLA_NOPE:].reshape(KV_RANK, MLA_WIDTH)],
                             axis=1).astype(BF16)
    cos_t, sin_t = _rope_tables()

    q_da, k_da, v_da, q_mla, k_mla, v_mla, mod_late = _mixer_inputs(
        ctx, x, mod3, norm1_w[l][None, :], cos_t, sin_t, w_in_t,
        q_norm_w[l][None, :], kv_norm_w[l][None, :], wuq_b, wukv_b, c8, w_ada[l], b_ada[l][None, :])
    mod3_late = mod_late.reshape(MOD_ROWS, 1, N_MOD * D_MODEL - MOD_EARLY)
    o_da = _diff_attn(q_da, k_da, v_da, lambda_q1[l][None, :], lambda_k1[l][None, :],
                      lambda_q2[l][None, :], lambda_k2[l][None, :], subln_w[l][None, :])
    o_mla = _mla_attn(q_mla, k_mla, v_mla)
    x1, h2 = _out_proj(o_da, o_mla, w_o[l], x, mod3_late, norm2_w[l][None, :])
    return _conv_ffn_final(h2, w_up[l], conv_w[l], conv_b[l][None, :],
                           w_down[l], x1, mod3_late, final_w[None, :])
```
